```python
import jax, jax.numpy as jnp
from jax import lax
import numpy as np

D_MODEL = 1024
BATCH = 8
SEQ = 2048
DEPTH = 4

N_MIXERS = 4
HEAD_DIM = 128
CHUNK = 64
GLA_HEADS = 4
GLA_DK = D_MODEL // 2 // GLA_HEADS
GLA_DV = D_MODEL // GLA_HEADS
GLA_GATE_RANK = 16
GLA_GATE_NORMALIZER = 16.0
GLA_COLS = (GLA_HEADS * GLA_DK, GLA_HEADS * GLA_DK, GLA_HEADS * GLA_DV, GLA_HEADS * GLA_DV, GLA_GATE_RANK)
MOBA_HEADS = D_MODEL // HEAD_DIM
MOBA_BLOCK = 256
MOBA_TOPK = 3
MOBA_QCHUNK = 16
MOBA_COLS = (D_MODEL, D_MODEL, D_MODEL)
ROPE_THETA = 500000.0
ROPE_DIM = HEAD_DIM // 4
NEG_INF = -1e30
GDN_HEADS = D_MODEL // HEAD_DIM
GDN_DK = HEAD_DIM
GDN_WIDTH = GDN_HEADS * GDN_DK
GDN_CONV = 4
GDN_COLS = (3 * GDN_WIDTH, GDN_WIDTH, GDN_HEADS, GDN_HEADS)
HGRN_HEADS = 8
HGRN_F_DIM = 128
HGRN_F = HGRN_HEADS * HGRN_F_DIM
HGRN_COLS = (HGRN_F, HGRN_F, D_MODEL, D_MODEL)
FFN_HIDDEN = -(-8 * D_MODEL // (3 * 256)) * 256
DEEPNORM_ALPHA = (2.0 * DEPTH) ** 0.25
DEEPNORM_BETA = (8.0 * DEPTH) ** -0.25
N_GLA = len(range(0, DEPTH, N_MIXERS))
N_MOBA = len(range(1, DEPTH, N_MIXERS))
N_GDN = len(range(2, DEPTH, N_MIXERS))
N_HGRN = len(range(3, DEPTH, N_MIXERS))

kernel_name = "hybrid_gla_moba_gdn_hgrn2_deepnorm"


def split_cols(x, cols):
    return jnp.split(x, [int(c) for c in np.cumsum(cols)[:-1]], axis=-1)


def to_heads(x, h):
    b, s, _ = x.shape
    return x.reshape(b, s, h, -1).transpose(0, 2, 1, 3)


def from_heads(x):
    b, h, s, d = x.shape
    return x.transpose(0, 2, 1, 3).reshape(b, s, h * d)


def layer_norm(x, g, b, eps=1e-5):
    xf = x.astype(jnp.float32)
    mu = jnp.mean(xf, -1, keepdims=True)
    var = jnp.mean(jnp.square(xf - mu), -1, keepdims=True)
    return ((xf - mu) * lax.rsqrt(var + eps) * g + b).astype(x.dtype)


def rms_norm(x, g, eps=1e-6):
    xf = x.astype(jnp.float32)
    return (xf * lax.rsqrt(jnp.mean(xf * xf, -1, keepdims=True) + eps) * g).astype(x.dtype)


def l2_normalize(x, eps=1e-6):
    xf = x.astype(jnp.float32)
    return (xf * lax.rsqrt(jnp.sum(xf * xf, -1, keepdims=True) + eps)).astype(x.dtype)


def partial_rope(x, positions):
    inv_freq = ROPE_THETA ** (-jnp.arange(0, ROPE_DIM, 2, dtype=jnp.float32) / ROPE_DIM)
    ang = positions.astype(jnp.float32)[:, None, :, None] * inv_freq
    cos, sin = jnp.cos(ang), jnp.sin(ang)
    xr = x[..., :ROPE_DIM].astype(jnp.float32)
    x1, x2 = xr[..., :ROPE_DIM // 2], xr[..., ROPE_DIM // 2:]
    rot = jnp.concatenate([x1 * cos - x2 * sin, x2 * cos + x1 * sin], -1).astype(x.dtype)
    return jnp.concatenate([rot, x[..., ROPE_DIM:]], -1)


def causal_depthwise_conv(x, w):
    k, c = w.shape
    return lax.conv_general_dilated(x, w.reshape(k, 1, c).astype(x.dtype), window_strides=(1,),
                                    padding=((k - 1, 0),), dimension_numbers=("NWC", "WIO", "NWC"),
                                    feature_group_count=c)


def swiglu_ffn(x, w_gu, w_down):
    gate, up = jnp.split(x @ w_gu, 2, axis=-1)
    return (jax.nn.silu(gate) * up) @ w_down


def chunk_gla(q, k, v, log_g):
    b, h, s, dk = q.shape
    dv = v.shape[-1]
    n = s // CHUNK
    f32 = jnp.float32
    q, k, log_g = (t.astype(f32).reshape(b, h, n, CHUNK, dk) for t in (q, k, log_g))
    v = v.astype(f32).reshape(b, h, n, CHUNK, dv)
    cum = jnp.cumsum(log_g, axis=3)
    last = cum[:, :, :, -1:, :]
    q_dec = q * jnp.exp(cum)
    k_inv = k * jnp.exp(-cum)
    k_to_end = k * jnp.exp(last - cum)
    causal = jnp.tril(jnp.ones((CHUNK, CHUNK), bool))
    scores = jnp.where(causal, jnp.einsum("bhnik,bhnjk->bhnij", q_dec, k_inv), 0.0)
    o_intra = jnp.einsum("bhnij,bhnjv->bhniv", scores, v)
    upd = jnp.einsum("bhnjk,bhnjv->bhnkv", k_to_end, v)
    decay = jnp.exp(last[:, :, :, 0, :])

    def step(state, inp):
        dec, u = inp
        return state * dec[..., None] + u, state

    _, states = lax.scan(step, jnp.zeros((b, h, dk, dv), f32),
                         (jnp.moveaxis(decay, 2, 0), jnp.moveaxis(upd, 2, 0)))
    states = jnp.moveaxis(states, 0, 2)
    o_inter = jnp.einsum("bhnik,bhnkv->bhniv", q_dec, states)
    return (o_intra + o_inter).reshape(b, h, s, dv)


def chunk_gated_delta(q, k, v, g, beta):
    b, h, s, dk = q.shape
    dv = v.shape[-1]
    n = s // CHUNK
    f32 = jnp.float32
    q = q.astype(f32).reshape(b, h, n, CHUNK, dk)
    k = k.astype(f32).reshape(b, h, n, CHUNK, dk)
    v = v.astype(f32).reshape(b, h, n, CHUNK, dv)
    g = g.astype(f32).reshape(b, h, n, CHUNK)
    beta = beta.astype(f32).reshape(b, h, n, CHUNK)
    cum = jnp.cumsum(g, axis=-1)
    incl = jnp.tril(jnp.ones((CHUNK, CHUNK), bool))
    strict = jnp.tril(jnp.ones((CHUNK, CHUNK), bool), -1)
    diff = cum[..., :, None] - cum[..., None, :]
    decay = jnp.where(incl, jnp.exp(jnp.where(incl, diff, 0.0)), 0.0)
    k_beta = k * beta[..., None]
    lower = jnp.where(strict, jnp.einsum("bhnid,bhnjd->bhnij", k_beta, k) * decay, 0.0)
    eye = jnp.eye(CHUNK, dtype=f32)
    rhs = jnp.concatenate([v * beta[..., None], k_beta * jnp.exp(cum)[..., None]], axis=-1)
    sol = lax.linalg.triangular_solve(lower + eye, rhs, left_side=True, lower=True)
    u, w = sol[..., :dv], sol[..., dv:]
    qk = jnp.einsum("bhnid,bhnjd->bhnij", q, k) * decay
    q_dec = q * jnp.exp(cum)[..., None]
    k_end = k * jnp.exp(cum[..., -1:] - cum)[..., None]
    chunk_decay = jnp.exp(cum[..., -1])

    def step(state, inp):
        u_c, w_c, qk_c, qd_c, ke_c, dec_c = inp
        v_new = u_c - jnp.einsum("bhck,bhkv->bhcv", w_c, state)
        o_c = jnp.einsum("bhck,bhkv->bhcv", qd_c, state) + jnp.einsum("bhij,bhjv->bhiv", qk_c, v_new)
        state = state * dec_c[..., None, None] + jnp.einsum("bhck,bhcv->bhkv", ke_c, v_new)
        return state, o_c

    xs = tuple(jnp.moveaxis(t, 2, 0) for t in (u, w, qk, q_dec, k_end, chunk_decay))
    _, o = lax.scan(step, jnp.zeros((b, h, dk, dv), f32), xs)
    return jnp.moveaxis(o, 0, 2).reshape(b, h, s, dv)


def moba_attention(q, k, v):
    b, h, s, d = q.shape
    nb = -(-s // MOBA_BLOCK)
    s_pad = nb * MOBA_BLOCK
    pad = ((0, 0), (0, 0), (0, s_pad - s), (0, 0))
    q, k, v = (jnp.pad(t, pad) for t in (q, k, v))
    scale = HEAD_DIM ** -0.5
    k_blocks = k.reshape(b, h, nb, MOBA_BLOCK, d)
    v_blocks = v.reshape(b, h, nb, MOBA_BLOCK, d)
    k_mean = jnp.mean(k_blocks.astype(jnp.float32), axis=3)
    gate = jnp.einsum("bhsd,bhnd->bhsn", q.astype(jnp.float32), k_mean)
    q_block = jnp.arange(s_pad) // MOBA_BLOCK
    past = jnp.arange(nb)[None, :] < q_block[:, None]
    gate = jnp.where(past, gate, NEG_INF)
    n_sel = min(MOBA_TOPK, nb)
    _, sel = lax.top_k(gate, n_sel)
    bi = jnp.arange(b)[:, None, None, None]
    hi = jnp.arange(h)[None, :, None, None]

    def chunk_fn(ci):
        q0 = ci * MOBA_QCHUNK
        blk = q0 // MOBA_BLOCK
        q_c = lax.dynamic_slice_in_dim(q, q0, MOBA_QCHUNK, axis=2)
        k_own = lax.dynamic_slice_in_dim(k, blk * MOBA_BLOCK, MOBA_BLOCK, axis=2)
        v_own = lax.dynamic_slice_in_dim(v, blk * MOBA_BLOCK, MOBA_BLOCK, axis=2)
        sel_c = lax.dynamic_slice_in_dim(sel, q0, MOBA_QCHUNK, axis=2)
        k_sel = k_blocks[bi, hi, sel_c]
        v_sel = v_blocks[bi, hi, sel_c]
        own = jnp.einsum("bhqd,bhkd->bhqk", q_c, k_own, preferred_element_type=jnp.float32) * scale
        q_pos = q0 + jnp.arange(MOBA_QCHUNK)
        k_pos = blk * MOBA_BLOCK + jnp.arange(MOBA_BLOCK)
        own = jnp.where(k_pos[None, :] <= q_pos[:, None], own, NEG_INF)
        far = jnp.einsum("bhqd,bhqnkd->bhqnk", q_c, k_sel, preferred_element_type=jnp.float32) * scale
        far = jnp.where((jnp.arange(n_sel) < blk)[:, None], far, NEG_INF)
        logits = jnp.concatenate([own, far.reshape(b, h, MOBA_QCHUNK, n_sel * MOBA_BLOCK)], axis=-1)
        p = jax.nn.softmax(logits, axis=-1).astype(v.dtype)
        p_own = p[..., :MOBA_BLOCK]
        p_far = p[..., MOBA_BLOCK:].reshape(b, h, MOBA_QCHUNK, n_sel, MOBA_BLOCK)
        return (jnp.einsum("bhqk,bhkd->bhqd", p_own, v_own)
                + jnp.einsum("bhqnk,bhqnkd->bhqd", p_far, v_sel))

    out = lax.map(chunk_fn, jnp.arange(s_pad // MOBA_QCHUNK))
    out = jnp.moveaxis(out, 0, 2).reshape(b, h, s_pad, d)
    return out[:, :, :s]


def gla_mixer(x, w_in, w_gk, b_gk, norm_g, w_out):
    q, k, v, g, gk_low = split_cols(x @ w_in, GLA_COLS)
    log_a = jax.nn.log_sigmoid((gk_low @ w_gk + b_gk).astype(jnp.float32)) / GLA_GATE_NORMALIZER
    q = to_heads(q, GLA_HEADS) * (GLA_DK ** -0.5)
    o = chunk_gla(q, to_heads(k, GLA_HEADS), to_heads(v, GLA_HEADS), to_heads(log_a, GLA_HEADS))
    o = from_heads(rms_norm(o.astype(x.dtype), norm_g)) * jax.nn.silu(g)
    return o @ w_out


def moba_mixer(x, positions, w_in, w_out):
    q, k, v = split_cols(x @ w_in, MOBA_COLS)
    q = partial_rope(to_heads(q, MOBA_HEADS), positions)
    k = partial_rope(to_heads(k, MOBA_HEADS), positions)
    o = moba_attention(q, k, to_heads(v, MOBA_HEADS))
    return from_heads(o) @ w_out


def gdn_mixer(x, w_in, conv_w, a_log, dt_bias, norm_g, w_out):
    qkv, z, a, b_logit = split_cols(x @ w_in, GDN_COLS)
    qkv = jax.nn.silu(causal_depthwise_conv(qkv, conv_w))
    q, k, v = jnp.split(qkv, 3, axis=-1)
    q = l2_normalize(to_heads(q, GDN_HEADS)) * (GDN_DK ** -0.5)
    k = l2_normalize(to_heads(k, GDN_HEADS))
    v = to_heads(v, GDN_HEADS)
    beta = jax.nn.sigmoid(b_logit.astype(jnp.float32)).transpose(0, 2, 1)
    g = (-jnp.exp(a_log.astype(jnp.float32))
         * jax.nn.softplus(a.astype(jnp.float32) + dt_bias)).transpose(0, 2, 1)
    o = chunk_gated_delta(q, k, v, g, beta).astype(x.dtype)
    o = from_heads(rms_norm(o, norm_g)) * jax.nn.silu(z)
    return o @ w_out


def hgrn2_mixer(x, lb, w_in, norm_g, w_out):
    q, f_logit, i, g = split_cols(x @ w_in, HGRN_COLS)
    f_logit = f_logit.astype(jnp.float32)
    forget = lb + (1.0 - lb) * jax.nn.sigmoid(f_logit)
    k_in = (1.0 - lb) * jax.nn.sigmoid(-f_logit)
    o = chunk_gla(to_heads(q, HGRN_HEADS), to_heads(k_in, HGRN_HEADS), to_heads(i, HGRN_HEADS),
                  to_heads(jnp.log(forget), HGRN_HEADS))
    o = from_heads(rms_norm(o.astype(x.dtype), norm_g)) * jax.nn.silu(g)
    return o @ w_out


def setup_inputs(seed: int = 0) -> dict:
    key = jax.random.key(seed)
    ks = iter(jax.random.split(key, 32))
    f32 = jnp.float32

    def normal(shape):
        return jax.random.normal(next(ks), shape, f32)

    def dense(shape, fan_in, scale=1.0):
        return normal(shape) * (scale * fan_in ** -0.5)

    def gain(shape):
        return 1.0 + 0.02 * normal(shape)

    def small(shape):
        return 0.02 * normal(shape)

    x = normal((BATCH, SEQ, D_MODEL))
    offset = jax.random.randint(next(ks), (BATCH,), 0, 4096, dtype=jnp.int32)
    positions = offset[:, None] + jnp.arange(SEQ, dtype=jnp.int32)[None, :]
    a_init = jax.random.uniform(next(ks), (N_GDN, GDN_HEADS), f32, 1.0, 16.0)
    dt = jnp.exp(jax.random.uniform(next(ks), (N_GDN, GDN_HEADS), f32, float(np.log(1e-3)), float(np.log(1e-1))))
    return {
        "x": x,
        "positions": positions,
        "gla_w_in": dense((N_GLA, D_MODEL, sum(GLA_COLS)), D_MODEL),
        "gla_w_gk": dense((N_GLA, GLA_GATE_RANK, GLA_HEADS * GLA_DK), GLA_GATE_RANK),
        "gla_b_gk": small((N_GLA, GLA_HEADS * GLA_DK)),
        "gla_norm_g": gain((N_GLA, GLA_DV)),
        "gla_w_out": dense((N_GLA, GLA_HEADS * GLA_DV, D_MODEL), GLA_HEADS * GLA_DV, DEEPNORM_BETA),
        "moba_w_in": dense((N_MOBA, D_MODEL, sum(MOBA_COLS)), D_MODEL),
        "moba_w_out": dense((N_MOBA, D_MODEL, D_MODEL), D_MODEL, DEEPNORM_BETA),
        "gdn_w_in": dense((N_GDN, D_MODEL, sum(GDN_COLS)), D_MODEL),
        "gdn_conv_w": dense((N_GDN, GDN_CONV, 3 * GDN_WIDTH), GDN_CONV),
        "gdn_a_log": jnp.log(a_init),
        "gdn_dt_bias": dt + jnp.log(-jnp.expm1(-dt)),
        "gdn_norm_g": gain((N_GDN, GDN_DK)),
        "gdn_w_out": dense((N_GDN, GDN_WIDTH, D_MODEL), GDN_WIDTH, DEEPNORM_BETA),
        "hgrn_lower_bounds": small((DEPTH, HGRN_F)),
        "hgrn_w_in": dense((N_HGRN, D_MODEL, sum(HGRN_COLS)), D_MODEL),
        "hgrn_norm_g": gain((N_HGRN, D_MODEL // HGRN_HEADS)),
        "hgrn_w_out": dense((N_HGRN, D_MODEL, D_MODEL), D_MODEL, DEEPNORM_BETA),
        "ffn_w_gu": dense((DEPTH, D_MODEL, 2 * FFN_HIDDEN), D_MODEL),
        "ffn_w_down": dense((DEPTH, FFN_HIDDEN, D_MODEL), FFN_HIDDEN, DEEPNORM_BETA),
        "ln_g": gain((DEPTH, 2, D_MODEL)),
        "ln_b": small((DEPTH, 2, D_MODEL)),
    }


def reference(x, positions, gla_w_in, gla_w_gk, gla_b_gk, gla_norm_g, gla_w_out, moba_w_in, moba_w_out,
              gdn_w_in, gdn_conv_w, gdn_a_log, gdn_dt_bias, gdn_norm_g, gdn_w_out, hgrn_lower_bounds,
              hgrn_w_in, hgrn_norm_g, hgrn_w_out, ffn_w_gu, ffn_w_down, ln_g, ln_b):
    lb_soft = jax.nn.softmax(hgrn_lower_bounds.astype(jnp.float32), axis=0)
    lower_bounds = jnp.cumsum(lb_soft, axis=0) - lb_soft[0]
    for i in range(DEPTH):
        kind, j = i % N_MIXERS, i // N_MIXERS
        if kind == 0:
            y = gla_mixer(x, gla_w_in[j], gla_w_gk[j], gla_b_gk[j], gla_norm_g[j], gla_w_out[j])
        elif kind == 1:
            y = moba_mixer(x, positions, moba_w_in[j], moba_w_out[j])
        elif kind == 2:
            y = gdn_mixer(x, gdn_w_in[j], gdn_conv_w[j], gdn_a_log[j], gdn_dt_bias[j], gdn_norm_g[j], gdn_w_out[j])
        else:
            y = hgrn2_mixer(x, lower_bounds[i], hgrn_w_in[j], hgrn_norm_g[j], hgrn_w_out[j])
        x = layer_norm(DEEPNORM_ALPHA * x + y, ln_g[i, 0], ln_b[i, 0])
        x = layer_norm(DEEPNORM_ALPHA * x + swiglu_ffn(x, ffn_w_gu[i], ffn_w_down[i]), ln_g[i, 1], ln_b[i, 1])
    return x
```

```python
import functools

import numpy as np
import jax
import jax.numpy as jnp
from jax import lax
from jax.experimental import pallas as pl
from jax.experimental.pallas import tpu as pltpu

F32 = jnp.float32
BF16 = jnp.bfloat16
HIGHEST = lax.Precision.HIGHEST

D_MODEL = 1024
DEPTH = 4
N_MIXERS = 4
HEAD_DIM = 128
CHUNK = 64
GLA_HEADS = 4
GLA_DK = 128
GLA_DV = 256
GLA_GATE_RANK = 16
GLA_GATE_NORMALIZER = 16.0
MOBA_HEADS = 8
MOBA_BLOCK = 256
MOBA_TOPK = 3
ROPE_THETA = 500000.0
ROPE_DIM = 32
NEG_INF = -1e30
GDN_HEADS = 8
GDN_CONV = 4
HGRN_HEADS = 8
FFN_HIDDEN = 2816
DEEPNORM_ALPHA = (2.0 * DEPTH) ** 0.25

LANES = 128
SUBLANES = 8
VMEM_LIMIT = 56 * 1024 * 1024
TOKEN_TILE = 512
COL_CHUNK = 512


def _dot(a, b):
    return jnp.dot(a.astype(BF16), b.astype(BF16), preferred_element_type=F32)


def _dot_nt(a, b):
    return lax.dot_general(a.astype(BF16), b.astype(BF16), (((1,), (1,)), ((), ())),
                           preferred_element_type=F32)


def _dot_tn(a, b):
    return lax.dot_general(a.astype(BF16), b.astype(BF16), (((0,), (0,)), ((), ())),
                           preferred_element_type=F32)


def _dot_f32(a, b):
    return jnp.dot(a, b, precision=HIGHEST, preferred_element_type=F32)


def _dot_nt_f32(a, b):
    return lax.dot_general(a, b, (((1,), (1,)), ((), ())), precision=HIGHEST,
                           preferred_element_type=F32)


def _sigmoid(x):
    return 1.0 / (1.0 + jnp.exp(-x))


def _silu(x):
    return x * _sigmoid(x)


def _softplus(x):
    return jnp.maximum(x, 0.0) + jnp.log1p(jnp.exp(-jnp.abs(x)))


def _log_sigmoid(x):
    return -_softplus(-x)


def _col_chunks(n, width=COL_CHUNK):
    return tuple((c0, min(width, n - c0)) for c0 in range(0, n, width))


def _params(*semantics):
    return pltpu.CompilerParams(dimension_semantics=semantics, vmem_limit_bytes=VMEM_LIMIT)


def _resident(shape):
    zeros = (0,) * len(shape)
    return pl.BlockSpec(shape, lambda *_: zeros, pipeline_mode=pl.Buffered(1))


def _layer_norm(z, g, b):
    mu = jnp.mean(z, axis=-1, keepdims=True)
    zc = z - mu
    var = jnp.mean(zc * zc, axis=-1, keepdims=True)
    return zc * lax.rsqrt(var + 1e-5) * g + b


def _gated_rms_norm(o, norm_g, gate):
    ms = jnp.mean(o * o, axis=-1, keepdims=True)
    return o * lax.rsqrt(ms + 1e-6) * norm_g * _silu(gate)


def _tri_masks(n):
    row = lax.broadcasted_iota(jnp.int32, (n, n), 0)
    col = lax.broadcasted_iota(jnp.int32, (n, n), 1)
    return row >= col, row > col, row == col


def _proj_kernel(x_ref, w_ref, o_ref, *, chunks):
    xb = x_ref[...].astype(BF16)
    for c0, cw in chunks:
        o_ref[:, c0:c0 + cw] = jnp.dot(xb, w_ref[:, c0:c0 + cw], preferred_element_type=F32)


def _proj(x, w):
    t, k = x.shape
    n = w.shape[1]
    return pl.pallas_call(
        functools.partial(_proj_kernel, chunks=_col_chunks(n)),
        grid=(t // TOKEN_TILE,),
        in_specs=[pl.BlockSpec((TOKEN_TILE, k), lambda i: (i, 0)), _resident((k, n))],
        out_specs=pl.BlockSpec((TOKEN_TILE, n), lambda i: (i, 0)),
        out_shape=jax.ShapeDtypeStruct((t, n), F32),
        compiler_params=_params("parallel"),
        name="proj",
    )(x, w)


def _outproj_ln_kernel(o_ref, w_ref, x_ref, g_ref, b_ref, y_ref):
    y = jnp.dot(o_ref[...].astype(BF16), w_ref[...], preferred_element_type=F32)
    y_ref[...] = _layer_norm(DEEPNORM_ALPHA * x_ref[...] + y, g_ref[...], b_ref[...])


def _outproj_ln(o, w, x, g, b):
    t, k = o.shape
    d = w.shape[1]
    row = lambda i: (i, 0)
    return pl.pallas_call(
        _outproj_ln_kernel,
        grid=(t // TOKEN_TILE,),
        in_specs=[pl.BlockSpec((TOKEN_TILE, k), row), _resident((k, d)),
                  pl.BlockSpec((TOKEN_TILE, d), row), _resident((1, d)), _resident((1, d))],
        out_specs=pl.BlockSpec((TOKEN_TILE, d), row),
        out_shape=jax.ShapeDtypeStruct((t, d), F32),
        compiler_params=_params("parallel"),
        name="outproj_ln",
    )(o, w, x, g.reshape(1, d), b.reshape(1, d))


def _ffn_kernel(x_ref, wgu_ref, wd_ref, g_ref, b_ref, y_ref, h_ref, *, hidden, chunks, out_chunks):
    x = x_ref[...]
    xb = x.astype(BF16)
    for c0, cw in chunks:
        gate = jnp.dot(xb, wgu_ref[:, c0:c0 + cw], preferred_element_type=F32)
        up = jnp.dot(xb, wgu_ref[:, hidden + c0:hidden + c0 + cw], preferred_element_type=F32)
        h_ref[:, c0:c0 + cw] = (_silu(gate) * up).astype(BF16)
    for c0, cw in out_chunks:
        y_ref[:, c0:c0 + cw] = jnp.dot(h_ref[...], wd_ref[:, c0:c0 + cw], preferred_element_type=F32)
    y_ref[...] = _layer_norm(DEEPNORM_ALPHA * x + y_ref[...], g_ref[...], b_ref[...])


def _ffn(x, w_gu, w_down, g, b):
    t, d = x.shape
    hidden = w_down.shape[0]
    row = lambda i: (i, 0)
    return pl.pallas_call(
        functools.partial(_ffn_kernel, hidden=hidden, chunks=_col_chunks(hidden, 256),
                          out_chunks=_col_chunks(d)),
        grid=(t // TOKEN_TILE,),
        in_specs=[pl.BlockSpec((TOKEN_TILE, d), row), _resident((d, 2 * hidden)), _resident((hidden, d)),
                  _resident((1, d)), _resident((1, d))],
        out_specs=pl.BlockSpec((TOKEN_TILE, d), row),
        out_shape=jax.ShapeDtypeStruct((t, d), F32),
        scratch_shapes=[pltpu.VMEM((TOKEN_TILE, hidden), BF16)],
        compiler_params=_params("parallel"),
        name="ffn",
    )(x, w_gu, w_down, g.reshape(1, d), b.reshape(1, d))


def _gla_chunk(c, q, k, v, log_g, gate, norm_g, o_ref, st_ref, tril, tril_f):
    rows = pl.ds(pl.multiple_of(c * CHUNK, CHUNK), CHUNK)
    cum = _dot_f32(tril_f, log_g)
    last = cum[CHUNK - 1:CHUNK, :]
    q_dec = q * jnp.exp(cum)
    k_inv = k * jnp.exp(-cum)
    k_end = k * jnp.exp(last - cum)
    scores = jnp.where(tril, _dot_nt(q_dec, k_inv), 0.0)
    st = st_ref[...]
    o = _dot(scores, v) + _dot_nt(q_dec, st)
    st_ref[...] = st * jnp.exp(last) + _dot_tn(v, k_end)
    o_ref[rows, :] = _gated_rms_norm(o, norm_g, gate)


def _gla_kernel(q_ref, k_ref, v_ref, g_ref, low_ref, wgk_ref, bgk_ref, ng_ref, o_ref, st_ref, *, n_chunks):
    tril, _, _ = _tri_masks(CHUNK)
    tril_f = tril.astype(F32)
    st_ref[...] = jnp.zeros_like(st_ref)
    wgk = wgk_ref[...]
    bgk = bgk_ref[...]
    norm_g = ng_ref[...]

    def body(c, carry):
        rows = pl.ds(pl.multiple_of(c * CHUNK, CHUNK), CHUNK)
        logit = _dot_f32(low_ref[rows, :], wgk) + bgk
        log_g = _log_sigmoid(logit) * (1.0 / GLA_GATE_NORMALIZER)
        q = q_ref[rows, :] * (GLA_DK ** -0.5)
        _gla_chunk(c, q, k_ref[rows, :], v_ref[rows, :], log_g, g_ref[rows, :], norm_g, o_ref, st_ref,
                   tril, tril_f)
        return carry

    lax.fori_loop(0, n_chunks, body, 0)


def _gla_mixer(proj, w_gk, b_gk, norm_g):
    bsz, s, _ = proj.shape
    dk, dv, h = GLA_DK, GLA_DV, GLA_HEADS
    wgk_pad = jnp.zeros((LANES, h * dk), F32).at[:GLA_GATE_RANK].set(w_gk)
    seq = lambda width, off: pl.BlockSpec((None, s, width), lambda b, i: (b, 0, off + i))
    return pl.pallas_call(
        functools.partial(_gla_kernel, n_chunks=s // CHUNK),
        grid=(bsz, h),
        in_specs=[seq(dk, 0), seq(dk, h), seq(dv, (2 * h * dk) // dv), seq(dv, (2 * h * dk) // dv + h),
                  pl.BlockSpec((None, s, LANES), lambda b, i: (b, 0, (2 * h * dk + 2 * h * dv) // LANES)),
                  pl.BlockSpec((LANES, dk), lambda b, i: (0, i)),
                  pl.BlockSpec((1, dk), lambda b, i: (0, i)),
                  pl.BlockSpec((1, dv), lambda b, i: (0, 0))],
        out_specs=pl.BlockSpec((None, s, dv), lambda b, i: (b, 0, i)),
        out_shape=jax.ShapeDtypeStruct((bsz, s, h * dv), F32),
        scratch_shapes=[pltpu.VMEM((dv, dk), F32)],
        compiler_params=_params("parallel", "parallel"),
        name="gla",
    )(proj, proj, proj, proj, proj, wgk_pad, b_gk.reshape(1, -1), norm_g.reshape(1, -1))


def _hgrn_kernel(q_ref, f_ref, i_ref, g_ref, lb_ref, ng_ref, o_ref, st_ref, *, n_chunks, layer):
    tril, _, _ = _tri_masks(CHUNK)
    tril_f = tril.astype(F32)
    st_ref[...] = jnp.zeros_like(st_ref)
    norm_g = ng_ref[...]
    raw = lb_ref[...]
    e = jnp.exp(raw - jnp.max(raw, axis=0, keepdims=True))
    soft = e / jnp.sum(e, axis=0, keepdims=True)
    lb = jnp.sum(soft[:layer + 1, :], axis=0, keepdims=True) - soft[0:1, :]

    def body(c, carry):
        rows = pl.ds(pl.multiple_of(c * CHUNK, CHUNK), CHUNK)
        f = f_ref[rows, :]
        forget = lb + (1.0 - lb) * _sigmoid(f)
        k = (1.0 - lb) * _sigmoid(-f)
        _gla_chunk(c, q_ref[rows, :], k, i_ref[rows, :], jnp.log(forget), g_ref[rows, :], norm_g, o_ref,
                   st_ref, tril, tril_f)
        return carry

    lax.fori_loop(0, n_chunks, body, 0)


def _hgrn_mixer(proj, lower_bounds, norm_g, layer):
    bsz, s, _ = proj.shape
    d, h = HEAD_DIM, HGRN_HEADS
    seq = lambda off: pl.BlockSpec((None, s, d), lambda b, i: (b, 0, off + i))
    return pl.pallas_call(
        functools.partial(_hgrn_kernel, n_chunks=s // CHUNK, layer=layer),
        grid=(bsz, h),
        in_specs=[seq(0), seq(h), seq(2 * h), seq(3 * h),
                  pl.BlockSpec((DEPTH, d), lambda b, i: (0, i)),
                  pl.BlockSpec((1, d), lambda b, i: (0, 0))],
        out_specs=pl.BlockSpec((None, s, d), lambda b, i: (b, 0, i)),
        out_shape=jax.ShapeDtypeStruct((bsz, s, h * d), F32),
        scratch_shapes=[pltpu.VMEM((d, d), F32)],
        compiler_params=_params("parallel", "parallel"),
        name="hgrn2",
    )(proj, proj, proj, proj, lower_bounds, norm_g.reshape(1, -1))


def _gdn_kernel(q_ref, k_ref, v_ref, z_ref, ab_ref, cwq_ref, cwk_ref, cwv_ref, alog_ref, dtb_ref, ng_ref,
                o_ref, u_s, wq_s, qk_s, ke_s, dec_s, st_ref, *, n_chunks):
    head = pl.program_id(1)
    incl, strict, eye = _tri_masks(CHUNK)
    incl_f = incl.astype(F32)
    eye_f = eye.astype(F32)
    lane = lax.broadcasted_iota(jnp.int32, (1, LANES), 1)
    norm_g = ng_ref[...]
    neg_a = -jnp.exp(alog_ref[...])
    dt_bias = dtb_ref[...]

    def conv_silu(ref, cw_ref, c):
        r0 = pl.multiple_of(c * CHUNK, CHUNK)
        cur = ref[pl.ds(r0, CHUNK), :]
        prev = ref[pl.ds(pl.multiple_of(jnp.maximum(r0 - SUBLANES, 0), SUBLANES), SUBLANES), :]
        prev = jnp.where(c > 0, prev, 0.0)
        ext = jnp.concatenate([prev, cur], axis=0)
        w = cw_ref[...]
        y = ext[SUBLANES:, :] * w[GDN_CONV - 1:GDN_CONV, :]
        for j in range(GDN_CONV - 1):
            lo = SUBLANES - (GDN_CONV - 1) + j
            y = y + ext[lo:lo + CHUNK, :] * w[j:j + 1, :]
        return _silu(y)

    def prepare(c, carry):
        rows = pl.ds(pl.multiple_of(c * CHUNK, CHUNK), CHUNK)
        q = conv_silu(q_ref, cwq_ref, c)
        k = conv_silu(k_ref, cwk_ref, c)
        v = conv_silu(v_ref, cwv_ref, c)
        q = q * lax.rsqrt(jnp.sum(q * q, axis=-1, keepdims=True) + 1e-6) * (HEAD_DIM ** -0.5)
        k = k * lax.rsqrt(jnp.sum(k * k, axis=-1, keepdims=True) + 1e-6)
        ab = ab_ref[rows, :]
        g_all = neg_a * _softplus(ab + dt_bias)
        cum_all = _dot_f32(incl_f, g_all)
        cum = jnp.sum(jnp.where(lane == head, cum_all, 0.0), axis=1, keepdims=True)
        beta = jnp.sum(jnp.where(lane == head + GDN_HEADS, _sigmoid(ab), 0.0), axis=1, keepdims=True)
        cum_row = jnp.sum(jnp.where(eye, jnp.broadcast_to(cum, (CHUNK, CHUNK)), 0.0), axis=0, keepdims=True)
        decay = jnp.where(incl, jnp.exp(jnp.where(incl, cum - cum_row, 0.0)), 0.0)
        k_beta = k * beta
        lower = jnp.where(strict, _dot_nt_f32(k_beta, k) * decay, 0.0)
        power = -lower
        inv = eye_f + power
        for _ in range(int(np.log2(CHUNK)) - 1):
            power = _dot_f32(power, power)
            inv = inv + _dot_f32(inv, power)
        e_cum = jnp.exp(cum)
        u = _dot_f32(inv, v * beta)
        w = _dot_f32(inv, k_beta * e_cum)
        last = cum[CHUNK - 1:CHUNK, :]
        u_s[rows, :] = u
        wq_s[c] = jnp.concatenate([w, q * e_cum], axis=0)
        qk_s[rows, :] = _dot_nt_f32(q, k) * decay
        ke_s[rows, :] = k * jnp.exp(last - cum)
        dec_s[pl.ds(pl.multiple_of(c * SUBLANES, SUBLANES), SUBLANES), :] = jnp.broadcast_to(
            jnp.exp(last), (SUBLANES, LANES))
        return carry

    lax.fori_loop(0, n_chunks, prepare, 0)
    st_ref[...] = jnp.zeros_like(st_ref)

    def recur(c, carry):
        rows = pl.ds(pl.multiple_of(c * CHUNK, CHUNK), CHUNK)
        st = st_ref[...]
        t = _dot(wq_s[c], st)
        v_new = u_s[rows, :] - t[:CHUNK, :]
        o = t[CHUNK:, :] + _dot(qk_s[rows, :], v_new)
        dec = dec_s[pl.ds(pl.multiple_of(c * SUBLANES, SUBLANES), SUBLANES), :]
        st_ref[...] = st * dec[0:1, :] + _dot_tn(ke_s[rows, :], v_new)
        o_ref[rows, :] = _gated_rms_norm(o, norm_g, z_ref[rows, :])
        return carry

    lax.fori_loop(0, n_chunks, recur, 0)


def _gdn_mixer(proj, conv_w, a_log, dt_bias, norm_g):
    bsz, s, _ = proj.shape
    d, h = HEAD_DIM, GDN_HEADS
    n_chunks = s // CHUNK
    seq = lambda off: pl.BlockSpec((None, s, d), lambda b, i: (b, 0, off + i))
    cw = lambda off: pl.BlockSpec((GDN_CONV, d), lambda b, i: (0, off + i))
    vec = pl.BlockSpec((1, LANES), lambda b, i: (0, 0))
    pad = lambda a: jnp.zeros((1, LANES), F32).at[0, :h].set(a)
    return pl.pallas_call(
        functools.partial(_gdn_kernel, n_chunks=n_chunks),
        grid=(bsz, h),
        in_specs=[seq(0), seq(h), seq(2 * h), seq(3 * h),
                  pl.BlockSpec((None, s, LANES), lambda b, i: (b, 0, 4 * h)),
                  cw(0), cw(h), cw(2 * h), vec, vec, vec],
        out_specs=pl.BlockSpec((None, s, d), lambda b, i: (b, 0, i)),
        out_shape=jax.ShapeDtypeStruct((bsz, s, h * d), F32),
        scratch_shapes=[pltpu.VMEM((s, d), F32),
                        pltpu.VMEM((n_chunks, 2 * CHUNK, d), F32),
                        pltpu.VMEM((s, CHUNK), F32),
                        pltpu.VMEM((s, d), F32),
                        pltpu.VMEM((n_chunks * SUBLANES, LANES), F32),
                        pltpu.VMEM((d, d), F32)],
        compiler_params=_params("parallel", "parallel"),
        name="gdn",
    )(proj, proj, proj, proj, proj, conv_w, conv_w, conv_w, pad(a_log), pad(dt_bias),
      norm_g.reshape(1, -1))


def _rope_table_kernel(pos_ref, invf_ref, cos_ref, sin_ref):
    ang = pos_ref[...].astype(F32) * invf_ref[...]
    lane = lax.broadcasted_iota(jnp.int32, ang.shape, 1)
    cos_ref[...] = jnp.cos(ang)
    s = jnp.sin(ang)
    sin_ref[...] = jnp.where(lane < ROPE_DIM // 2, -s, s)


def _rope_tables(positions):
    bsz, s = positions.shape
    half = ROPE_DIM // 2
    inv_freq = (np.float32(ROPE_THETA) ** (-np.arange(0, ROPE_DIM, 2, dtype=np.float32) / ROPE_DIM)).astype(np.float32)
    invf = np.zeros((1, LANES), np.float32)
    invf[0, :half] = inv_freq
    invf[0, half:ROPE_DIM] = inv_freq
    rows = MOBA_BLOCK
    out = jax.ShapeDtypeStruct((bsz, s, LANES), F32)
    return pl.pallas_call(
        _rope_table_kernel,
        grid=(bsz, s // rows),
        in_specs=[pl.BlockSpec((None, rows, 1), lambda b, i: (b, i, 0)),
                  pl.BlockSpec((1, LANES), lambda b, i: (0, 0))],
        out_specs=[pl.BlockSpec((None, rows, LANES), lambda b, i: (b, i, 0))] * 2,
        out_shape=[out, out],
        compiler_params=_params("parallel", "parallel"),
        name="rope_table",
    )(positions.reshape(bsz, s, 1), jnp.asarray(invf))


def _moba_kernel(q_ref, k_ref, v_ref, cos_ref, sin_ref, o_ref, kr_s, vb_s, km_s, s_s, *, n_blocks):
    blk = MOBA_BLOCK
    half = ROPE_DIM // 2
    scale = HEAD_DIM ** -0.5
    lane = lax.broadcasted_iota(jnp.int32, (1, LANES), 1)
    causal, _, _ = _tri_masks(blk)

    def rope(x, rows):
        partner = jnp.where(lane < half, pltpu.roll(x, LANES - half, 1), pltpu.roll(x, half, 1))
        return x * cos_ref[rows, :] + partner * sin_ref[rows, :]

    km_s[...] = jnp.zeros_like(km_s)
    for j in range(n_blocks):
        rows = slice(j * blk, (j + 1) * blk)
        kr = rope(k_ref[rows, :], rows)
        kr_s[rows, :] = kr.astype(BF16)
        km_s[j:j + 1, :] = jnp.mean(kr, axis=0, keepdims=True)
    vb_s[...] = v_ref[...].astype(BF16)

    for qb in range(n_blocks):
        rows = slice(qb * blk, (qb + 1) * blk)
        qr = rope(q_ref[rows, :], rows)
        qb16 = qr.astype(BF16)
        if qb > 0:
            gate = jnp.where(lane < qb, _dot_nt_f32(qr, km_s[...]), NEG_INF)
            for j in range(qb):
                gj = gate[:, j:j + 1]
                beats = jnp.where(lane < j, jnp.where(gate >= gj, 1.0, 0.0), jnp.where(gate > gj, 1.0, 0.0))
                rank = jnp.sum(beats, axis=1, keepdims=True)
                cols = slice(j * blk, (j + 1) * blk)
                far = _dot_nt(qb16, kr_s[cols, :]) * scale
                s_s[:, cols] = jnp.where(rank < MOBA_TOPK, far, NEG_INF)
        own = _dot_nt(qb16, kr_s[rows, :]) * scale
        s_s[:, rows] = jnp.where(causal, own, NEG_INF)
        width = (qb + 1) * blk
        logits = s_s[:, :width]
        p = jnp.exp(logits - jnp.max(logits, axis=1, keepdims=True))
        denom = jnp.sum(p, axis=1, keepdims=True)
        o_ref[rows, :] = jnp.dot(p.astype(BF16), vb_s[:width, :], preferred_element_type=F32) / denom


def _moba_mixer(proj, cos, sin):
    bsz, s, _ = proj.shape
    d, h = HEAD_DIM, MOBA_HEADS
    seq = lambda off: pl.BlockSpec((None, s, d), lambda b, i: (b, 0, off + i))
    tab = pl.BlockSpec((None, s, LANES), lambda b, i: (b, 0, 0))
    return pl.pallas_call(
        functools.partial(_moba_kernel, n_blocks=s // MOBA_BLOCK),
        grid=(bsz, h),
        in_specs=[seq(0), seq(h), seq(2 * h), tab, tab],
        out_specs=pl.BlockSpec((None, s, d), lambda b, i: (b, 0, i)),
        out_shape=jax.ShapeDtypeStruct((bsz, s, h * d), F32),
        scratch_shapes=[pltpu.VMEM((s, d), BF16),
                        pltpu.VMEM((s, d), BF16),
                        pltpu.VMEM((LANES, d), F32),
                        pltpu.VMEM((MOBA_BLOCK, s), F32)],
        compiler_params=_params("parallel", "parallel"),
        name="moba",
    )(proj, proj, proj, cos, sin)


def _pad_cols(w, multiple=LANES):
    n = w.shape[1]
    return jnp.pad(w, ((0, 0), (0, -n % multiple)))


def kernel(x, positions, gla_w_in, gla_w_gk, gla_b_gk, gla_norm_g, gla_w_out, moba_w_in, moba_w_out,
           gdn_w_in, gdn_conv_w, gdn_a_log, gdn_dt_bias, gdn_norm_g, gdn_w_out, hgrn_lower_bounds,
           hgrn_w_in, hgrn_norm_g, hgrn_w_out, ffn_w_gu, ffn_w_down, ln_g, ln_b):
    bsz, s, d = x.shape
    assert (s % MOBA_BLOCK, (bsz * s) % TOKEN_TILE, d) == (0, 0, D_MODEL)
    t = bsz * s
    xf = x.reshape(t, d)
    cos = sin = None
    for i in range(DEPTH):
        kind, j = i % N_MIXERS, i // N_MIXERS
        if kind == 0:
            proj = _proj(xf, _pad_cols(gla_w_in[j]).astype(BF16)).reshape(bsz, s, -1)
            o = _gla_mixer(proj, gla_w_gk[j], gla_b_gk[j], gla_norm_g[j])
            w_out = gla_w_out[j]
        elif kind == 1:
            if cos is None:
                cos, sin = _rope_tables(positions)
            proj = _proj(xf, moba_w_in[j].astype(BF16)).reshape(bsz, s, -1)
            o = _moba_mixer(proj, cos, sin)
            w_out = moba_w_out[j]
        elif kind == 2:
            proj = _proj(xf, _pad_cols(gdn_w_in[j]).astype(BF16)).reshape(bsz, s, -1)
            o = _gdn_mixer(proj, gdn_conv_w[j], gdn_a_log[j], gdn_dt_bias[j], gdn_norm_g[j])
            w_out = gdn_w_out[j]
        else:
            proj = _proj(xf, hgrn_w_in[j].astype(BF16)).reshape(bsz, s, -1)
            o = _hgrn_mixer(proj, hgrn_lower_bounds, hgrn_norm_g[j], i)
            w_out = hgrn_w_out[j]
        xf = _outproj_ln(o.reshape(t, d), w_out.astype(BF16), xf, ln_g[i, 0], ln_b[i, 0])
        xf = _ffn(xf, ffn_w_gu[i].astype(BF16), ffn_w_down[i].astype(BF16), ln_g[i, 1], ln_b[i, 1])
    return xf.reshape(bsz, s, d)
```

```python
import functools

import numpy as np
import jax
import jax.numpy as jnp
from jax import lax
from jax.experimental import pallas as pl
from jax.experimental.pallas import tpu as pltpu

F32 = jnp.float32
BF16 = jnp.bfloat16
HIGHEST = lax.Precision.HIGHEST

D_MODEL = 1024
DEPTH = 4
N_MIXERS = 4
HEAD_DIM = 128
CHUNK = 64
GLA_HEADS = 4
GLA_DK = 128
GLA_DV = 256
GLA_GATE_RANK = 16
GLA_GATE_NORMALIZER = 16.0
MOBA_HEADS = 8
MOBA_BLOCK = 256
MOBA_TOPK = 3
ROPE_THETA = 500000.0
ROPE_DIM = 32
NEG_INF = -1e30
GDN_HEADS = 8
GDN_CONV = 4
GDN_CHUNK = 256
GDN_INV_BASE = 16
GDN_GROUP = 2
HGRN_HEADS = 8
FFN_HIDDEN = 2816
DEEPNORM_ALPHA = (2.0 * DEPTH) ** 0.25

LANES = 128
SUBLANES = 8
VMEM_LIMIT = 56 * 1024 * 1024
TOKEN_TILE = 512
COL_CHUNK = 512


def _dot(a, b):
    return jnp.dot(a.astype(BF16), b.astype(BF16), preferred_element_type=F32)


def _dot_nt(a, b):
    return lax.dot_general(a.astype(BF16), b.astype(BF16), (((1,), (1,)), ((), ())),
                           preferred_element_type=F32)


def _dot_tn(a, b):
    return lax.dot_general(a.astype(BF16), b.astype(BF16), (((0,), (0,)), ((), ())),
                           preferred_element_type=F32)


def _bdot(a, b):
    return lax.dot_general(a.astype(BF16), b.astype(BF16), (((2,), (1,)), ((0,), (0,))),
                           preferred_element_type=F32)


def _bdot_nt(a, b):
    return lax.dot_general(a.astype(BF16), b.astype(BF16), (((2,), (2,)), ((0,), (0,))),
                           preferred_element_type=F32)


def _dot_f32(a, b):
    return jnp.dot(a, b, precision=HIGHEST, preferred_element_type=F32)


def _dot_nt_f32(a, b):
    return lax.dot_general(a, b, (((1,), (1,)), ((), ())), precision=HIGHEST,
                           preferred_element_type=F32)


def _sigmoid(x):
    return 1.0 / (1.0 + jnp.exp(-x))


def _silu(x):
    return x * _sigmoid(x)


def _softplus(x):
    return jnp.maximum(x, 0.0) + jnp.log1p(jnp.exp(-jnp.abs(x)))


def _log_sigmoid(x):
    return -_softplus(-x)


def _col_chunks(n, width=COL_CHUNK):
    return tuple((c0, min(width, n - c0)) for c0 in range(0, n, width))


def _params(*semantics):
    return pltpu.CompilerParams(dimension_semantics=semantics, vmem_limit_bytes=VMEM_LIMIT)


def _resident(shape):
    zeros = (0,) * len(shape)
    return pl.BlockSpec(shape, lambda *_: zeros, pipeline_mode=pl.Buffered(1))


def _layer_norm(z, g, b):
    mu = jnp.mean(z, axis=-1, keepdims=True)
    zc = z - mu
    var = jnp.mean(zc * zc, axis=-1, keepdims=True)
    return zc * lax.rsqrt(var + 1e-5) * g + b


def _gated_rms_norm(o, norm_g, gate):
    ms = jnp.mean(o * o, axis=-1, keepdims=True)
    return o * lax.rsqrt(ms + 1e-6) * norm_g * _silu(gate)


def _tri_masks(n):
    row = lax.broadcasted_iota(jnp.int32, (n, n), 0)
    col = lax.broadcasted_iota(jnp.int32, (n, n), 1)
    return row >= col, row > col, row == col


def _proj_kernel(x_ref, w_ref, o_ref, *, chunks):
    xb = x_ref[...].astype(BF16)
    for c0, cw in chunks:
        o_ref[:, c0:c0 + cw] = jnp.dot(xb, w_ref[:, c0:c0 + cw], preferred_element_type=F32)


def _proj(x, w):
    t, k = x.shape
    n = w.shape[1]
    return pl.pallas_call(
        functools.partial(_proj_kernel, chunks=_col_chunks(n)),
        grid=(t // TOKEN_TILE,),
        in_specs=[pl.BlockSpec((TOKEN_TILE, k), lambda i: (i, 0)), _resident((k, n))],
        out_specs=pl.BlockSpec((TOKEN_TILE, n), lambda i: (i, 0)),
        out_shape=jax.ShapeDtypeStruct((t, n), F32),
        compiler_params=_params("parallel"),
        name="proj",
    )(x, w)


def _outproj_ln_kernel(o_ref, w_ref, x_ref, g_ref, b_ref, y_ref):
    y = jnp.dot(o_ref[...].astype(BF16), w_ref[...], preferred_element_type=F32)
    y_ref[...] = _layer_norm(DEEPNORM_ALPHA * x_ref[...] + y, g_ref[...], b_ref[...])


def _outproj_ln(o, w, x, g, b):
    t, k = o.shape
    d = w.shape[1]
    row = lambda i: (i, 0)
    return pl.pallas_call(
        _outproj_ln_kernel,
        grid=(t // TOKEN_TILE,),
        in_specs=[pl.BlockSpec((TOKEN_TILE, k), row), _resident((k, d)),
                  pl.BlockSpec((TOKEN_TILE, d), row), _resident((1, d)), _resident((1, d))],
        out_specs=pl.BlockSpec((TOKEN_TILE, d), row),
        out_shape=jax.ShapeDtypeStruct((t, d), F32),
        compiler_params=_params("parallel"),
        name="outproj_ln",
    )(o, w, x, g.reshape(1, d), b.reshape(1, d))


def _ffn_kernel(x_ref, wgu_ref, wd_ref, g_ref, b_ref, y_ref, h_ref, *, hidden, chunks, out_chunks):
    x = x_ref[...]
    xb = x.astype(BF16)
    for c0, cw in chunks:
        gate = jnp.dot(xb, wgu_ref[:, c0:c0 + cw], preferred_element_type=F32)
        up = jnp.dot(xb, wgu_ref[:, hidden + c0:hidden + c0 + cw], preferred_element_type=F32)
        h_ref[:, c0:c0 + cw] = (_silu(gate) * up).astype(BF16)
    for c0, cw in out_chunks:
        y_ref[:, c0:c0 + cw] = jnp.dot(h_ref[...], wd_ref[:, c0:c0 + cw], preferred_element_type=F32)
    y_ref[...] = _layer_norm(DEEPNORM_ALPHA * x + y_ref[...], g_ref[...], b_ref[...])


def _ffn(x, w_gu, w_down, g, b):
    t, d = x.shape
    hidden = w_down.shape[0]
    row = lambda i: (i, 0)
    return pl.pallas_call(
        functools.partial(_ffn_kernel, hidden=hidden, chunks=_col_chunks(hidden, 256),
                          out_chunks=_col_chunks(d)),
        grid=(t // TOKEN_TILE,),
        in_specs=[pl.BlockSpec((TOKEN_TILE, d), row), _resident((d, 2 * hidden)), _resident((hidden, d)),
                  _resident((1, d)), _resident((1, d))],
        out_specs=pl.BlockSpec((TOKEN_TILE, d), row),
        out_shape=jax.ShapeDtypeStruct((t, d), F32),
        scratch_shapes=[pltpu.VMEM((TOKEN_TILE, hidden), BF16)],
        compiler_params=_params("parallel"),
        name="ffn",
    )(x, w_gu, w_down, g.reshape(1, d), b.reshape(1, d))


def _gla_chunk(c, q, k, v, log_g, gate, norm_g, o_ref, st_ref, tril, tril_f):
    rows = pl.ds(pl.multiple_of(c * CHUNK, CHUNK), CHUNK)
    cum = _dot_f32(tril_f, log_g)
    last = cum[CHUNK - 1:CHUNK, :]
    q_dec = q * jnp.exp(cum)
    k_inv = k * jnp.exp(-cum)
    k_end = k * jnp.exp(last - cum)
    scores = jnp.where(tril, _dot_nt(q_dec, k_inv), 0.0)
    st = st_ref[...]
    o = _dot(scores, v) + _dot_nt(q_dec, st)
    st_ref[...] = st * jnp.exp(last) + _dot_tn(v, k_end)
    o_ref[rows, :] = _gated_rms_norm(o, norm_g, gate)


def _gla_kernel(q_ref, k_ref, v_ref, g_ref, low_ref, wgk_ref, bgk_ref, ng_ref, o_ref, st_ref, *, n_chunks):
    tril, _, _ = _tri_masks(CHUNK)
    tril_f = tril.astype(F32)
    st_ref[...] = jnp.zeros_like(st_ref)
    wgk = wgk_ref[...]
    bgk = bgk_ref[...]
    norm_g = ng_ref[...]

    def body(c, carry):
        rows = pl.ds(pl.multiple_of(c * CHUNK, CHUNK), CHUNK)
        logit = _dot_f32(low_ref[rows, :], wgk) + bgk
        log_g = _log_sigmoid(logit) * (1.0 / GLA_GATE_NORMALIZER)
        q = q_ref[rows, :] * (GLA_DK ** -0.5)
        _gla_chunk(c, q, k_ref[rows, :], v_ref[rows, :], log_g, g_ref[rows, :], norm_g, o_ref, st_ref,
                   tril, tril_f)
        return carry

    lax.fori_loop(0, n_chunks, body, 0, unroll=4)


def _gla_mixer(proj, w_gk, b_gk, norm_g):
    bsz, s, _ = proj.shape
    dk, dv, h = GLA_DK, GLA_DV, GLA_HEADS
    wgk_pad = jnp.zeros((LANES, h * dk), F32).at[:GLA_GATE_RANK].set(w_gk)
    seq = lambda width, off: pl.BlockSpec((None, s, width), lambda b, i: (b, 0, off + i))
    return pl.pallas_call(
        functools.partial(_gla_kernel, n_chunks=s // CHUNK),
        grid=(bsz, h),
        in_specs=[seq(dk, 0), seq(dk, h), seq(dv, (2 * h * dk) // dv), seq(dv, (2 * h * dk) // dv + h),
                  pl.BlockSpec((None, s, LANES), lambda b, i: (b, 0, (2 * h * dk + 2 * h * dv) // LANES)),
                  pl.BlockSpec((LANES, dk), lambda b, i: (0, i)),
                  pl.BlockSpec((1, dk), lambda b, i: (0, i)),
                  pl.BlockSpec((1, dv), lambda b, i: (0, 0))],
        out_specs=pl.BlockSpec((None, s, dv), lambda b, i: (b, 0, i)),
        out_shape=jax.ShapeDtypeStruct((bsz, s, h * dv), F32),
        scratch_shapes=[pltpu.VMEM((dv, dk), F32)],
        compiler_params=_params("parallel", "parallel"),
        name="gla",
    )(proj, proj, proj, proj, proj, wgk_pad, b_gk.reshape(1, -1), norm_g.reshape(1, -1))


def _hgrn_kernel(q_ref, f_ref, i_ref, g_ref, lb_ref, ng_ref, o_ref, st_ref, *, n_chunks, layer):
    tril, _, _ = _tri_masks(CHUNK)
    tril_f = tril.astype(F32)
    st_ref[...] = jnp.zeros_like(st_ref)
    norm_g = ng_ref[...]
    raw = lb_ref[...]
    e = jnp.exp(raw - jnp.max(raw, axis=0, keepdims=True))
    soft = e / jnp.sum(e, axis=0, keepdims=True)
    lb = jnp.sum(soft[:layer + 1, :], axis=0, keepdims=True) - soft[0:1, :]

    def body(c, carry):
        rows = pl.ds(pl.multiple_of(c * CHUNK, CHUNK), CHUNK)
        f = f_ref[rows, :]
        forget = lb + (1.0 - lb) * _sigmoid(f)
        k = (1.0 - lb) * _sigmoid(-f)
        _gla_chunk(c, q_ref[rows, :], k, i_ref[rows, :], jnp.log(forget), g_ref[rows, :], norm_g, o_ref,
                   st_ref, tril, tril_f)
        return carry

    lax.fori_loop(0, n_chunks, body, 0, unroll=4)


def _hgrn_mixer(proj, lower_bounds, norm_g, layer):
    bsz, s, _ = proj.shape
    d, h = HEAD_DIM, HGRN_HEADS
    seq = lambda off: pl.BlockSpec((None, s, d), lambda b, i: (b, 0, off + i))
    return pl.pallas_call(
        functools.partial(_hgrn_kernel, n_chunks=s // CHUNK, layer=layer),
        grid=(bsz, h),
        in_specs=[seq(0), seq(h), seq(2 * h), seq(3 * h),
                  pl.BlockSpec((DEPTH, d), lambda b, i: (0, i)),
                  pl.BlockSpec((1, d), lambda b, i: (0, 0))],
        out_specs=pl.BlockSpec((None, s, d), lambda b, i: (b, 0, i)),
        out_shape=jax.ShapeDtypeStruct((bsz, s, h * d), F32),
        scratch_shapes=[pltpu.VMEM((d, d), F32)],
        compiler_params=_params("parallel", "parallel"),
        name="hgrn2",
    )(proj, proj, proj, proj, lower_bounds, norm_g.reshape(1, -1))


def _split3(x):
    hi = x.astype(BF16)
    rest = x - hi.astype(F32)
    mid = rest.astype(BF16)
    return hi, mid, (rest - mid.astype(F32)).astype(BF16)


def _gdn_kernel(q_ref, k_ref, v_ref, z_ref, ab_ref, cwq_ref, cwk_ref, cwv_ref, alog_ref, dtb_ref, ng_ref,
                o_ref, st_ref, *, n_chunks):
    cs, gh, hd = GDN_CHUNK, GDN_GROUP, HEAD_DIM
    head0 = pl.program_id(1) * gh
    row = lax.broadcasted_iota(jnp.int32, (cs, cs), 0)
    col = lax.broadcasted_iota(jnp.int32, (cs, cs), 1)
    incl, strict, eye = row >= col, row > col, row == col
    tril_b = jnp.where(incl, 1.0, 0.0).astype(BF16)
    eye_f = jnp.where(eye, 1.0, 0.0)
    block_bits = jnp.bitwise_xor(row, col)
    lane = lax.broadcasted_iota(jnp.int32, (1, LANES), 1)
    norm_g = ng_ref[...]
    neg_a = -jnp.exp(alog_ref[...])
    dt_bias = dtb_ref[...]
    st_ref[...] = jnp.zeros_like(st_ref)

    def conv_silu(ref, cw_ref, c):
        r0 = pl.multiple_of(c * cs, cs)
        cur = ref[pl.ds(r0, cs), :]
        prev = ref[pl.ds(pl.multiple_of(jnp.maximum(r0 - SUBLANES, 0), SUBLANES), SUBLANES), :]
        prev = jnp.where(c > 0, prev, 0.0)
        ext = jnp.concatenate([prev, cur], axis=0)
        w = cw_ref[...]
        y = ext[SUBLANES:, :] * w[GDN_CONV - 1:GDN_CONV, :]
        for j in range(GDN_CONV - 1):
            lo = SUBLANES - (GDN_CONV - 1) + j
            y = y + ext[lo:lo + cs, :] * w[j:j + 1, :]
        return _silu(y)

    def per_head(x):
        return jnp.stack([x[:, j * hd:(j + 1) * hd] for j in range(gh)])

    def pick(x, first):
        return jnp.stack([jnp.sum(jnp.where(lane == first + j, x, 0.0), axis=1, keepdims=True)
                          for j in range(gh)])

    def body(c, carry):
        rows = pl.ds(pl.multiple_of(c * cs, cs), cs)
        q = per_head(conv_silu(q_ref, cwq_ref, c))
        k = per_head(conv_silu(k_ref, cwk_ref, c))
        v = per_head(conv_silu(v_ref, cwv_ref, c))
        q = q * lax.rsqrt(jnp.sum(q * q, axis=-1, keepdims=True) + 1e-6) * (hd ** -0.5)
        k = k * lax.rsqrt(jnp.sum(k * k, axis=-1, keepdims=True) + 1e-6)
        ab = ab_ref[rows, :]
        g = pick(neg_a * _softplus(ab + dt_bias), head0)
        beta = pick(_sigmoid(ab), head0 + GDN_HEADS)
        g_wide = jnp.concatenate([jnp.broadcast_to(g[j], (cs, hd)) for j in range(gh)], axis=1)
        cum = per_head(sum(jnp.dot(tril_b, part, preferred_element_type=F32) for part in _split3(g_wide)))
        cum_wide = jnp.concatenate([cum] * (cs // hd), axis=2)
        cum_row = jnp.sum(jnp.where(eye, cum_wide, 0.0), axis=1, keepdims=True)
        decay = jnp.where(incl, jnp.exp(jnp.where(incl, cum_wide - cum_row, 0.0)), 0.0)
        k_beta = k * beta
        lower = jnp.where(strict, _bdot_nt(k_beta, k) * decay, 0.0)
        base_bits = int(np.log2(GDN_INV_BASE))
        power = -jnp.where(jnp.right_shift(block_bits, base_bits) == 0, lower, 0.0)
        inv = eye_f + power
        for _ in range(base_bits - 1):
            power = _bdot(power, power)
            inv = inv + _bdot(inv, power)
        for bits in range(base_bits, int(np.log2(cs))):
            cross = jnp.where(jnp.right_shift(block_bits, bits) == 1, lower, 0.0)
            inv = inv - _bdot(_bdot(inv, cross), inv)
        e_cum = jnp.exp(cum)
        sol = _bdot(inv, jnp.concatenate([v * beta, k_beta * e_cum], axis=2))
        u, w = sol[:, :, :hd], sol[:, :, hd:]
        qk = _bdot_nt(q, k) * decay
        last = cum[:, cs - 1:cs, :]
        k_end = k * jnp.exp(last - cum)
        st = st_ref[...]
        t = _bdot(jnp.concatenate([w, q * e_cum], axis=1), st)
        v_new = u - t[:, :cs, :]
        o = t[:, cs:, :] + _bdot(qk, v_new)
        st_ref[...] = st * jnp.exp(last) + jnp.stack([_dot_tn(k_end[j], v_new[j]) for j in range(gh)])
        o = o * lax.rsqrt(jnp.mean(o * o, axis=-1, keepdims=True) + 1e-6) * norm_g
        o_ref[rows, :] = jnp.concatenate([o[j] for j in range(gh)], axis=1) * _silu(z_ref[rows, :])
        return carry

    lax.fori_loop(0, n_chunks, body, 0)


def _gdn_mixer(proj, conv_w, a_log, dt_bias, norm_g):
    bsz, s, _ = proj.shape
    d, h = HEAD_DIM, GDN_HEADS
    n_chunks = s // GDN_CHUNK
    gh = GDN_GROUP
    groups = h // gh
    seq = lambda off: pl.BlockSpec((None, s, gh * d), lambda b, i: (b, 0, off + i))
    cw = lambda off: pl.BlockSpec((GDN_CONV, gh * d), lambda b, i: (0, off + i))
    vec = pl.BlockSpec((1, LANES), lambda b, i: (0, 0))
    pad = lambda a: jnp.zeros((1, LANES), F32).at[0, :h].set(a)
    return pl.pallas_call(
        functools.partial(_gdn_kernel, n_chunks=n_chunks),
        grid=(bsz, groups),
        in_specs=[seq(0), seq(groups), seq(2 * groups), seq(3 * groups),
                  pl.BlockSpec((None, s, LANES), lambda b, i: (b, 0, 4 * h)),
                  cw(0), cw(groups), cw(2 * groups), vec, vec, vec],
        out_specs=pl.BlockSpec((None, s, gh * d), lambda b, i: (b, 0, i)),
        out_shape=jax.ShapeDtypeStruct((bsz, s, h * d), F32),
        scratch_shapes=[pltpu.VMEM((gh, d, d), F32)],
        compiler_params=_params("parallel", "parallel"),
        name="gdn",
    )(proj, proj, proj, proj, proj, conv_w, conv_w, conv_w, pad(a_log), pad(dt_bias),
      norm_g.reshape(1, -1))


def _rope_table_kernel(pos_ref, invf_ref, cos_ref, sin_ref):
    ang = pos_ref[...].astype(F32) * invf_ref[...]
    lane = lax.broadcasted_iota(jnp.int32, ang.shape, 1)
    cos_ref[...] = jnp.cos(ang)
    s = jnp.sin(ang)
    sin_ref[...] = jnp.where(lane < ROPE_DIM // 2, -s, s)


def _rope_tables(positions):
    bsz, s = positions.shape
    half = ROPE_DIM // 2
    inv_freq = (np.float32(ROPE_THETA) ** (-np.arange(0, ROPE_DIM, 2, dtype=np.float32) / ROPE_DIM)).astype(np.float32)
    invf = np.zeros((1, LANES), np.float32)
    invf[0, :half] = inv_freq
    invf[0, half:ROPE_DIM] = inv_freq
    rows = MOBA_BLOCK
    out = jax.ShapeDtypeStruct((bsz, s, LANES), F32)
    return pl.pallas_call(
        _rope_table_kernel,
        grid=(bsz, s // rows),
        in_specs=[pl.BlockSpec((None, rows, 1), lambda b, i: (b, i, 0)),
                  pl.BlockSpec((1, LANES), lambda b, i: (0, 0))],
        out_specs=[pl.BlockSpec((None, rows, LANES), lambda b, i: (b, i, 0))] * 2,
        out_shape=[out, out],
        compiler_params=_params("parallel", "parallel"),
        name="rope_table",
    )(positions.reshape(bsz, s, 1), jnp.asarray(invf))


def _moba_kernel(q_ref, k_ref, v_ref, cos_ref, sin_ref, o_ref, kr_s, vb_s, km_s, s_s, *, n_blocks):
    blk = MOBA_BLOCK
    half = ROPE_DIM // 2
    scale = HEAD_DIM ** -0.5
    lane = lax.broadcasted_iota(jnp.int32, (1, LANES), 1)
    causal, _, _ = _tri_masks(blk)

    def rope(x, rows):
        partner = jnp.where(lane < half, pltpu.roll(x, LANES - half, 1), pltpu.roll(x, half, 1))
        return x * cos_ref[rows, :] + partner * sin_ref[rows, :]

    km_s[...] = jnp.zeros_like(km_s)
    for j in range(n_blocks):
        rows = slice(j * blk, (j + 1) * blk)
        kr = rope(k_ref[rows, :], rows)
        kr_s[rows, :] = kr.astype(BF16)
        km_s[j:j + 1, :] = jnp.mean(kr, axis=0, keepdims=True)
    vb_s[...] = v_ref[...].astype(BF16)

    for qb in range(n_blocks):
        rows = slice(qb * blk, (qb + 1) * blk)
        qr = rope(q_ref[rows, :], rows)
        qb16 = qr.astype(BF16)
        if qb > 0:
            gate = jnp.where(lane < qb, _dot_nt_f32(qr, km_s[...]), NEG_INF)
            for j in range(qb):
                gj = gate[:, j:j + 1]
                beats = jnp.where(lane < j, jnp.where(gate >= gj, 1.0, 0.0), jnp.where(gate > gj, 1.0, 0.0))
                rank = jnp.sum(beats, axis=1, keepdims=True)
                cols = slice(j * blk, (j + 1) * blk)
                far = _dot_nt(qb16, kr_s[cols, :]) * scale
                s_s[:, cols] = jnp.where(rank < MOBA_TOPK, far, NEG_INF)
        own = _dot_nt(qb16, kr_s[rows, :]) * scale
        s_s[:, rows] = jnp.where(causal, own, NEG_INF)
        width = (qb + 1) * blk
        logits = s_s[:, :width]
        p = jnp.exp(logits - jnp.max(logits, axis=1, keepdims=True))
        denom = jnp.sum(p, axis=1, keepdims=True)
        o_ref[rows, :] = jnp.dot(p.astype(BF16), vb_s[:width, :], preferred_element_type=F32) / denom


def _moba_mixer(proj, cos, sin):
    bsz, s, _ = proj.shape
    d, h = HEAD_DIM, MOBA_HEADS
    seq = lambda off: pl.BlockSpec((None, s, d), lambda b, i: (b, 0, off + i))
    tab = pl.BlockSpec((None, s, LANES), lambda b, i: (b, 0, 0))
    return pl.pallas_call(
        functools.partial(_moba_kernel, n_blocks=s // MOBA_BLOCK),
        grid=(bsz, h),
        in_specs=[seq(0), seq(h), seq(2 * h), tab, tab],
        out_specs=pl.BlockSpec((None, s, d), lambda b, i: (b, 0, i)),
        out_shape=jax.ShapeDtypeStruct((bsz, s, h * d), F32),
        scratch_shapes=[pltpu.VMEM((s, d), BF16),
                        pltpu.VMEM((s, d), BF16),
                        pltpu.VMEM((LANES, d), F32),
                        pltpu.VMEM((MOBA_BLOCK, s), F32)],
        compiler_params=_params("parallel", "parallel"),
        name="moba",
    )(proj, proj, proj, cos, sin)


def _pad_cols(w, multiple=LANES):
    n = w.shape[1]
    return jnp.pad(w, ((0, 0), (0, -n % multiple)))


def kernel(x, positions, gla_w_in, gla_w_gk, gla_b_gk, gla_norm_g, gla_w_out, moba_w_in, moba_w_out,
           gdn_w_in, gdn_conv_w, gdn_a_log, gdn_dt_bias, gdn_norm_g, gdn_w_out, hgrn_lower_bounds,
           hgrn_w_in, hgrn_norm_g, hgrn_w_out, ffn_w_gu, ffn_w_down, ln_g, ln_b):
    bsz, s, d = x.shape
    assert (s % MOBA_BLOCK, (bsz * s) % TOKEN_TILE, d) == (0, 0, D_MODEL)
    t = bsz * s
    xf = x.reshape(t, d)
    cos = sin = None
    for i in range(DEPTH):
        kind, j = i % N_MIXERS, i // N_MIXERS
        if kind == 0:
            proj = _proj(xf, _pad_cols(gla_w_in[j]).astype(BF16)).reshape(bsz, s, -1)
            o = _gla_mixer(proj, gla_w_gk[j], gla_b_gk[j], gla_norm_g[j])
            w_out = gla_w_out[j]
        elif kind == 1:
            if cos is None:
                cos, sin = _rope_tables(positions)
            proj = _proj(xf, moba_w_in[j].astype(BF16)).reshape(bsz, s, -1)
            o = _moba_mixer(proj, cos, sin)
            w_out = moba_w_out[j]
        elif kind == 2:
            proj = _proj(xf, _pad_cols(gdn_w_in[j]).astype(BF16)).reshape(bsz, s, -1)
            o = _gdn_mixer(proj, gdn_conv_w[j], gdn_a_log[j], gdn_dt_bias[j], gdn_norm_g[j])
            w_out = gdn_w_out[j]
        else:
            proj = _proj(xf, hgrn_w_in[j].astype(BF16)).reshape(bsz, s, -1)
            o = _hgrn_mixer(proj, hgrn_lower_bounds, hgrn_norm_g[j], i)
            w_out = hgrn_w_out[j]
        xf = _outproj_ln(o.reshape(t, d), w_out.astype(BF16), xf, ln_g[i, 0], ln_b[i, 0])
        xf = _ffn(xf, ffn_w_gu[i].astype(BF16), ffn_w_down[i].astype(BF16), ln_g[i, 1], ln_b[i, 1])
    return xf.reshape(bsz, s, d)
```

```python
import functools

import numpy as np
import jax
import jax.numpy as jnp
from jax import lax
from jax.experimental import pallas as pl
from jax.experimental.pallas import tpu as pltpu

F32 = jnp.float32
BF16 = jnp.bfloat16
HIGHEST = lax.Precision.HIGHEST

D_MODEL = 1024
DEPTH = 4
N_MIXERS = 4
HEAD_DIM = 128
CHUNK = 64
SUPER = 256
MIX_GROUP = 2
GLA_HEADS = 4
GLA_DK = 128
GLA_DV = 256
GLA_GATE_RANK = 16
GLA_GATE_NORMALIZER = 16.0
MOBA_HEADS = 8
MOBA_BLOCK = 256
MOBA_TOPK = 3
ROPE_THETA = 500000.0
ROPE_DIM = 32
NEG_INF = -1e30
LOG2_E = 1.4426950408889634
GDN_HEADS = 8
GDN_CONV = 4
GDN_CHUNK = 256
GDN_INV_BASE = 16
GDN_GROUP = 2
HGRN_HEADS = 8
FFN_HIDDEN = 2816
DEEPNORM_ALPHA = (2.0 * DEPTH) ** 0.25

LANES = 128
SUBLANES = 8
VMEM_LIMIT = 56 * 1024 * 1024
TOKEN_TILE = 512
COL_CHUNK = 512


def _dot_tn(a, b):
    return lax.dot_general(a.astype(BF16), b.astype(BF16), (((0,), (0,)), ((), ())),
                           preferred_element_type=F32)


def _bdot(a, b):
    return lax.dot_general(a.astype(BF16), b.astype(BF16), (((2,), (1,)), ((0,), (0,))),
                           preferred_element_type=F32)


def _bdot_nt(a, b):
    return lax.dot_general(a.astype(BF16), b.astype(BF16), (((2,), (2,)), ((0,), (0,))),
                           preferred_element_type=F32)


def _bdot_nt_f32(a, b):
    return lax.dot_general(a, b, (((2,), (2,)), ((0,), (0,))), precision=HIGHEST,
                           preferred_element_type=F32)


def _sigmoid(x):
    return 1.0 / (1.0 + jnp.exp(-x))


def _silu(x):
    return x * _sigmoid(x)


def _softplus(x):
    return jnp.maximum(x, 0.0) + jnp.log1p(jnp.exp(-jnp.abs(x)))


def _log_sigmoid(x):
    return -_softplus(-x)


def _col_chunks(n, width=COL_CHUNK):
    return tuple((c0, min(width, n - c0)) for c0 in range(0, n, width))


def _params(*semantics):
    return pltpu.CompilerParams(dimension_semantics=semantics, vmem_limit_bytes=VMEM_LIMIT)


def _resident(shape):
    zeros = (0,) * len(shape)
    return pl.BlockSpec(shape, lambda *_: zeros, pipeline_mode=pl.Buffered(1))


def _layer_norm(z, g, b):
    mu = jnp.mean(z, axis=-1, keepdims=True)
    zc = z - mu
    var = jnp.mean(zc * zc, axis=-1, keepdims=True)
    return zc * lax.rsqrt(var + 1e-5) * g + b


def _tri_masks(n):
    row = lax.broadcasted_iota(jnp.int32, (n, n), 0)
    col = lax.broadcasted_iota(jnp.int32, (n, n), 1)
    return row >= col, row > col, row == col


def _proj_kernel(x_ref, w_ref, o_ref, *, chunks):
    xb = x_ref[...].astype(BF16)
    for c0, cw in chunks:
        o_ref[:, c0:c0 + cw] = jnp.dot(xb, w_ref[:, c0:c0 + cw], preferred_element_type=F32)


def _proj(x, w):
    t, k = x.shape
    n = w.shape[1]
    return pl.pallas_call(
        functools.partial(_proj_kernel, chunks=_col_chunks(n)),
        grid=(t // TOKEN_TILE,),
        in_specs=[pl.BlockSpec((TOKEN_TILE, k), lambda i: (i, 0)), _resident((k, n))],
        out_specs=pl.BlockSpec((TOKEN_TILE, n), lambda i: (i, 0)),
        out_shape=jax.ShapeDtypeStruct((t, n), F32),
        compiler_params=_params("parallel"),
        name="proj",
    )(x, w)


def _outproj_ln_kernel(o_ref, w_ref, x_ref, g_ref, b_ref, y_ref):
    y = jnp.dot(o_ref[...].astype(BF16), w_ref[...], preferred_element_type=F32)
    y_ref[...] = _layer_norm(DEEPNORM_ALPHA * x_ref[...] + y, g_ref[...], b_ref[...])


def _outproj_ln(o, w, x, g, b):
    t, k = o.shape
    d = w.shape[1]
    row = lambda i: (i, 0)
    return pl.pallas_call(
        _outproj_ln_kernel,
        grid=(t // TOKEN_TILE,),
        in_specs=[pl.BlockSpec((TOKEN_TILE, k), row), _resident((k, d)),
                  pl.BlockSpec((TOKEN_TILE, d), row), _resident((1, d)), _resident((1, d))],
        out_specs=pl.BlockSpec((TOKEN_TILE, d), row),
        out_shape=jax.ShapeDtypeStruct((t, d), F32),
        compiler_params=_params("parallel"),
        name="outproj_ln",
    )(o, w, x, g.reshape(1, d), b.reshape(1, d))


def _ffn_kernel(x_ref, wgu_ref, wd_ref, g_ref, b_ref, y_ref, h_ref, *, hidden, chunks, out_chunks):
    x = x_ref[...]
    xb = x.astype(BF16)
    for c0, cw in chunks:
        gate = jnp.dot(xb, wgu_ref[:, c0:c0 + cw], preferred_element_type=F32)
        up = jnp.dot(xb, wgu_ref[:, hidden + c0:hidden + c0 + cw], preferred_element_type=F32)
        h_ref[:, c0:c0 + cw] = (_silu(gate) * up).astype(BF16)
    for c0, cw in out_chunks:
        y_ref[:, c0:c0 + cw] = jnp.dot(h_ref[...], wd_ref[:, c0:c0 + cw], preferred_element_type=F32)
    y_ref[...] = _layer_norm(DEEPNORM_ALPHA * x + y_ref[...], g_ref[...], b_ref[...])


def _ffn(x, w_gu, w_down, g, b):
    t, d = x.shape
    hidden = w_down.shape[0]
    row = lambda i: (i, 0)
    return pl.pallas_call(
        functools.partial(_ffn_kernel, hidden=hidden, chunks=_col_chunks(hidden, 256),
                          out_chunks=_col_chunks(d)),
        grid=(t // TOKEN_TILE,),
        in_specs=[pl.BlockSpec((TOKEN_TILE, d), row), _resident((d, 2 * hidden)), _resident((hidden, d)),
                  _resident((1, d)), _resident((1, d))],
        out_specs=pl.BlockSpec((TOKEN_TILE, d), row),
        out_shape=jax.ShapeDtypeStruct((t, d), F32),
        scratch_shapes=[pltpu.VMEM((TOKEN_TILE, hidden), BF16)],
        compiler_params=_params("parallel"),
        name="ffn",
    )(x, w_gu, w_down, g.reshape(1, d), b.reshape(1, d))


def _split3(x):
    hi = x.astype(BF16)
    rest = x - hi.astype(F32)
    mid = rest.astype(BF16)
    return hi, mid, (rest - mid.astype(F32)).astype(BF16)


def _per_head(x, width):
    return jnp.stack([x[:, j * width:(j + 1) * width] for j in range(x.shape[1] // width)])


def _merge_heads(x):
    return jnp.concatenate([x[j] for j in range(x.shape[0])], axis=1)


def _gla_block(q, k, v, log_g, st):
    g, rows, dk = q.shape
    n = rows // CHUNK
    row = lax.broadcasted_iota(jnp.int32, (rows, rows), 0)
    col = lax.broadcasted_iota(jnp.int32, (rows, rows), 1)
    chunk_start = jnp.bitwise_and(row, -CHUNK)
    causal = jnp.logical_and(col <= row, col >= chunk_start)
    parts = [p for j in range(g) for p in _split3(log_g[j])]
    acc = jnp.dot(jnp.where(causal, 1.0, 0.0).astype(BF16), jnp.concatenate(parts, axis=1),
                  preferred_element_type=F32)
    cum = jnp.stack([acc[:, 3 * j * dk:(3 * j + 1) * dk] + acc[:, (3 * j + 1) * dk:(3 * j + 2) * dk]
                     + acc[:, (3 * j + 2) * dk:(3 * j + 3) * dk] for j in range(g)])
    lasts = [cum[:, (i + 1) * CHUNK - 1:(i + 1) * CHUNK, :] for i in range(n)]
    total = jnp.concatenate([jnp.broadcast_to(last, (g, CHUNK, dk)) for last in lasts], axis=1)
    q_dec = (q * jnp.exp(cum)).astype(BF16)
    k_inv = k * jnp.exp(-cum)
    k_end = (k * jnp.exp(total - cum)).astype(BF16)
    vb = v.astype(BF16)
    scores = jnp.where(causal, _bdot_nt(q_dec, k_inv), 0.0)
    o_intra = _bdot(scores, vb)
    o_inter = []
    for i in range(n):
        rs = slice(i * CHUNK, (i + 1) * CHUNK)
        o_inter.append(_bdot_nt(q_dec[:, rs, :], st))
        update = jnp.stack([_dot_tn(vb[j, rs, :], k_end[j, rs, :]) for j in range(g)])
        st = st * jnp.exp(lasts[i]) + update
    return o_intra + jnp.concatenate(o_inter, axis=1), st


def _gla_finish(o, norm_g, gate, o_ref, rows):
    o = o * lax.rsqrt(jnp.mean(o * o, axis=-1, keepdims=True) + 1e-6) * norm_g
    o_ref[rows, :] = _merge_heads(o) * _silu(gate)


def _gla_kernel(q_ref, k_ref, v_ref, g_ref, low_ref, wgk_ref, bgk_ref, ng_ref, o_ref, st_ref, *, n_blocks):
    st_ref[...] = jnp.zeros_like(st_ref)
    wgk = wgk_ref[...].astype(BF16)
    bgk = bgk_ref[...]
    norm_g = ng_ref[...]

    def body(c, carry):
        rows = pl.ds(pl.multiple_of(c * SUPER, SUPER), SUPER)
        logit = jnp.dot(low_ref[rows, :].astype(BF16), wgk, preferred_element_type=F32) + bgk
        log_g = _per_head(_log_sigmoid(logit) * (1.0 / GLA_GATE_NORMALIZER), GLA_DK)
        q = _per_head(q_ref[rows, :], GLA_DK) * (GLA_DK ** -0.5)
        o, st = _gla_block(q, _per_head(k_ref[rows, :], GLA_DK), _per_head(v_ref[rows, :], GLA_DV), log_g,
                           st_ref[...])
        st_ref[...] = st
        _gla_finish(o, norm_g, g_ref[rows, :], o_ref, rows)
        return carry

    lax.fori_loop(0, n_blocks, body, 0)


def _gla_mixer(proj, w_gk, b_gk, norm_g):
    bsz, s, _ = proj.shape
    dk, dv, h, gh = GLA_DK, GLA_DV, GLA_HEADS, MIX_GROUP
    groups = h // gh
    wgk_pad = jnp.zeros((LANES, h * dk), F32).at[:GLA_GATE_RANK].set(w_gk)
    seq = lambda width, off: pl.BlockSpec((None, s, gh * width), lambda b, i: (b, 0, off + i))
    return pl.pallas_call(
        functools.partial(_gla_kernel, n_blocks=s // SUPER),
        grid=(bsz, groups),
        in_specs=[seq(dk, 0), seq(dk, groups), seq(dv, (2 * h * dk) // (gh * dv)),
                  seq(dv, (2 * h * dk) // (gh * dv) + groups),
                  pl.BlockSpec((None, s, LANES), lambda b, i: (b, 0, (2 * h * dk + 2 * h * dv) // LANES)),
                  pl.BlockSpec((LANES, gh * dk), lambda b, i: (0, i)),
                  pl.BlockSpec((1, gh * dk), lambda b, i: (0, i)),
                  pl.BlockSpec((1, dv), lambda b, i: (0, 0))],
        out_specs=pl.BlockSpec((None, s, gh * dv), lambda b, i: (b, 0, i)),
        out_shape=jax.ShapeDtypeStruct((bsz, s, h * dv), F32),
        scratch_shapes=[pltpu.VMEM((gh, dv, dk), F32)],
        compiler_params=_params("parallel", "parallel"),
        name="gla",
    )(proj, proj, proj, proj, proj, wgk_pad, b_gk.reshape(1, -1), norm_g.reshape(1, -1))


def _hgrn_kernel(q_ref, f_ref, i_ref, g_ref, lb_ref, ng_ref, o_ref, st_ref, *, n_blocks, layer):
    st_ref[...] = jnp.zeros_like(st_ref)
    norm_g = ng_ref[...]
    raw = lb_ref[...]
    e = jnp.exp(raw - jnp.max(raw, axis=0, keepdims=True))
    soft = e / jnp.sum(e, axis=0, keepdims=True)
    lb = jnp.sum(soft[:layer + 1, :], axis=0, keepdims=True) - soft[0:1, :]

    def body(c, carry):
        rows = pl.ds(pl.multiple_of(c * SUPER, SUPER), SUPER)
        f = f_ref[rows, :]
        forget = lb + (1.0 - lb) * _sigmoid(f)
        k = (1.0 - lb) * _sigmoid(-f)
        o, st = _gla_block(_per_head(q_ref[rows, :], HEAD_DIM), _per_head(k, HEAD_DIM),
                           _per_head(i_ref[rows, :], HEAD_DIM), _per_head(jnp.log(forget), HEAD_DIM), st_ref[...])
        st_ref[...] = st
        _gla_finish(o, norm_g, g_ref[rows, :], o_ref, rows)
        return carry

    lax.fori_loop(0, n_blocks, body, 0)


def _hgrn_mixer(proj, lower_bounds, norm_g, layer):
    bsz, s, _ = proj.shape
    d, h, gh = HEAD_DIM, HGRN_HEADS, MIX_GROUP
    groups = h // gh
    seq = lambda off: pl.BlockSpec((None, s, gh * d), lambda b, i: (b, 0, off + i))
    return pl.pallas_call(
        functools.partial(_hgrn_kernel, n_blocks=s // SUPER, layer=layer),
        grid=(bsz, groups),
        in_specs=[seq(0), seq(groups), seq(2 * groups), seq(3 * groups),
                  pl.BlockSpec((DEPTH, gh * d), lambda b, i: (0, i)),
                  pl.BlockSpec((1, d), lambda b, i: (0, 0))],
        out_specs=pl.BlockSpec((None, s, gh * d), lambda b, i: (b, 0, i)),
        out_shape=jax.ShapeDtypeStruct((bsz, s, h * d), F32),
        scratch_shapes=[pltpu.VMEM((gh, d, d), F32)],
        compiler_params=_params("parallel", "parallel"),
        name="hgrn2",
    )(proj, proj, proj, proj, lower_bounds, norm_g.reshape(1, -1))


def _gdn_kernel(q_ref, k_ref, v_ref, z_ref, ab_ref, cwq_ref, cwk_ref, cwv_ref, alog_ref, dtb_ref, ng_ref,
                o_ref, st_ref, *, n_chunks):
    cs, gh, hd = GDN_CHUNK, GDN_GROUP, HEAD_DIM
    head0 = pl.program_id(1) * gh
    row = lax.broadcasted_iota(jnp.int32, (cs, cs), 0)
    col = lax.broadcasted_iota(jnp.int32, (cs, cs), 1)
    incl, strict, eye = row >= col, row > col, row == col
    tril_b = jnp.where(incl, 1.0, 0.0).astype(BF16)
    eye_f = jnp.where(eye, 1.0, 0.0)
    block_bits = jnp.bitwise_xor(row, col)
    lane = lax.broadcasted_iota(jnp.int32, (1, LANES), 1)
    norm_g = ng_ref[...]
    neg_a = -jnp.exp(alog_ref[...])
    dt_bias = dtb_ref[...]
    st_ref[...] = jnp.zeros_like(st_ref)

    def conv_silu(ref, cw_ref, c):
        r0 = pl.multiple_of(c * cs, cs)
        cur = ref[pl.ds(r0, cs), :]
        prev = ref[pl.ds(pl.multiple_of(jnp.maximum(r0 - SUBLANES, 0), SUBLANES), SUBLANES), :]
        prev = jnp.where(c > 0, prev, 0.0)
        ext = jnp.concatenate([prev, cur], axis=0)
        w = cw_ref[...]
        y = ext[SUBLANES:, :] * w[GDN_CONV - 1:GDN_CONV, :]
        for j in range(GDN_CONV - 1):
            lo = SUBLANES - (GDN_CONV - 1) + j
            y = y + ext[lo:lo + cs, :] * w[j:j + 1, :]
        return _silu(y)

    def per_head(x):
        return jnp.stack([x[:, j * hd:(j + 1) * hd] for j in range(gh)])

    def pick(x, first):
        return jnp.stack([jnp.sum(jnp.where(lane == first + j, x, 0.0), axis=1, keepdims=True)
                          for j in range(gh)])

    def body(c, carry):
        rows = pl.ds(pl.multiple_of(c * cs, cs), cs)
        q = per_head(conv_silu(q_ref, cwq_ref, c))
        k = per_head(conv_silu(k_ref, cwk_ref, c))
        v = per_head(conv_silu(v_ref, cwv_ref, c))
        q = q * lax.rsqrt(jnp.sum(q * q, axis=-1, keepdims=True) + 1e-6) * (hd ** -0.5)
        k = k * lax.rsqrt(jnp.sum(k * k, axis=-1, keepdims=True) + 1e-6)
        ab = ab_ref[rows, :]
        g = pick(neg_a * _softplus(ab + dt_bias), head0)
        beta = pick(_sigmoid(ab), head0 + GDN_HEADS)
        g_wide = jnp.concatenate([jnp.broadcast_to(g[j], (cs, hd)) for j in range(gh)], axis=1)
        cum = per_head(sum(jnp.dot(tril_b, part, preferred_element_type=F32) for part in _split3(g_wide)))
        cum_wide = jnp.concatenate([cum] * (cs // hd), axis=2)
        cum_row = jnp.sum(jnp.where(eye, cum_wide, 0.0), axis=1, keepdims=True)
        decay = jnp.where(incl, jnp.exp(jnp.where(incl, cum_wide - cum_row, 0.0)), 0.0)
        k_beta = k * beta
        lower = jnp.where(strict, _bdot_nt(k_beta, k) * decay, 0.0)
        base_bits = int(np.log2(GDN_INV_BASE))
        power = -jnp.where(jnp.right_shift(block_bits, base_bits) == 0, lower, 0.0)
        inv = eye_f + power
        for _ in range(base_bits - 1):
            power = _bdot(power, power)
            inv = inv + _bdot(inv, power)
        for bits in range(base_bits, int(np.log2(cs))):
            cross = jnp.where(jnp.right_shift(block_bits, bits) == 1, lower, 0.0)
            inv = inv - _bdot(_bdot(inv, cross), inv)
        e_cum = jnp.exp(cum)
        sol = _bdot(inv, jnp.concatenate([v * beta, k_beta * e_cum], axis=2))
        u, w = sol[:, :, :hd], sol[:, :, hd:]
        qk = _bdot_nt(q, k) * decay
        last = cum[:, cs - 1:cs, :]
        k_end = k * jnp.exp(last - cum)
        st = st_ref[...]
        t = _bdot(jnp.concatenate([w, q * e_cum], axis=1), st)
        v_new = u - t[:, :cs, :]
        o = t[:, cs:, :] + _bdot(qk, v_new)
        st_ref[...] = st * jnp.exp(last) + jnp.stack([_dot_tn(k_end[j], v_new[j]) for j in range(gh)])
        o = o * lax.rsqrt(jnp.mean(o * o, axis=-1, keepdims=True) + 1e-6) * norm_g
        o_ref[rows, :] = jnp.concatenate([o[j] for j in range(gh)], axis=1) * _silu(z_ref[rows, :])
        return carry

    lax.fori_loop(0, n_chunks, body, 0)


def _gdn_mixer(proj, conv_w, a_log, dt_bias, norm_g):
    bsz, s, _ = proj.shape
    d, h = HEAD_DIM, GDN_HEADS
    n_chunks = s // GDN_CHUNK
    gh = GDN_GROUP
    groups = h // gh
    seq = lambda off: pl.BlockSpec((None, s, gh * d), lambda b, i: (b, 0, off + i))
    cw = lambda off: pl.BlockSpec((GDN_CONV, gh * d), lambda b, i: (0, off + i))
    vec = pl.BlockSpec((1, LANES), lambda b, i: (0, 0))
    pad = lambda a: jnp.zeros((1, LANES), F32).at[0, :h].set(a)
    return pl.pallas_call(
        functools.partial(_gdn_kernel, n_chunks=n_chunks),
        grid=(bsz, groups),
        in_specs=[seq(0), seq(groups), seq(2 * groups), seq(3 * groups),
                  pl.BlockSpec((None, s, LANES), lambda b, i: (b, 0, 4 * h)),
                  cw(0), cw(groups), cw(2 * groups), vec, vec, vec],
        out_specs=pl.BlockSpec((None, s, gh * d), lambda b, i: (b, 0, i)),
        out_shape=jax.ShapeDtypeStruct((bsz, s, h * d), F32),
        scratch_shapes=[pltpu.VMEM((gh, d, d), F32)],
        compiler_params=_params("parallel", "parallel"),
        name="gdn",
    )(proj, proj, proj, proj, proj, conv_w, conv_w, conv_w, pad(a_log), pad(dt_bias),
      norm_g.reshape(1, -1))


def _rope_table_kernel(pos_ref, invf_ref, cos_ref, sin_ref):
    ang = pos_ref[...].astype(F32) * invf_ref[...]
    lane = lax.broadcasted_iota(jnp.int32, ang.shape, 1)
    cos_ref[...] = jnp.cos(ang)
    s = jnp.sin(ang)
    sin_ref[...] = jnp.where(lane < ROPE_DIM // 2, -s, s)


def _rope_tables(positions):
    bsz, s = positions.shape
    half = ROPE_DIM // 2
    inv_freq = (np.float32(ROPE_THETA) ** (-np.arange(0, ROPE_DIM, 2, dtype=np.float32) / ROPE_DIM)).astype(np.float32)
    invf = np.zeros((1, LANES), np.float32)
    invf[0, :half] = inv_freq
    invf[0, half:ROPE_DIM] = inv_freq
    rows = MOBA_BLOCK
    out = jax.ShapeDtypeStruct((bsz, s, LANES), F32)
    return pl.pallas_call(
        _rope_table_kernel,
        grid=(bsz, s // rows),
        in_specs=[pl.BlockSpec((None, rows, 1), lambda b, i: (b, i, 0)),
                  pl.BlockSpec((1, LANES), lambda b, i: (0, 0))],
        out_specs=[pl.BlockSpec((None, rows, LANES), lambda b, i: (b, i, 0))] * 2,
        out_shape=[out, out],
        compiler_params=_params("parallel", "parallel"),
        name="rope_table",
    )(positions.reshape(bsz, s, 1), jnp.asarray(invf))


def _moba_kernel(q_ref, k_ref, v_ref, cos_ref, sin_ref, o_ref, kr_s, vb_s, km_s, s_s, *, n_blocks):
    blk, hd, gh = MOBA_BLOCK, HEAD_DIM, MIX_GROUP
    half = ROPE_DIM // 2
    scale = hd ** -0.5
    wide = gh * hd
    causal, _, _ = _tri_masks(blk)
    src = lax.broadcasted_iota(jnp.int32, (wide, wide), 0)
    dst = lax.broadcasted_iota(jnp.int32, (wide, wide), 1)
    dst_dim = jnp.bitwise_and(dst, hd - 1)
    shift = jnp.where(dst_dim < half, half, jnp.where(dst_dim < 2 * half, -half, wide))
    swap_b = jnp.where(src == dst + shift, 1.0, 0.0).astype(BF16)
    blk_id = lax.broadcasted_iota(jnp.int32, (1, n_blocks, 1), 1)
    owner = lax.broadcasted_iota(jnp.int32, (n_blocks, n_blocks * blk), 0)
    column = lax.broadcasted_iota(jnp.int32, (n_blocks, n_blocks * blk), 1)
    spread_b = jnp.where(jnp.right_shift(column, int(np.log2(blk))) == owner, 1.0, 0.0).astype(BF16)

    def rope(x, rows):
        moved = jnp.dot(jnp.concatenate(_split3(x), axis=0), swap_b, preferred_element_type=F32)
        partner = moved[:blk] + moved[blk:2 * blk] + moved[2 * blk:]
        cos = jnp.concatenate([cos_ref[rows, :]] * gh, axis=1)
        sin = jnp.concatenate([sin_ref[rows, :]] * gh, axis=1)
        return x * cos + partner * sin

    for j in range(n_blocks):
        rows = slice(j * blk, (j + 1) * blk)
        kr = _per_head(rope(k_ref[rows, :], rows), hd)
        kr_s[:, rows, :] = kr.astype(BF16)
        km_s[:, j:j + 1, :] = jnp.mean(kr, axis=1, keepdims=True)
        vb_s[:, rows, :hd] = _per_head(v_ref[rows, :], hd).astype(BF16)
        vb_s[:, rows, hd:] = jnp.ones((gh, blk, hd), BF16)

    for qb in range(n_blocks):
        rows = slice(qb * blk, (qb + 1) * blk)
        qr = _per_head(rope(q_ref[rows, :], rows), hd)
        qb16 = (qr * (scale * LOG2_E)).astype(BF16)
        if qb > 0:
            gate = jnp.where(blk_id < qb, _bdot_nt_f32(km_s[...], qr), NEG_INF)
            picked = []
            for j in range(qb):
                gj = gate[:, j:j + 1, :]
                beats = jnp.where(blk_id < j, jnp.where(gate >= gj, 1.0, 0.0), jnp.where(gate > gj, 1.0, 0.0))
                rank = jnp.sum(beats, axis=1, keepdims=True)
                picked.append(jnp.where(rank < MOBA_TOPK, 1.0, 0.0))
            picked.append(jnp.zeros((gh, n_blocks - qb, blk), F32))
            picked = jnp.concatenate(picked, axis=1)
            far_w = qb * blk
            keep = jnp.stack([_dot_tn(picked[g], spread_b[:, :far_w]) for g in range(gh)])
            far = _bdot_nt(qb16, kr_s[:, :far_w, :])
            s_s[:, :, :far_w] = jnp.where(keep > 0.5, far, NEG_INF)
        own = _bdot_nt(qb16, kr_s[:, rows, :])
        s_s[:, :, rows] = jnp.where(causal, own, NEG_INF)
        width = (qb + 1) * blk
        top = s_s[:, :, :LANES]
        for c0 in range(LANES, width, LANES):
            top = jnp.maximum(top, s_s[:, :, c0:c0 + LANES])
        p = jnp.exp2(s_s[:, :, :width] - jnp.max(top, axis=2, keepdims=True))
        pv = _bdot(p, vb_s[:, :width, :])
        o_ref[rows, :] = _merge_heads(pv[:, :, :hd] / pv[:, :, hd:])


def _moba_mixer(proj, cos, sin):
    bsz, s, _ = proj.shape
    d, h, gh = HEAD_DIM, MOBA_HEADS, MIX_GROUP
    groups = h // gh
    seq = lambda off: pl.BlockSpec((None, s, gh * d), lambda b, i: (b, 0, off + i))
    tab = pl.BlockSpec((None, s, LANES), lambda b, i: (b, 0, 0))
    return pl.pallas_call(
        functools.partial(_moba_kernel, n_blocks=s // MOBA_BLOCK),
        grid=(bsz, groups),
        in_specs=[seq(0), seq(groups), seq(2 * groups), tab, tab],
        out_specs=pl.BlockSpec((None, s, gh * d), lambda b, i: (b, 0, i)),
        out_shape=jax.ShapeDtypeStruct((bsz, s, h * d), F32),
        scratch_shapes=[pltpu.VMEM((gh, s, d), BF16),
                        pltpu.VMEM((gh, s, 2 * d), BF16),
                        pltpu.VMEM((gh, s // MOBA_BLOCK, d), F32),
                        pltpu.VMEM((gh, MOBA_BLOCK, s), F32)],
        compiler_params=_params("parallel", "parallel"),
        name="moba",
    )(proj, proj, proj, cos, sin)


def _pad_cols(w, multiple=LANES):
    n = w.shape[1]
    return jnp.pad(w, ((0, 0), (0, -n % multiple)))


def kernel(x, positions, gla_w_in, gla_w_gk, gla_b_gk, gla_norm_g, gla_w_out, moba_w_in, moba_w_out,
           gdn_w_in, gdn_conv_w, gdn_a_log, gdn_dt_bias, gdn_norm_g, gdn_w_out, hgrn_lower_bounds,
           hgrn_w_in, hgrn_norm_g, hgrn_w_out, ffn_w_gu, ffn_w_down, ln_g, ln_b):
    bsz, s, d = x.shape
    assert (s % MOBA_BLOCK, (bsz * s) % TOKEN_TILE, d) == (0, 0, D_MODEL)
    t = bsz * s
    xf = x.reshape(t, d)
    cos = sin = None
    for i in range(DEPTH):
        kind, j = i % N_MIXERS, i // N_MIXERS
        if kind == 0:
            proj = _proj(xf, _pad_cols(gla_w_in[j]).astype(BF16)).reshape(bsz, s, -1)
            o = _gla_mixer(proj, gla_w_gk[j], gla_b_gk[j], gla_norm_g[j])
            w_out = gla_w_out[j]
        elif kind == 1:
            if cos is None:
                cos, sin = _rope_tables(positions)
            proj = _proj(xf, moba_w_in[j].astype(BF16)).reshape(bsz, s, -1)
            o = _moba_mixer(proj, cos, sin)
            w_out = moba_w_out[j]
        elif kind == 2:
            proj = _proj(xf, _pad_cols(gdn_w_in[j]).astype(BF16)).reshape(bsz, s, -1)
            o = _gdn_mixer(proj, gdn_conv_w[j], gdn_a_log[j], gdn_dt_bias[j], gdn_norm_g[j])
            w_out = gdn_w_out[j]
        else:
            proj = _proj(xf, hgrn_w_in[j].astype(BF16)).reshape(bsz, s, -1)
            o = _hgrn_mixer(proj, hgrn_lower_bounds, hgrn_norm_g[j], i)
            w_out = hgrn_w_out[j]
        xf = _outproj_ln(o.reshape(t, d), w_out.astype(BF16), xf, ln_g[i, 0], ln_b[i, 0])
        xf = _ffn(xf, ffn_w_gu[i].astype(BF16), ffn_w_down[i].astype(BF16), ln_g[i, 1], ln_b[i, 1])
    return xf.reshape(bsz, s, d)
```

```python
import functools

import numpy as np
import jax
import jax.numpy as jnp
from jax import lax
from jax.experimental import pallas as pl
from jax.experimental.pallas import tpu as pltpu

F32 = jnp.float32
BF16 = jnp.bfloat16
HIGHEST = lax.Precision.HIGHEST

D_MODEL = 1024
DEPTH = 4
N_MIXERS = 4
HEAD_DIM = 128
CHUNK = 64
SUPER = 256
MIX_GROUP = 4
MOBA_GROUP = 2
GLA_HEADS = 4
GLA_DK = 128
GLA_DV = 256
GLA_GATE_RANK = 16
GLA_GATE_NORMALIZER = 16.0
MOBA_HEADS = 8
MOBA_BLOCK = 256
MOBA_TOPK = 3
ROPE_THETA = 500000.0
ROPE_DIM = 32
NEG_INF = -1e30
LOG2_E = 1.4426950408889634
GDN_HEADS = 8
GDN_CONV = 4
GDN_CHUNK = 256
GDN_INV_BASE = 16
GDN_GROUP = 4
GLA_COLS = (512, 512, 1024, 1024, 16)
GDN_COLS = (3072, 1024, 8, 8)
HGRN_COLS = (1024, 1024, 1024, 1024)
HGRN_HEADS = 8
FFN_HIDDEN = 2816
DEEPNORM_ALPHA = (2.0 * DEPTH) ** 0.25

LANES = 128
SUBLANES = 8
BF16_ROWS = 16
VMEM_LIMIT = 56 * 1024 * 1024
TOKEN_TILE = 512
COL_CHUNK = 512


def _dot_tn(a, b):
    return lax.dot_general(a.astype(BF16), b.astype(BF16), (((0,), (0,)), ((), ())),
                           preferred_element_type=F32)


def _bdot(a, b):
    return lax.dot_general(a.astype(BF16), b.astype(BF16), (((2,), (1,)), ((0,), (0,))),
                           preferred_element_type=F32)


def _bdot_nt(a, b):
    return lax.dot_general(a.astype(BF16), b.astype(BF16), (((2,), (2,)), ((0,), (0,))),
                           preferred_element_type=F32)


def _bdot_nt_f32(a, b):
    return lax.dot_general(a, b, (((2,), (2,)), ((0,), (0,))), precision=HIGHEST,
                           preferred_element_type=F32)


def _sigmoid(x):
    return 1.0 / (1.0 + jnp.exp(-x))


def _silu(x):
    return x * _sigmoid(x)


def _softplus(x):
    return jnp.maximum(x, 0.0) + jnp.log1p(jnp.exp(-jnp.abs(x)))


def _log_sigmoid(x):
    return -_softplus(-x)


def _col_chunks(n, width=COL_CHUNK):
    return tuple((c0, min(width, n - c0)) for c0 in range(0, n, width))


def _params(*semantics):
    return pltpu.CompilerParams(dimension_semantics=semantics, vmem_limit_bytes=VMEM_LIMIT)


def _resident(shape):
    zeros = (0,) * len(shape)
    return pl.BlockSpec(shape, lambda *_: zeros, pipeline_mode=pl.Buffered(1))


def _layer_norm(z, g, b):
    mu = jnp.mean(z, axis=-1, keepdims=True)
    zc = z - mu
    var = jnp.mean(zc * zc, axis=-1, keepdims=True)
    return zc * lax.rsqrt(var + 1e-5) * g + b


def _tri_masks(n):
    row = lax.broadcasted_iota(jnp.int32, (n, n), 0)
    col = lax.broadcasted_iota(jnp.int32, (n, n), 1)
    return row >= col, row > col, row == col


def _proj_kernel(x_ref, *refs):
    xb = x_ref[...].astype(BF16)
    n_out = len(refs) // 2
    for w_ref, o_ref in zip(refs[:n_out], refs[n_out:]):
        for c0, cw in _col_chunks(w_ref.shape[1]):
            o_ref[:, c0:c0 + cw] = jnp.dot(xb, w_ref[:, c0:c0 + cw].astype(BF16),
                                           preferred_element_type=F32).astype(o_ref.dtype)


def _proj(x, windows):
    t, k = x.shape
    row = lambda i: (i, 0)
    w_specs = []
    for _, c0, n, _ in windows:
        assert c0 % n == 0 and n % LANES == 0
        w_specs.append(pl.BlockSpec((k, n), functools.partial(lambda j, i: (0, j), c0 // n),
                                    pipeline_mode=pl.Buffered(1)))
    return pl.pallas_call(
        _proj_kernel,
        grid=(t // TOKEN_TILE,),
        in_specs=[pl.BlockSpec((TOKEN_TILE, k), row)] + w_specs,
        out_specs=[pl.BlockSpec((TOKEN_TILE, n), row) for _, _, n, _ in windows],
        out_shape=[jax.ShapeDtypeStruct((t, n), dt) for _, _, n, dt in windows],
        compiler_params=_params("parallel"),
        name="proj",
    )(x, *[w for w, _, _, _ in windows])


def _outproj_ln_kernel(o_ref, w_ref, x_ref, g_ref, b_ref, y_ref):
    y = jnp.dot(o_ref[...].astype(BF16), w_ref[...].astype(BF16), preferred_element_type=F32)
    y_ref[...] = _layer_norm(DEEPNORM_ALPHA * x_ref[...] + y, g_ref[...], b_ref[...])


def _outproj_ln(o, w, x, g, b):
    t, k = o.shape
    d = w.shape[1]
    row = lambda i: (i, 0)
    return pl.pallas_call(
        _outproj_ln_kernel,
        grid=(t // TOKEN_TILE,),
        in_specs=[pl.BlockSpec((TOKEN_TILE, k), row), _resident((k, d)),
                  pl.BlockSpec((TOKEN_TILE, d), row), _resident((1, d)), _resident((1, d))],
        out_specs=pl.BlockSpec((TOKEN_TILE, d), row),
        out_shape=jax.ShapeDtypeStruct((t, d), F32),
        compiler_params=_params("parallel"),
        name="outproj_ln",
    )(o, w, x, g.reshape(1, d), b.reshape(1, d))


def _ffn_kernel(x_ref, wgu_ref, wd_ref, g_ref, b_ref, y_ref, h_ref, *, hidden, chunks, out_chunks):
    x = x_ref[...]
    xb = x.astype(BF16)
    for c0, cw in chunks:
        gate = jnp.dot(xb, wgu_ref[:, c0:c0 + cw].astype(BF16), preferred_element_type=F32)
        up = jnp.dot(xb, wgu_ref[:, hidden + c0:hidden + c0 + cw].astype(BF16), preferred_element_type=F32)
        h_ref[:, c0:c0 + cw] = (_silu(gate) * up).astype(BF16)
    for c0, cw in out_chunks:
        y_ref[:, c0:c0 + cw] = jnp.dot(h_ref[...], wd_ref[:, c0:c0 + cw].astype(BF16),
                                       preferred_element_type=F32)
    y_ref[...] = _layer_norm(DEEPNORM_ALPHA * x + y_ref[...], g_ref[...], b_ref[...])


def _ffn(x, w_gu, w_down, g, b):
    t, d = x.shape
    hidden = w_down.shape[0]
    row = lambda i: (i, 0)
    return pl.pallas_call(
        functools.partial(_ffn_kernel, hidden=hidden, chunks=_col_chunks(hidden, 256),
                          out_chunks=_col_chunks(d)),
        grid=(t // TOKEN_TILE,),
        in_specs=[pl.BlockSpec((TOKEN_TILE, d), row), _resident((d, 2 * hidden)), _resident((hidden, d)),
                  _resident((1, d)), _resident((1, d))],
        out_specs=pl.BlockSpec((TOKEN_TILE, d), row),
        out_shape=jax.ShapeDtypeStruct((t, d), F32),
        scratch_shapes=[pltpu.VMEM((TOKEN_TILE, hidden), BF16)],
        compiler_params=_params("parallel"),
        name="ffn",
    )(x, w_gu, w_down, g.reshape(1, d), b.reshape(1, d))


def _split3(x):
    hi = x.astype(BF16)
    rest = x - hi.astype(F32)
    mid = rest.astype(BF16)
    return hi, mid, (rest - mid.astype(F32)).astype(BF16)


def _per_head(x, width):
    return jnp.stack([x[:, j * width:(j + 1) * width] for j in range(x.shape[1] // width)])


def _merge_heads(x):
    return jnp.concatenate([x[j] for j in range(x.shape[0])], axis=1)


def _gla_block(q, k, v, log_g, st):
    g, rows, dk = q.shape
    n = rows // CHUNK
    row = lax.broadcasted_iota(jnp.int32, (rows, rows), 0)
    col = lax.broadcasted_iota(jnp.int32, (rows, rows), 1)
    chunk_start = jnp.bitwise_and(row, -CHUNK)
    causal = jnp.logical_and(col <= row, col >= chunk_start)
    parts = [p for j in range(g) for p in _split3(log_g[j])]
    acc = jnp.dot(jnp.where(causal, 1.0, 0.0).astype(BF16), jnp.concatenate(parts, axis=1),
                  preferred_element_type=F32)
    cum = jnp.stack([acc[:, 3 * j * dk:(3 * j + 1) * dk] + acc[:, (3 * j + 1) * dk:(3 * j + 2) * dk]
                     + acc[:, (3 * j + 2) * dk:(3 * j + 3) * dk] for j in range(g)])
    lasts = [cum[:, (i + 1) * CHUNK - 1:(i + 1) * CHUNK, :] for i in range(n)]
    total = jnp.concatenate([jnp.broadcast_to(last, (g, CHUNK, dk)) for last in lasts], axis=1)
    q_dec = (q * jnp.exp(cum)).astype(BF16)
    k_inv = k * jnp.exp(-cum)
    k_end = (k * jnp.exp(total - cum)).astype(BF16)
    vb = v.astype(BF16)
    scores = jnp.where(causal, _bdot_nt(q_dec, k_inv), 0.0)
    o_intra = _bdot(scores, vb)
    o_inter = []
    for i in range(n):
        rs = slice(i * CHUNK, (i + 1) * CHUNK)
        o_inter.append(_bdot_nt(q_dec[:, rs, :], st))
        update = jnp.stack([_dot_tn(vb[j, rs, :], k_end[j, rs, :]) for j in range(g)])
        st = st * jnp.exp(lasts[i]) + update
    return o_intra + jnp.concatenate(o_inter, axis=1), st


def _gla_finish(o, norm_g, gate, o_ref, rows):
    o = o * lax.rsqrt(jnp.mean(o * o, axis=-1, keepdims=True) + 1e-6) * norm_g
    o_ref[rows, :] = (_merge_heads(o) * _silu(gate.astype(F32))).astype(o_ref.dtype)


def _gla_kernel(q_ref, k_ref, v_ref, g_ref, low_ref, wgk_ref, bgk_ref, ng_ref, o_ref, st_ref, *, n_blocks):
    st_ref[...] = jnp.zeros_like(st_ref)
    wgk = wgk_ref[...].astype(BF16)
    bgk = bgk_ref[...]
    norm_g = ng_ref[...]

    def body(c, carry):
        rows = pl.ds(pl.multiple_of(c * SUPER, SUPER), SUPER)
        logit = jnp.dot(low_ref[rows, :].astype(BF16), wgk, preferred_element_type=F32) + bgk
        log_g = _per_head(_log_sigmoid(logit) * (1.0 / GLA_GATE_NORMALIZER), GLA_DK)
        q = _per_head(q_ref[rows, :].astype(F32), GLA_DK) * (GLA_DK ** -0.5)
        o, st = _gla_block(q, _per_head(k_ref[rows, :].astype(F32), GLA_DK), _per_head(v_ref[rows, :], GLA_DV),
                           log_g, st_ref[...])
        st_ref[...] = st
        _gla_finish(o, norm_g, g_ref[rows, :], o_ref, rows)
        return carry

    lax.fori_loop(0, n_blocks, body, 0)


def _gla_mixer(proj, low, w_gk, b_gk, norm_g):
    bsz, s, _ = proj.shape
    dk, dv, h, gh = GLA_DK, GLA_DV, GLA_HEADS, MIX_GROUP
    groups = h // gh
    wgk_pad = jnp.zeros((LANES, h * dk), F32).at[:GLA_GATE_RANK].set(w_gk)
    seq = lambda width, off: pl.BlockSpec((None, s, gh * width), lambda b, i: (b, 0, off + i))
    return pl.pallas_call(
        functools.partial(_gla_kernel, n_blocks=s // SUPER),
        grid=(bsz, groups),
        in_specs=[seq(dk, 0), seq(dk, groups), seq(dv, (2 * h * dk) // (gh * dv)),
                  seq(dv, (2 * h * dk) // (gh * dv) + groups),
                  pl.BlockSpec((None, s, LANES), lambda b, i: (b, 0, 0)),
                  pl.BlockSpec((LANES, gh * dk), lambda b, i: (0, i)),
                  pl.BlockSpec((1, gh * dk), lambda b, i: (0, i)),
                  pl.BlockSpec((1, dv), lambda b, i: (0, 0))],
        out_specs=pl.BlockSpec((None, s, gh * dv), lambda b, i: (b, 0, i)),
        out_shape=jax.ShapeDtypeStruct((bsz, s, h * dv), BF16),
        scratch_shapes=[pltpu.VMEM((gh, dv, dk), F32)],
        compiler_params=_params("parallel", "parallel"),
        name="gla",
    )(proj, proj, proj, proj, low, wgk_pad, b_gk.reshape(1, -1), norm_g.reshape(1, -1))


def _hgrn_kernel(q_ref, f_ref, i_ref, g_ref, lb_ref, ng_ref, o_ref, st_ref, *, n_blocks, layer):
    st_ref[...] = jnp.zeros_like(st_ref)
    norm_g = ng_ref[...]
    raw = lb_ref[...]
    e = jnp.exp(raw - jnp.max(raw, axis=0, keepdims=True))
    soft = e / jnp.sum(e, axis=0, keepdims=True)
    lb = jnp.sum(soft[:layer + 1, :], axis=0, keepdims=True) - soft[0:1, :]

    def body(c, carry):
        rows = pl.ds(pl.multiple_of(c * SUPER, SUPER), SUPER)
        f = f_ref[rows, :]
        forget = lb + (1.0 - lb) * _sigmoid(f)
        k = (1.0 - lb) * _sigmoid(-f)
        o, st = _gla_block(_per_head(q_ref[rows, :].astype(F32), HEAD_DIM), _per_head(k, HEAD_DIM),
                           _per_head(i_ref[rows, :], HEAD_DIM), _per_head(jnp.log(forget), HEAD_DIM), st_ref[...])
        st_ref[...] = st
        _gla_finish(o, norm_g, g_ref[rows, :], o_ref, rows)
        return carry

    lax.fori_loop(0, n_blocks, body, 0)


def _hgrn_mixer(q, f_logit, ig, lower_bounds, norm_g, layer):
    bsz, s, _ = q.shape
    d, h, gh = HEAD_DIM, HGRN_HEADS, MIX_GROUP
    groups = h // gh
    seq = lambda off: pl.BlockSpec((None, s, gh * d), lambda b, i: (b, 0, off + i))
    return pl.pallas_call(
        functools.partial(_hgrn_kernel, n_blocks=s // SUPER, layer=layer),
        grid=(bsz, groups),
        in_specs=[seq(0), seq(0), seq(0), seq(groups),
                  pl.BlockSpec((DEPTH, gh * d), lambda b, i: (0, i)),
                  pl.BlockSpec((1, d), lambda b, i: (0, 0))],
        out_specs=pl.BlockSpec((None, s, gh * d), lambda b, i: (b, 0, i)),
        out_shape=jax.ShapeDtypeStruct((bsz, s, h * d), BF16),
        scratch_shapes=[pltpu.VMEM((gh, d, d), F32)],
        compiler_params=_params("parallel", "parallel"),
        name="hgrn2",
    )(q, f_logit, ig, ig, lower_bounds, norm_g.reshape(1, -1))


def _gdn_kernel(q_ref, k_ref, v_ref, z_ref, ab_ref, cwq_ref, cwk_ref, cwv_ref, alog_ref, dtb_ref, ng_ref,
                o_ref, st_ref, *, n_chunks):
    cs, gh, hd = GDN_CHUNK, GDN_GROUP, HEAD_DIM
    head0 = pl.program_id(1) * gh
    row = lax.broadcasted_iota(jnp.int32, (cs, cs), 0)
    col = lax.broadcasted_iota(jnp.int32, (cs, cs), 1)
    incl, strict, eye = row >= col, row > col, row == col
    tril_b = jnp.where(incl, 1.0, 0.0).astype(BF16)
    eye_f = jnp.where(eye, 1.0, 0.0)
    block_bits = jnp.bitwise_xor(row, col)
    lane = lax.broadcasted_iota(jnp.int32, (1, LANES), 1)
    norm_g = ng_ref[...]
    neg_a = -jnp.exp(alog_ref[...])
    dt_bias = dtb_ref[...]
    st_ref[...] = jnp.zeros_like(st_ref)

    def conv_silu(ref, cw_ref, c):
        r0 = pl.multiple_of(c * cs, cs)
        hist = BF16_ROWS
        cur = ref[pl.ds(r0, cs), :].astype(F32)
        prev = ref[pl.ds(pl.multiple_of(jnp.maximum(r0 - hist, 0), hist), hist), :].astype(F32)
        prev = jnp.where(c > 0, prev, 0.0)
        ext = jnp.concatenate([prev, cur], axis=0)
        w = cw_ref[...]
        y = ext[hist:, :] * w[GDN_CONV - 1:GDN_CONV, :]
        for j in range(GDN_CONV - 1):
            lo = hist - (GDN_CONV - 1) + j
            y = y + ext[lo:lo + cs, :] * w[j:j + 1, :]
        return _silu(y)

    def per_head(x):
        return jnp.stack([x[:, j * hd:(j + 1) * hd] for j in range(gh)])

    def pick(x, first):
        return jnp.stack([jnp.sum(jnp.where(lane == first + j, x, 0.0), axis=1, keepdims=True)
                          for j in range(gh)])

    def body(c, carry):
        rows = pl.ds(pl.multiple_of(c * cs, cs), cs)
        q = per_head(conv_silu(q_ref, cwq_ref, c))
        k = per_head(conv_silu(k_ref, cwk_ref, c))
        v = per_head(conv_silu(v_ref, cwv_ref, c))
        q = q * lax.rsqrt(jnp.sum(q * q, axis=-1, keepdims=True) + 1e-6) * (hd ** -0.5)
        k = k * lax.rsqrt(jnp.sum(k * k, axis=-1, keepdims=True) + 1e-6)
        ab = ab_ref[rows, :]
        g = pick(neg_a * _softplus(ab + dt_bias), head0)
        beta = pick(_sigmoid(ab), head0 + GDN_HEADS)
        g_wide = jnp.concatenate([jnp.broadcast_to(g[j], (cs, hd)) for j in range(gh)], axis=1)
        cum = per_head(sum(jnp.dot(tril_b, part, preferred_element_type=F32) for part in _split3(g_wide)))
        cum_wide = jnp.concatenate([cum] * (cs // hd), axis=2)
        cum_row = jnp.sum(jnp.where(eye, cum_wide, 0.0), axis=1, keepdims=True)
        decay = jnp.where(incl, jnp.exp(jnp.where(incl, cum_wide - cum_row, 0.0)), 0.0)
        k_beta = k * beta
        lower = jnp.where(strict, _bdot_nt(k_beta, k) * decay, 0.0)
        base_bits = int(np.log2(GDN_INV_BASE))
        power = -jnp.where(jnp.right_shift(block_bits, base_bits) == 0, lower, 0.0)
        inv = eye_f + power
        for _ in range(base_bits - 1):
            power = _bdot(power, power)
            inv = inv + _bdot(inv, power)
        for bits in range(base_bits, int(np.log2(cs))):
            cross = jnp.where(jnp.right_shift(block_bits, bits) == 1, lower, 0.0)
            inv = inv - _bdot(_bdot(inv, cross), inv)
        e_cum = jnp.exp(cum)
        sol = _bdot(inv, jnp.concatenate([v * beta, k_beta * e_cum], axis=2))
        u, w = sol[:, :, :hd], sol[:, :, hd:]
        qk = _bdot_nt(q, k) * decay
        last = cum[:, cs - 1:cs, :]
        k_end = k * jnp.exp(last - cum)
        st = st_ref[...]
        t = _bdot(jnp.concatenate([w, q * e_cum], axis=1), st)
        v_new = u - t[:, :cs, :]
        o = t[:, cs:, :] + _bdot(qk, v_new)
        st_ref[...] = st * jnp.exp(last) + jnp.stack([_dot_tn(k_end[j], v_new[j]) for j in range(gh)])
        o = o * lax.rsqrt(jnp.mean(o * o, axis=-1, keepdims=True) + 1e-6) * norm_g
        o_ref[rows, :] = (_merge_heads(o) * _silu(z_ref[rows, :].astype(F32))).astype(o_ref.dtype)
        return carry

    lax.fori_loop(0, n_chunks, body, 0)


def _gdn_mixer(proj, ab, conv_w, a_log, dt_bias, norm_g):
    bsz, s, _ = proj.shape
    d, h = HEAD_DIM, GDN_HEADS
    n_chunks = s // GDN_CHUNK
    gh = GDN_GROUP
    groups = h // gh
    seq = lambda off: pl.BlockSpec((None, s, gh * d), lambda b, i: (b, 0, off + i))
    cw = lambda off: pl.BlockSpec((GDN_CONV, gh * d), lambda b, i: (0, off + i))
    vec = pl.BlockSpec((1, LANES), lambda b, i: (0, 0))
    pad = lambda a: jnp.zeros((1, LANES), F32).at[0, :h].set(a)
    return pl.pallas_call(
        functools.partial(_gdn_kernel, n_chunks=n_chunks),
        grid=(bsz, groups),
        in_specs=[seq(0), seq(groups), seq(2 * groups), seq(3 * groups),
                  pl.BlockSpec((None, s, LANES), lambda b, i: (b, 0, 0)),
                  cw(0), cw(groups), cw(2 * groups), vec, vec, vec],
        out_specs=pl.BlockSpec((None, s, gh * d), lambda b, i: (b, 0, i)),
        out_shape=jax.ShapeDtypeStruct((bsz, s, h * d), BF16),
        scratch_shapes=[pltpu.VMEM((gh, d, d), F32)],
        compiler_params=_params("parallel", "parallel"),
        name="gdn",
    )(proj, proj, proj, proj, ab, conv_w, conv_w, conv_w, pad(a_log), pad(dt_bias),
      norm_g.reshape(1, -1))


def _rope_table_kernel(pos_ref, invf_ref, cos_ref, sin_ref):
    ang = pos_ref[...].astype(F32) * invf_ref[...]
    lane = lax.broadcasted_iota(jnp.int32, ang.shape, 1)
    cos_ref[...] = jnp.cos(ang)
    s = jnp.sin(ang)
    sin_ref[...] = jnp.where(lane < ROPE_DIM // 2, -s, s)


def _rope_tables(positions):
    bsz, s = positions.shape
    half = ROPE_DIM // 2
    inv_freq = (np.float32(ROPE_THETA) ** (-np.arange(0, ROPE_DIM, 2, dtype=np.float32) / ROPE_DIM)).astype(np.float32)
    invf = np.zeros((1, LANES), np.float32)
    invf[0, :half] = inv_freq
    invf[0, half:ROPE_DIM] = inv_freq
    rows = MOBA_BLOCK
    out = jax.ShapeDtypeStruct((bsz, s, LANES), F32)
    return pl.pallas_call(
        _rope_table_kernel,
        grid=(bsz, s // rows),
        in_specs=[pl.BlockSpec((None, rows, 1), lambda b, i: (b, i, 0)),
                  pl.BlockSpec((1, LANES), lambda b, i: (0, 0))],
        out_specs=[pl.BlockSpec((None, rows, LANES), lambda b, i: (b, i, 0))] * 2,
        out_shape=[out, out],
        compiler_params=_params("parallel", "parallel"),
        name="rope_table",
    )(positions.reshape(bsz, s, 1), jnp.asarray(invf))


def _moba_kernel(q_ref, k_ref, v_ref, cos_ref, sin_ref, o_ref, kr_s, vb_s, km_s, s_s, *, n_blocks):
    blk, hd, gh = MOBA_BLOCK, HEAD_DIM, MOBA_GROUP
    half = ROPE_DIM // 2
    scale = hd ** -0.5
    wide = gh * hd
    causal, _, _ = _tri_masks(blk)
    head_lane = jnp.bitwise_and(lax.broadcasted_iota(jnp.int32, (1, wide), 1), hd - 1)
    lane = lax.broadcasted_iota(jnp.int32, (1, hd), 1)
    blk_id = lax.broadcasted_iota(jnp.int32, (1, n_blocks, 1), 1)
    to_lanes = jnp.where(lax.broadcasted_iota(jnp.int32, (n_blocks, hd), 0)
                         == lax.broadcasted_iota(jnp.int32, (n_blocks, hd), 1), 1.0, 0.0).astype(BF16)

    def rope(x, rows):
        x = x.astype(F32)
        partner = jnp.where(head_lane < half, pltpu.roll(x, wide - half, 1), pltpu.roll(x, half, 1))
        cos = jnp.concatenate([cos_ref[rows, :]] * gh, axis=1)
        sin = jnp.concatenate([sin_ref[rows, :]] * gh, axis=1)
        return x * cos + partner * sin

    for j in range(n_blocks):
        rows = slice(j * blk, (j + 1) * blk)
        kr = _per_head(rope(k_ref[rows, :], rows), hd)
        kr_s[:, rows, :hd] = kr.astype(BF16)
        kr_s[:, rows, hd:] = jnp.broadcast_to(jnp.where(lane == j, 1.0, 0.0), (gh, blk, hd)).astype(BF16)
        km_s[:, j:j + 1, :] = jnp.mean(kr, axis=1, keepdims=True)
        vb_s[:, rows, :hd] = _per_head(v_ref[rows, :], hd).astype(BF16)
        vb_s[:, rows, hd:] = jnp.ones((gh, blk, hd), BF16)

    for qb in range(n_blocks):
        rows = slice(qb * blk, (qb + 1) * blk)
        qr = _per_head(rope(q_ref[rows, :], rows), hd)
        qb16 = (qr * (scale * LOG2_E)).astype(BF16)
        bias = jnp.zeros((gh, blk, hd), BF16)
        if qb > 0:
            gate = jnp.where(blk_id < qb, _bdot_nt_f32(km_s[...], qr), NEG_INF)
            picked = []
            for j in range(qb):
                gj = gate[:, j:j + 1, :]
                beats = jnp.where(blk_id < j, jnp.where(gate >= gj, 1.0, 0.0), jnp.where(gate > gj, 1.0, 0.0))
                rank = jnp.sum(beats, axis=1, keepdims=True)
                picked.append(jnp.where(rank < MOBA_TOPK, 1.0, 0.0))
            picked.append(jnp.zeros((gh, n_blocks - qb, blk), F32))
            picked = jnp.concatenate(picked, axis=1)
            keep = jnp.stack([_dot_tn(picked[g], to_lanes) for g in range(gh)])
            bias = jnp.where(lane < qb, (1.0 - keep) * NEG_INF, 0.0).astype(BF16)
        width = (qb + 1) * blk
        logits = _bdot_nt(jnp.concatenate([qb16, bias], axis=2), kr_s[:, :width, :])
        if qb > 0:
            s_s[:, :, :qb * blk] = logits[:, :, :qb * blk]
        s_s[:, :, rows] = jnp.where(causal, logits[:, :, qb * blk:], NEG_INF)
        top = s_s[:, :, :LANES]
        for c0 in range(LANES, width, LANES):
            top = jnp.maximum(top, s_s[:, :, c0:c0 + LANES])
        p = jnp.exp2(s_s[:, :, :width] - jnp.max(top, axis=2, keepdims=True))
        pv = _bdot(p, vb_s[:, :width, :])
        o_ref[rows, :] = _merge_heads(pv[:, :, :hd] / pv[:, :, hd:]).astype(o_ref.dtype)


def _moba_mixer(proj, cos, sin):
    bsz, s, _ = proj.shape
    d, h, gh = HEAD_DIM, MOBA_HEADS, MOBA_GROUP
    groups = h // gh
    seq = lambda off: pl.BlockSpec((None, s, gh * d), lambda b, i: (b, 0, off + i))
    tab = pl.BlockSpec((None, s, LANES), lambda b, i: (b, 0, 0))
    return pl.pallas_call(
        functools.partial(_moba_kernel, n_blocks=s // MOBA_BLOCK),
        grid=(bsz, groups),
        in_specs=[seq(0), seq(groups), seq(2 * groups), tab, tab],
        out_specs=pl.BlockSpec((None, s, gh * d), lambda b, i: (b, 0, i)),
        out_shape=jax.ShapeDtypeStruct((bsz, s, h * d), BF16),
        scratch_shapes=[pltpu.VMEM((gh, s, 2 * d), BF16),
                        pltpu.VMEM((gh, s, 2 * d), BF16),
                        pltpu.VMEM((gh, s // MOBA_BLOCK, d), F32),
                        pltpu.VMEM((gh, MOBA_BLOCK, s), F32)],
        compiler_params=_params("parallel", "parallel"),
        name="moba",
    )(proj, proj, proj, cos, sin)


def _pad_cols(w, multiple=LANES):
    n = w.shape[1]
    return jnp.pad(w, ((0, 0), (0, -n % multiple)))


def kernel(x, positions, gla_w_in, gla_w_gk, gla_b_gk, gla_norm_g, gla_w_out, moba_w_in, moba_w_out,
           gdn_w_in, gdn_conv_w, gdn_a_log, gdn_dt_bias, gdn_norm_g, gdn_w_out, hgrn_lower_bounds,
           hgrn_w_in, hgrn_norm_g, hgrn_w_out, ffn_w_gu, ffn_w_down, ln_g, ln_b):
    bsz, s, d = x.shape
    assert (s % MOBA_BLOCK, (bsz * s) % TOKEN_TILE, d) == (0, 0, D_MODEL)
    t = bsz * s
    xf = x.reshape(t, d)
    cos = sin = None
    for i in range(DEPTH):
        kind, j = i % N_MIXERS, i // N_MIXERS
        tokens = lambda a: a.reshape(bsz, s, -1)
        if kind == 0:
            n_main = sum(GLA_COLS[:4])
            w = gla_w_in[j]
            low_w = _pad_cols(w[:, n_main:])
            proj, low = _proj(xf, [(w, 0, n_main, BF16), (low_w, 0, low_w.shape[1], F32)])
            o = _gla_mixer(tokens(proj), tokens(low), gla_w_gk[j], gla_b_gk[j], gla_norm_g[j])
            w_out = gla_w_out[j]
        elif kind == 1:
            if cos is None:
                cos, sin = _rope_tables(positions)
            proj, = _proj(xf, [(moba_w_in[j], 0, moba_w_in.shape[2], BF16)])
            o = _moba_mixer(tokens(proj), cos, sin)
            w_out = moba_w_out[j]
        elif kind == 2:
            n_main = sum(GDN_COLS[:2])
            w = gdn_w_in[j]
            ab_w = _pad_cols(w[:, n_main:])
            proj, ab = _proj(xf, [(w, 0, n_main, BF16), (ab_w, 0, ab_w.shape[1], F32)])
            o = _gdn_mixer(tokens(proj), tokens(ab), gdn_conv_w[j], gdn_a_log[j], gdn_dt_bias[j], gdn_norm_g[j])
            w_out = gdn_w_out[j]
        else:
            nq, nf = HGRN_COLS[0], HGRN_COLS[1]
            w = hgrn_w_in[j]
            q, f_logit, ig = _proj(xf, [(w, 0, nq, BF16), (w, nq, nf, F32), (w, nq + nf, sum(HGRN_COLS[2:]), BF16)])
            o = _hgrn_mixer(tokens(q), tokens(f_logit), tokens(ig), hgrn_lower_bounds, hgrn_norm_g[j], i)
            w_out = hgrn_w_out[j]
        xf = _outproj_ln(o.reshape(t, d), w_out, xf, ln_g[i, 0], ln_b[i, 0])
        xf = _ffn(xf, ffn_w_gu[i], ffn_w_down[i], ln_g[i, 1], ln_b[i, 1])
    return xf.reshape(bsz, s, d)
```

```python
import functools

import numpy as np
import jax
import jax.numpy as jnp
from jax import lax
from jax.experimental import pallas as pl
from jax.experimental.pallas import tpu as pltpu

F32 = jnp.float32
BF16 = jnp.bfloat16
HIGHEST = lax.Precision.HIGHEST

D_MODEL = 1024
DEPTH = 4
N_MIXERS = 4
HEAD_DIM = 128
CHUNK = 64
SUPER = 256
MIX_GROUP = 4
MOBA_GROUP = 2
GLA_HEADS = 4
GLA_DK = 128
GLA_DV = 256
GLA_GATE_RANK = 16
GLA_GATE_NORMALIZER = 16.0
MOBA_HEADS = 8
MOBA_BLOCK = 256
MOBA_TOPK = 3
ROPE_THETA = 500000.0
ROPE_DIM = 32
NEG_INF = -1e30
LOG2_E = 1.4426950408889634
GDN_HEADS = 8
GDN_CONV = 4
GDN_CHUNK = 256
GDN_INV_BASE = 16
GDN_GROUP = 4
GLA_COLS = (512, 512, 1024, 1024, 16)
GDN_COLS = (3072, 1024, 8, 8)
HGRN_COLS = (1024, 1024, 1024, 1024)
HGRN_HEADS = 8
FFN_HIDDEN = 2816
DEEPNORM_ALPHA = (2.0 * DEPTH) ** 0.25

LANES = 128
SUBLANES = 8
BF16_ROWS = 16
VMEM_LIMIT = 56 * 1024 * 1024
TOKEN_TILE = 512
COL_CHUNK = 512
TAIL_TILE = 512
LN_SUBTILE = 256


def _dot_tn(a, b):
    return lax.dot_general(a.astype(BF16), b.astype(BF16), (((0,), (0,)), ((), ())),
                           preferred_element_type=F32)


def _bdot(a, b):
    return lax.dot_general(a.astype(BF16), b.astype(BF16), (((2,), (1,)), ((0,), (0,))),
                           preferred_element_type=F32)


def _bdot_nt(a, b):
    return lax.dot_general(a.astype(BF16), b.astype(BF16), (((2,), (2,)), ((0,), (0,))),
                           preferred_element_type=F32)


def _bdot_nt_f32(a, b):
    return lax.dot_general(a, b, (((2,), (2,)), ((0,), (0,))), precision=HIGHEST,
                           preferred_element_type=F32)


def _sigmoid(x):
    return 1.0 / (1.0 + jnp.exp(-x))


def _silu(x):
    return x * _sigmoid(x)


def _softplus(x):
    return jnp.maximum(x, 0.0) + jnp.log1p(jnp.exp(-jnp.abs(x)))


def _log_sigmoid(x):
    return -_softplus(-x)


def _col_chunks(n, width=COL_CHUNK):
    return tuple((c0, min(width, n - c0)) for c0 in range(0, n, width))


def _params(*semantics):
    return pltpu.CompilerParams(dimension_semantics=semantics, vmem_limit_bytes=VMEM_LIMIT)


def _resident(shape):
    zeros = (0,) * len(shape)
    return pl.BlockSpec(shape, lambda *_: zeros, pipeline_mode=pl.Buffered(1))


def _layer_norm(z, g, b):
    mu = jnp.mean(z, axis=-1, keepdims=True)
    zc = z - mu
    var = jnp.mean(zc * zc, axis=-1, keepdims=True)
    return zc * lax.rsqrt(var + 1e-5) * g + b


def _tri_masks(n):
    row = lax.broadcasted_iota(jnp.int32, (n, n), 0)
    col = lax.broadcasted_iota(jnp.int32, (n, n), 1)
    return row >= col, row > col, row == col


def _proj_kernel(x_ref, *refs):
    xb = x_ref[...].astype(BF16)
    n_out = len(refs) // 2
    for w_ref, o_ref in zip(refs[:n_out], refs[n_out:]):
        for c0, cw in _col_chunks(w_ref.shape[1]):
            o_ref[:, c0:c0 + cw] = jnp.dot(xb, w_ref[:, c0:c0 + cw].astype(BF16),
                                           preferred_element_type=F32).astype(o_ref.dtype)


def _layer_window(layer, k, n, col_block=0):
    return pl.BlockSpec((None, k, n), lambda *_: (layer, 0, col_block), pipeline_mode=pl.Buffered(1))


def _proj(x, windows):
    t, k = x.shape
    row = lambda i: (i, 0)
    w_specs = []
    for _, layer, c0, n, _ in windows:
        assert c0 % n == 0 and n % LANES == 0
        w_specs.append(_layer_window(layer, k, n, c0 // n))
    return pl.pallas_call(
        _proj_kernel,
        grid=(t // TOKEN_TILE,),
        in_specs=[pl.BlockSpec((TOKEN_TILE, k), row)] + w_specs,
        out_specs=[pl.BlockSpec((TOKEN_TILE, n), row) for _, _, _, n, _ in windows],
        out_shape=[jax.ShapeDtypeStruct((t, n), dt) for _, _, _, n, dt in windows],
        compiler_params=_params("parallel"),
        name="proj",
    )(x, *[w for w, _, _, _, _ in windows])


def _tail_kernel(o_ref, wo_ref, x_ref, g1_ref, b1_ref, wgu_ref, wd_ref, g2_ref, b2_ref, y_ref, h_ref, *, hidden):
    tile = y_ref.shape[0]
    wo = wo_ref[...].astype(BF16)
    for r0 in range(0, tile, LN_SUBTILE):
        rows = slice(r0, r0 + LN_SUBTILE)
        y = jnp.dot(o_ref[rows, :], wo, preferred_element_type=F32)
        y_ref[rows, :] = _layer_norm(DEEPNORM_ALPHA * x_ref[rows, :] + y, g1_ref[...], b1_ref[...])
    xb = y_ref[...].astype(BF16)
    for c0, cw in _col_chunks(hidden, 256):
        gate = jnp.dot(xb, wgu_ref[:, c0:c0 + cw].astype(BF16), preferred_element_type=F32)
        up = jnp.dot(xb, wgu_ref[:, hidden + c0:hidden + c0 + cw].astype(BF16), preferred_element_type=F32)
        h_ref[:, c0:c0 + cw] = (_silu(gate) * up).astype(BF16)
    for r0 in range(0, tile, LN_SUBTILE):
        rows = slice(r0, r0 + LN_SUBTILE)
        down = jnp.concatenate([jnp.dot(h_ref[rows, :], wd_ref[:, c0:c0 + cw].astype(BF16),
                                        preferred_element_type=F32) for c0, cw in _col_chunks(y_ref.shape[1])], axis=1)
        y_ref[rows, :] = _layer_norm(DEEPNORM_ALPHA * y_ref[rows, :] + down, g2_ref[...], b2_ref[...])


def _layer_tail(o, w_out, out_layer, x, w_gu, w_down, layer, ln_g, ln_b):
    t, k = o.shape
    d = x.shape[1]
    hidden = w_down.shape[1]
    row = lambda i: (i, 0)
    vec = lambda a: a.reshape(1, d)
    return pl.pallas_call(
        functools.partial(_tail_kernel, hidden=hidden),
        grid=(t // TAIL_TILE,),
        in_specs=[pl.BlockSpec((TAIL_TILE, k), row), _layer_window(out_layer, k, d),
                  pl.BlockSpec((TAIL_TILE, d), row), _resident((1, d)), _resident((1, d)),
                  _layer_window(layer, d, 2 * hidden), _layer_window(layer, hidden, d),
                  _resident((1, d)), _resident((1, d))],
        out_specs=pl.BlockSpec((TAIL_TILE, d), row),
        out_shape=jax.ShapeDtypeStruct((t, d), F32),
        scratch_shapes=[pltpu.VMEM((TAIL_TILE, hidden), BF16)],
        compiler_params=_params("parallel"),
        name="layer_tail",
    )(o, w_out, x, vec(ln_g[layer, 0]), vec(ln_b[layer, 0]), w_gu, w_down, vec(ln_g[layer, 1]), vec(ln_b[layer, 1]))


def _split3(x):
    hi = x.astype(BF16)
    rest = x - hi.astype(F32)
    mid = rest.astype(BF16)
    return hi, mid, (rest - mid.astype(F32)).astype(BF16)


def _per_head(x, width):
    return jnp.stack([x[:, j * width:(j + 1) * width] for j in range(x.shape[1] // width)])


def _merge_heads(x):
    return jnp.concatenate([x[j] for j in range(x.shape[0])], axis=1)


def _gla_block(q, k, v, log_g, st):
    g, rows, dk = q.shape
    n = rows // CHUNK
    row = lax.broadcasted_iota(jnp.int32, (rows, rows), 0)
    col = lax.broadcasted_iota(jnp.int32, (rows, rows), 1)
    chunk_start = jnp.bitwise_and(row, -CHUNK)
    causal = jnp.logical_and(col <= row, col >= chunk_start)
    parts = [p for j in range(g) for p in _split3(log_g[j])]
    acc = jnp.dot(jnp.where(causal, 1.0, 0.0).astype(BF16), jnp.concatenate(parts, axis=1),
                  preferred_element_type=F32)
    cum = jnp.stack([acc[:, 3 * j * dk:(3 * j + 1) * dk] + acc[:, (3 * j + 1) * dk:(3 * j + 2) * dk]
                     + acc[:, (3 * j + 2) * dk:(3 * j + 3) * dk] for j in range(g)])
    lasts = [cum[:, (i + 1) * CHUNK - 1:(i + 1) * CHUNK, :] for i in range(n)]
    total = jnp.concatenate([jnp.broadcast_to(last, (g, CHUNK, dk)) for last in lasts], axis=1)
    q_dec = (q * jnp.exp(cum)).astype(BF16)
    k_inv = k * jnp.exp(-cum)
    k_end = (k * jnp.exp(total - cum)).astype(BF16)
    vb = v.astype(BF16)
    scores = jnp.where(causal, _bdot_nt(q_dec, k_inv), 0.0)
    o_intra = _bdot(scores, vb)
    o_inter = []
    for i in range(n):
        rs = slice(i * CHUNK, (i + 1) * CHUNK)
        o_inter.append(_bdot_nt(q_dec[:, rs, :], st))
        update = jnp.stack([_dot_tn(vb[j, rs, :], k_end[j, rs, :]) for j in range(g)])
        st = st * jnp.exp(lasts[i]) + update
    return o_intra + jnp.concatenate(o_inter, axis=1), st


def _gla_finish(o, norm_g, gate, o_ref, rows):
    o = o * lax.rsqrt(jnp.mean(o * o, axis=-1, keepdims=True) + 1e-6) * norm_g
    o_ref[rows, :] = (_merge_heads(o) * _silu(gate.astype(F32))).astype(o_ref.dtype)


def _gla_kernel(q_ref, k_ref, v_ref, g_ref, low_ref, wgk_ref, bgk_ref, ng_ref, o_ref, st_ref, *, n_blocks):
    st_ref[...] = jnp.zeros_like(st_ref)
    wgk = wgk_ref[...].astype(BF16)
    bgk = bgk_ref[...]
    norm_g = ng_ref[...]

    def body(c, carry):
        rows = pl.ds(pl.multiple_of(c * SUPER, SUPER), SUPER)
        logit = jnp.dot(low_ref[rows, :].astype(BF16), wgk, preferred_element_type=F32) + bgk
        log_g = _per_head(_log_sigmoid(logit) * (1.0 / GLA_GATE_NORMALIZER), GLA_DK)
        q = _per_head(q_ref[rows, :].astype(F32), GLA_DK) * (GLA_DK ** -0.5)
        o, st = _gla_block(q, _per_head(k_ref[rows, :].astype(F32), GLA_DK), _per_head(v_ref[rows, :], GLA_DV),
                           log_g, st_ref[...])
        st_ref[...] = st
        _gla_finish(o, norm_g, g_ref[rows, :], o_ref, rows)
        return carry

    lax.fori_loop(0, n_blocks, body, 0)


def _gla_mixer(proj, low, w_gk, b_gk, norm_g):
    bsz, s, _ = proj.shape
    dk, dv, h, gh = GLA_DK, GLA_DV, GLA_HEADS, MIX_GROUP
    groups = h // gh
    wgk_pad = jnp.zeros((LANES, h * dk), F32).at[:GLA_GATE_RANK].set(w_gk)
    seq = lambda width, off: pl.BlockSpec((None, s, gh * width), lambda b, i: (b, 0, off + i))
    return pl.pallas_call(
        functools.partial(_gla_kernel, n_blocks=s // SUPER),
        grid=(bsz, groups),
        in_specs=[seq(dk, 0), seq(dk, groups), seq(dv, (2 * h * dk) // (gh * dv)),
                  seq(dv, (2 * h * dk) // (gh * dv) + groups),
                  pl.BlockSpec((None, s, LANES), lambda b, i: (b, 0, 0)),
                  pl.BlockSpec((LANES, gh * dk), lambda b, i: (0, i)),
                  pl.BlockSpec((1, gh * dk), lambda b, i: (0, i)),
                  pl.BlockSpec((1, dv), lambda b, i: (0, 0))],
        out_specs=pl.BlockSpec((None, s, gh * dv), lambda b, i: (b, 0, i)),
        out_shape=jax.ShapeDtypeStruct((bsz, s, h * dv), BF16),
        scratch_shapes=[pltpu.VMEM((gh, dv, dk), F32)],
        compiler_params=_params("parallel", "parallel"),
        name="gla",
    )(proj, proj, proj, proj, low, wgk_pad, b_gk.reshape(1, -1), norm_g.reshape(1, -1))


def _hgrn_kernel(q_ref, f_ref, i_ref, g_ref, lb_ref, ng_ref, o_ref, st_ref, *, n_blocks, layer):
    st_ref[...] = jnp.zeros_like(st_ref)
    norm_g = ng_ref[...]
    raw = lb_ref[...]
    e = jnp.exp(raw - jnp.max(raw, axis=0, keepdims=True))
    soft = e / jnp.sum(e, axis=0, keepdims=True)
    lb = jnp.sum(soft[:layer + 1, :], axis=0, keepdims=True) - soft[0:1, :]

    def body(c, carry):
        rows = pl.ds(pl.multiple_of(c * SUPER, SUPER), SUPER)
        f = f_ref[rows, :]
        forget = lb + (1.0 - lb) * _sigmoid(f)
        k = (1.0 - lb) * _sigmoid(-f)
        o, st = _gla_block(_per_head(q_ref[rows, :].astype(F32), HEAD_DIM), _per_head(k, HEAD_DIM),
                           _per_head(i_ref[rows, :], HEAD_DIM), _per_head(jnp.log(forget), HEAD_DIM), st_ref[...])
        st_ref[...] = st
        _gla_finish(o, norm_g, g_ref[rows, :], o_ref, rows)
        return carry

    lax.fori_loop(0, n_blocks, body, 0)


def _hgrn_mixer(q, f_logit, ig, lower_bounds, norm_g, layer):
    bsz, s, _ = q.shape
    d, h, gh = HEAD_DIM, HGRN_HEADS, MIX_GROUP
    groups = h // gh
    seq = lambda off: pl.BlockSpec((None, s, gh * d), lambda b, i: (b, 0, off + i))
    return pl.pallas_call(
        functools.partial(_hgrn_kernel, n_blocks=s // SUPER, layer=layer),
        grid=(bsz, groups),
        in_specs=[seq(0), seq(0), seq(0), seq(groups),
                  pl.BlockSpec((DEPTH, gh * d), lambda b, i: (0, i)),
                  pl.BlockSpec((1, d), lambda b, i: (0, 0))],
        out_specs=pl.BlockSpec((None, s, gh * d), lambda b, i: (b, 0, i)),
        out_shape=jax.ShapeDtypeStruct((bsz, s, h * d), BF16),
        scratch_shapes=[pltpu.VMEM((gh, d, d), F32)],
        compiler_params=_params("parallel", "parallel"),
        name="hgrn2",
    )(q, f_logit, ig, ig, lower_bounds, norm_g.reshape(1, -1))


def _gdn_kernel(q_ref, k_ref, v_ref, z_ref, ab_ref, cwq_ref, cwk_ref, cwv_ref, alog_ref, dtb_ref, ng_ref,
                o_ref, st_ref, *, n_chunks):
    cs, gh, hd = GDN_CHUNK, GDN_GROUP, HEAD_DIM
    head0 = pl.program_id(1) * gh
    row = lax.broadcasted_iota(jnp.int32, (cs, cs), 0)
    col = lax.broadcasted_iota(jnp.int32, (cs, cs), 1)
    incl, strict, eye = row >= col, row > col, row == col
    tril_b = jnp.where(incl, 1.0, 0.0).astype(BF16)
    eye_f = jnp.where(eye, 1.0, 0.0)
    block_bits = jnp.bitwise_xor(row, col)
    lane = lax.broadcasted_iota(jnp.int32, (1, LANES), 1)
    norm_g = ng_ref[...]
    neg_a = -jnp.exp(alog_ref[...])
    dt_bias = dtb_ref[...]
    st_ref[...] = jnp.zeros_like(st_ref)

    def conv_silu(ref, cw_ref, c):
        r0 = pl.multiple_of(c * cs, cs)
        hist = BF16_ROWS
        cur = ref[pl.ds(r0, cs), :].astype(F32)
        prev = ref[pl.ds(pl.multiple_of(jnp.maximum(r0 - hist, 0), hist), hist), :].astype(F32)
        prev = jnp.where(c > 0, prev, 0.0)
        ext = jnp.concatenate([prev, cur], axis=0)
        w = cw_ref[...]
        y = ext[hist:, :] * w[GDN_CONV - 1:GDN_CONV, :]
        for j in range(GDN_CONV - 1):
            lo = hist - (GDN_CONV - 1) + j
            y = y + ext[lo:lo + cs, :] * w[j:j + 1, :]
        return _silu(y)

    def per_head(x):
        return jnp.stack([x[:, j * hd:(j + 1) * hd] for j in range(gh)])

    def pick(x, first):
        return jnp.stack([jnp.sum(jnp.where(lane == first + j, x, 0.0), axis=1, keepdims=True)
                          for j in range(gh)])

    def body(c, carry):
        rows = pl.ds(pl.multiple_of(c * cs, cs), cs)
        q = per_head(conv_silu(q_ref, cwq_ref, c))
        k = per_head(conv_silu(k_ref, cwk_ref, c))
        v = per_head(conv_silu(v_ref, cwv_ref, c))
        q = q * lax.rsqrt(jnp.sum(q * q, axis=-1, keepdims=True) + 1e-6) * (hd ** -0.5)
        k = k * lax.rsqrt(jnp.sum(k * k, axis=-1, keepdims=True) + 1e-6)
        ab = ab_ref[rows, :]
        g = pick(neg_a * _softplus(ab + dt_bias), head0)
        beta = pick(_sigmoid(ab), head0 + GDN_HEADS)
        g_wide = jnp.concatenate([jnp.broadcast_to(g[j], (cs, hd)) for j in range(gh)], axis=1)
        cum = per_head(sum(jnp.dot(tril_b, part, preferred_element_type=F32) for part in _split3(g_wide)))
        cum_wide = jnp.concatenate([cum] * (cs // hd), axis=2)
        cum_row = jnp.sum(jnp.where(eye, cum_wide, 0.0), axis=1, keepdims=True)
        decay = jnp.where(incl, jnp.exp(jnp.where(incl, cum_wide - cum_row, 0.0)), 0.0)
        k_beta = k * beta
        lower = jnp.where(strict, _bdot_nt(k_beta, k) * decay, 0.0)
        base_bits = int(np.log2(GDN_INV_BASE))
        power = -jnp.where(jnp.right_shift(block_bits, base_bits) == 0, lower, 0.0)
        inv = eye_f + power
        for _ in range(base_bits - 1):
            power = _bdot(power, power)
            inv = inv + _bdot(inv, power)
        for bits in range(base_bits, int(np.log2(cs))):
            cross = jnp.where(jnp.right_shift(block_bits, bits) == 1, lower, 0.0)
            inv = inv - _bdot(_bdot(inv, cross), inv)
        e_cum = jnp.exp(cum)
        sol = _bdot(inv, jnp.concatenate([v * beta, k_beta * e_cum], axis=2))
        u, w = sol[:, :, :hd], sol[:, :, hd:]
        qk = _bdot_nt(q, k) * decay
        last = cum[:, cs - 1:cs, :]
        k_end = k * jnp.exp(last - cum)
        st = st_ref[...]
        t = _bdot(jnp.concatenate([w, q * e_cum], axis=1), st)
        v_new = u - t[:, :cs, :]
        o = t[:, cs:, :] + _bdot(qk, v_new)
        st_ref[...] = st * jnp.exp(last) + jnp.stack([_dot_tn(k_end[j], v_new[j]) for j in range(gh)])
        o = o * lax.rsqrt(jnp.mean(o * o, axis=-1, keepdims=True) + 1e-6) * norm_g
        o_ref[rows, :] = (_merge_heads(o) * _silu(z_ref[rows, :].astype(F32))).astype(o_ref.dtype)
        return carry

    lax.fori_loop(0, n_chunks, body, 0)


def _gdn_mixer(proj, ab, conv_w, a_log, dt_bias, norm_g):
    bsz, s, _ = proj.shape
    d, h = HEAD_DIM, GDN_HEADS
    n_chunks = s // GDN_CHUNK
    gh = GDN_GROUP
    groups = h // gh
    seq = lambda off: pl.BlockSpec((None, s, gh * d), lambda b, i: (b, 0, off + i))
    cw = lambda off: pl.BlockSpec((GDN_CONV, gh * d), lambda b, i: (0, off + i))
    vec = pl.BlockSpec((1, LANES), lambda b, i: (0, 0))
    pad = lambda a: jnp.zeros((1, LANES), F32).at[0, :h].set(a)
    return pl.pallas_call(
        functools.partial(_gdn_kernel, n_chunks=n_chunks),
        grid=(bsz, groups),
        in_specs=[seq(0), seq(groups), seq(2 * groups), seq(3 * groups),
                  pl.BlockSpec((None, s, LANES), lambda b, i: (b, 0, 0)),
                  cw(0), cw(groups), cw(2 * groups), vec, vec, vec],
        out_specs=pl.BlockSpec((None, s, gh * d), lambda b, i: (b, 0, i)),
        out_shape=jax.ShapeDtypeStruct((bsz, s, h * d), BF16),
        scratch_shapes=[pltpu.VMEM((gh, d, d), F32)],
        compiler_params=_params("parallel", "parallel"),
        name="gdn",
    )(proj, proj, proj, proj, ab, conv_w, conv_w, conv_w, pad(a_log), pad(dt_bias),
      norm_g.reshape(1, -1))


def _rope_table_kernel(pos_ref, invf_ref, cos_ref, sin_ref):
    half = ROPE_DIM // 2
    rest = LANES - ROPE_DIM
    for c0 in range(0, pos_ref.shape[1], LANES):
        ang = invf_ref[...] * pos_ref[:, c0:c0 + LANES].astype(F32)
        c, s = jnp.cos(ang), jnp.sin(ang)
        cos_t = jnp.concatenate([c, c, jnp.ones((rest, LANES), F32)], axis=0)
        sin_t = jnp.concatenate([-s, s, jnp.zeros((rest, LANES), F32)], axis=0)
        cos_ref[c0:c0 + LANES, :] = cos_t.T
        sin_ref[c0:c0 + LANES, :] = sin_t.T


def _rope_tables(positions):
    bsz, s = positions.shape
    half = ROPE_DIM // 2
    inv_freq = (np.float32(ROPE_THETA) ** (-np.arange(0, ROPE_DIM, 2, dtype=np.float32) / ROPE_DIM)).astype(np.float32)
    invf = np.ascontiguousarray(np.broadcast_to(inv_freq[:, None], (half, LANES)))
    out = jax.ShapeDtypeStruct((bsz, s, LANES), F32)
    return pl.pallas_call(
        _rope_table_kernel,
        grid=(bsz,),
        in_specs=[pl.BlockSpec((None, 1, s), lambda b: (b, 0, 0)),
                  pl.BlockSpec((half, LANES), lambda b: (0, 0))],
        out_specs=[pl.BlockSpec((None, s, LANES), lambda b: (b, 0, 0))] * 2,
        out_shape=[out, out],
        compiler_params=_params("parallel"),
        name="rope_table",
    )(positions.reshape(bsz, 1, s), jnp.asarray(invf))


def _moba_kernel(q_ref, k_ref, v_ref, cos_ref, sin_ref, o_ref, kr_s, vb_s, km_s, s_s, *, n_blocks):
    blk, hd, gh = MOBA_BLOCK, HEAD_DIM, MOBA_GROUP
    half = ROPE_DIM // 2
    scale = hd ** -0.5
    wide = gh * hd
    causal, _, _ = _tri_masks(blk)
    head_lane = jnp.bitwise_and(lax.broadcasted_iota(jnp.int32, (1, wide), 1), hd - 1)
    lane = lax.broadcasted_iota(jnp.int32, (1, hd), 1)
    blk_id = lax.broadcasted_iota(jnp.int32, (1, n_blocks, 1), 1)
    to_lanes = jnp.where(lax.broadcasted_iota(jnp.int32, (n_blocks, hd), 0)
                         == lax.broadcasted_iota(jnp.int32, (n_blocks, hd), 1), 1.0, 0.0).astype(BF16)

    def rope(x, rows):
        x = x.astype(F32)
        partner = jnp.where(head_lane < half, pltpu.roll(x, wide - half, 1), pltpu.roll(x, half, 1))
        cos = jnp.concatenate([cos_ref[rows, :]] * gh, axis=1)
        sin = jnp.concatenate([sin_ref[rows, :]] * gh, axis=1)
        return x * cos + partner * sin

    for j in range(n_blocks):
        rows = slice(j * blk, (j + 1) * blk)
        kr = _per_head(rope(k_ref[rows, :], rows), hd)
        kr_s[:, rows, :hd] = kr.astype(BF16)
        kr_s[:, rows, hd:] = jnp.broadcast_to(jnp.where(lane == j, 1.0, 0.0), (gh, blk, hd)).astype(BF16)
        km_s[:, j:j + 1, :] = jnp.mean(kr, axis=1, keepdims=True)
        vb_s[:, rows, :hd] = _per_head(v_ref[rows, :], hd).astype(BF16)
        vb_s[:, rows, hd:] = jnp.ones((gh, blk, hd), BF16)

    for qb in range(n_blocks):
        rows = slice(qb * blk, (qb + 1) * blk)
        qr = _per_head(rope(q_ref[rows, :], rows), hd)
        qb16 = (qr * (scale * LOG2_E)).astype(BF16)
        bias = jnp.zeros((gh, blk, hd), BF16)
        if qb > 0:
            gate = jnp.where(blk_id < qb, _bdot_nt_f32(km_s[...], qr), NEG_INF)
            picked = []
            for j in range(qb):
                gj = gate[:, j:j + 1, :]
                beats = jnp.where(blk_id < j, jnp.where(gate >= gj, 1.0, 0.0), jnp.where(gate > gj, 1.0, 0.0))
                rank = jnp.sum(beats, axis=1, keepdims=True)
                picked.append(jnp.where(rank < MOBA_TOPK, 1.0, 0.0))
            picked.append(jnp.zeros((gh, n_blocks - qb, blk), F32))
            picked = jnp.concatenate(picked, axis=1)
            keep = jnp.stack([_dot_tn(picked[g], to_lanes) for g in range(gh)])
            bias = jnp.where(lane < qb, (1.0 - keep) * NEG_INF, 0.0).astype(BF16)
        width = (qb + 1) * blk
        logits = _bdot_nt(jnp.concatenate([qb16, bias], axis=2), kr_s[:, :width, :])
        if qb > 0:
            s_s[:, :, :qb * blk] = logits[:, :, :qb * blk]
        s_s[:, :, rows] = jnp.where(causal, logits[:, :, qb * blk:], NEG_INF)
        top = s_s[:, :, :LANES]
        for c0 in range(LANES, width, LANES):
            top = jnp.maximum(top, s_s[:, :, c0:c0 + LANES])
        p = jnp.exp2(s_s[:, :, :width] - jnp.max(top, axis=2, keepdims=True))
        pv = _bdot(p, vb_s[:, :width, :])
        o_ref[rows, :] = _merge_heads(pv[:, :, :hd] / pv[:, :, hd:]).astype(o_ref.dtype)


def _moba_mixer(proj, cos, sin):
    bsz, s, _ = proj.shape
    d, h, gh = HEAD_DIM, MOBA_HEADS, MOBA_GROUP
    groups = h // gh
    seq = lambda off: pl.BlockSpec((None, s, gh * d), lambda b, i: (b, 0, off + i))
    tab = pl.BlockSpec((None, s, LANES), lambda b, i: (b, 0, 0))
    return pl.pallas_call(
        functools.partial(_moba_kernel, n_blocks=s // MOBA_BLOCK),
        grid=(bsz, groups),
        in_specs=[seq(0), seq(groups), seq(2 * groups), tab, tab],
        out_specs=pl.BlockSpec((None, s, gh * d), lambda b, i: (b, 0, i)),
        out_shape=jax.ShapeDtypeStruct((bsz, s, h * d), BF16),
        scratch_shapes=[pltpu.VMEM((gh, s, 2 * d), BF16),
                        pltpu.VMEM((gh, s, 2 * d), BF16),
                        pltpu.VMEM((gh, s // MOBA_BLOCK, d), F32),
                        pltpu.VMEM((gh, MOBA_BLOCK, s), F32)],
        compiler_params=_params("parallel", "parallel"),
        name="moba",
    )(proj, proj, proj, cos, sin)


def _pad_cols(w, multiple=LANES):
    n = w.shape[1]
    return jnp.pad(w, ((0, 0), (0, -n % multiple)))


def kernel(x, positions, gla_w_in, gla_w_gk, gla_b_gk, gla_norm_g, gla_w_out, moba_w_in, moba_w_out,
           gdn_w_in, gdn_conv_w, gdn_a_log, gdn_dt_bias, gdn_norm_g, gdn_w_out, hgrn_lower_bounds,
           hgrn_w_in, hgrn_norm_g, hgrn_w_out, ffn_w_gu, ffn_w_down, ln_g, ln_b):
    bsz, s, d = x.shape
    assert (s % MOBA_BLOCK, (bsz * s) % TOKEN_TILE, d) == (0, 0, D_MODEL)
    t = bsz * s
    xf = x.reshape(t, d)
    cos = sin = None
    for i in range(DEPTH):
        kind, j = i % N_MIXERS, i // N_MIXERS
        tokens = lambda a: a.reshape(bsz, s, -1)
        if kind == 0:
            n_main = sum(GLA_COLS[:4])
            low_w = _pad_cols(gla_w_in[j, :, n_main:])[None]
            proj, low = _proj(xf, [(gla_w_in, j, 0, n_main, BF16), (low_w, 0, 0, low_w.shape[2], F32)])
            o = _gla_mixer(tokens(proj), tokens(low), gla_w_gk[j], gla_b_gk[j], gla_norm_g[j])
            w_out = gla_w_out
        elif kind == 1:
            if cos is None:
                cos, sin = _rope_tables(positions)
            proj, = _proj(xf, [(moba_w_in, j, 0, moba_w_in.shape[2], BF16)])
            o = _moba_mixer(tokens(proj), cos, sin)
            w_out = moba_w_out
        elif kind == 2:
            n_main = sum(GDN_COLS[:2])
            ab_w = _pad_cols(gdn_w_in[j, :, n_main:])[None]
            proj, ab = _proj(xf, [(gdn_w_in, j, 0, n_main, BF16), (ab_w, 0, 0, ab_w.shape[2], F32)])
            o = _gdn_mixer(tokens(proj), tokens(ab), gdn_conv_w[j], gdn_a_log[j], gdn_dt_bias[j], gdn_norm_g[j])
            w_out = gdn_w_out
        else:
            nq, nf = HGRN_COLS[0], HGRN_COLS[1]
            q, f_logit, ig = _proj(xf, [(hgrn_w_in, j, 0, nq, BF16), (hgrn_w_in, j, nq, nf, F32),
                                        (hgrn_w_in, j, nq + nf, sum(HGRN_COLS[2:]), BF16)])
            o = _hgrn_mixer(tokens(q), tokens(f_logit), tokens(ig), hgrn_lower_bounds, hgrn_norm_g[j], i)
            w_out = hgrn_w_out
        xf = _layer_tail(o.reshape(t, d), w_out, j, xf, ffn_w_gu, ffn_w_down, i, ln_g, ln_b)
    return xf.reshape(bsz, s, d)
```

```python
import functools

import numpy as np
import jax
import jax.numpy as jnp
from jax import lax
from jax.experimental import pallas as pl
from jax.experimental.pallas import tpu as pltpu

F32 = jnp.float32
BF16 = jnp.bfloat16
HIGHEST = lax.Precision.HIGHEST

D_MODEL = 1024
DEPTH = 4
N_MIXERS = 4
HEAD_DIM = 128
CHUNK = 64
SUPER = 256
MIX_GROUP = 4
MOBA_GROUP = 4
GLA_HEADS = 4
GLA_DK = 128
GLA_DV = 256
GLA_GATE_RANK = 16
GLA_GATE_NORMALIZER = 16.0
MOBA_HEADS = 8
MOBA_BLOCK = 256
MOBA_TOPK = 3
ROPE_THETA = 500000.0
ROPE_DIM = 32
NEG_INF = -1e30
LOG2_E = 1.4426950408889634
GDN_HEADS = 8
GDN_CONV = 4
GDN_ROWS = 256
GDN_CHUNK = 128
GDN_INV_BASE = 16
GDN_GROUP = 4
GLA_COLS = (512, 512, 1024, 1024, 16)
GDN_COLS = (3072, 1024, 8, 8)
HGRN_COLS = (1024, 1024, 1024, 1024)
HGRN_HEADS = 8
FFN_HIDDEN = 2816
DEEPNORM_ALPHA = (2.0 * DEPTH) ** 0.25

LANES = 128
SUBLANES = 8
BF16_ROWS = 16
VMEM_LIMIT = 56 * 1024 * 1024
TOKEN_TILE = 512
COL_CHUNK = 512
TAIL_TILE = 512
LN_SUBTILE = 256


def _dot_tn(a, b):
    return lax.dot_general(a.astype(BF16), b.astype(BF16), (((0,), (0,)), ((), ())),
                           preferred_element_type=F32)


def _bdot(a, b):
    return lax.dot_general(a.astype(BF16), b.astype(BF16), (((2,), (1,)), ((0,), (0,))),
                           preferred_element_type=F32)


def _bdot_nt(a, b):
    return lax.dot_general(a.astype(BF16), b.astype(BF16), (((2,), (2,)), ((0,), (0,))),
                           preferred_element_type=F32)


def _bdot_nt_f32(a, b):
    return lax.dot_general(a, b, (((2,), (2,)), ((0,), (0,))), precision=HIGHEST,
                           preferred_element_type=F32)


def _sigmoid(x):
    return 1.0 / (1.0 + jnp.exp(-x))


def _silu(x):
    return x * _sigmoid(x)


def _softplus(x):
    return jnp.maximum(x, 0.0) + jnp.log1p(jnp.exp(-jnp.abs(x)))


def _log_sigmoid(x):
    return -_softplus(-x)


def _col_chunks(n, width=COL_CHUNK):
    return tuple((c0, min(width, n - c0)) for c0 in range(0, n, width))


def _params(*semantics):
    return pltpu.CompilerParams(dimension_semantics=semantics, vmem_limit_bytes=VMEM_LIMIT)


def _resident(shape):
    zeros = (0,) * len(shape)
    return pl.BlockSpec(shape, lambda *_: zeros, pipeline_mode=pl.Buffered(1))


def _layer_norm(z, g, b):
    mu = jnp.mean(z, axis=-1, keepdims=True)
    zc = z - mu
    var = jnp.mean(zc * zc, axis=-1, keepdims=True)
    return zc * lax.rsqrt(var + 1e-5) * g + b


def _tri_masks(n):
    row = lax.broadcasted_iota(jnp.int32, (n, n), 0)
    col = lax.broadcasted_iota(jnp.int32, (n, n), 1)
    return row >= col, row > col, row == col


def _proj_kernel(x_ref, *refs):
    xb = x_ref[...].astype(BF16)
    n_out = len(refs) // 2
    for w_ref, o_ref in zip(refs[:n_out], refs[n_out:]):
        for c0, cw in _col_chunks(w_ref.shape[1]):
            o_ref[:, c0:c0 + cw] = jnp.dot(xb, w_ref[:, c0:c0 + cw].astype(BF16),
                                           preferred_element_type=F32).astype(o_ref.dtype)


def _layer_window(layer, k, n, col_block=0):
    return pl.BlockSpec((None, k, n), lambda *_: (layer, 0, col_block), pipeline_mode=pl.Buffered(1))


def _proj(x, windows):
    t, k = x.shape
    row = lambda i: (i, 0)
    w_specs = []
    for _, layer, c0, n, _ in windows:
        assert c0 % n == 0 and n % LANES == 0
        w_specs.append(_layer_window(layer, k, n, c0 // n))
    return pl.pallas_call(
        _proj_kernel,
        grid=(t // TOKEN_TILE,),
        in_specs=[pl.BlockSpec((TOKEN_TILE, k), row)] + w_specs,
        out_specs=[pl.BlockSpec((TOKEN_TILE, n), row) for _, _, _, n, _ in windows],
        out_shape=[jax.ShapeDtypeStruct((t, n), dt) for _, _, _, n, dt in windows],
        compiler_params=_params("parallel"),
        name="proj",
    )(x, *[w for w, _, _, _, _ in windows])


def _tail_kernel(o_ref, wo_ref, x_ref, g1_ref, b1_ref, wgu_ref, wd_ref, g2_ref, b2_ref, y_ref, h_ref, *, hidden):
    tile = y_ref.shape[0]
    wo = wo_ref[...].astype(BF16)
    for r0 in range(0, tile, LN_SUBTILE):
        rows = slice(r0, r0 + LN_SUBTILE)
        y = jnp.dot(o_ref[rows, :], wo, preferred_element_type=F32)
        y_ref[rows, :] = _layer_norm(DEEPNORM_ALPHA * x_ref[rows, :] + y, g1_ref[...], b1_ref[...])
    xb = y_ref[...].astype(BF16)
    for c0, cw in _col_chunks(hidden, 256):
        gate = jnp.dot(xb, wgu_ref[:, c0:c0 + cw].astype(BF16), preferred_element_type=F32)
        up = jnp.dot(xb, wgu_ref[:, hidden + c0:hidden + c0 + cw].astype(BF16), preferred_element_type=F32)
        h_ref[:, c0:c0 + cw] = (_silu(gate) * up).astype(BF16)
    for r0 in range(0, tile, LN_SUBTILE):
        rows = slice(r0, r0 + LN_SUBTILE)
        down = jnp.concatenate([jnp.dot(h_ref[rows, :], wd_ref[:, c0:c0 + cw].astype(BF16),
                                        preferred_element_type=F32) for c0, cw in _col_chunks(y_ref.shape[1])], axis=1)
        y_ref[rows, :] = _layer_norm(DEEPNORM_ALPHA * y_ref[rows, :] + down, g2_ref[...], b2_ref[...])


def _layer_tail(o, w_out, out_layer, x, w_gu, w_down, layer, ln_g, ln_b):
    t, k = o.shape
    d = x.shape[1]
    hidden = w_down.shape[1]
    row = lambda i: (i, 0)
    vec = lambda a: a.reshape(1, d)
    return pl.pallas_call(
        functools.partial(_tail_kernel, hidden=hidden),
        grid=(t // TAIL_TILE,),
        in_specs=[pl.BlockSpec((TAIL_TILE, k), row), _layer_window(out_layer, k, d),
                  pl.BlockSpec((TAIL_TILE, d), row), _resident((1, d)), _resident((1, d)),
                  _layer_window(layer, d, 2 * hidden), _layer_window(layer, hidden, d),
                  _resident((1, d)), _resident((1, d))],
        out_specs=pl.BlockSpec((TAIL_TILE, d), row),
        out_shape=jax.ShapeDtypeStruct((t, d), F32),
        scratch_shapes=[pltpu.VMEM((TAIL_TILE, hidden), BF16)],
        compiler_params=_params("parallel"),
        name="layer_tail",
    )(o, w_out, x, vec(ln_g[layer, 0]), vec(ln_b[layer, 0]), w_gu, w_down, vec(ln_g[layer, 1]), vec(ln_b[layer, 1]))


def _split3(x):
    hi = x.astype(BF16)
    rest = x - hi.astype(F32)
    mid = rest.astype(BF16)
    return hi, mid, (rest - mid.astype(F32)).astype(BF16)


def _per_head(x, width):
    return jnp.stack([x[:, j * width:(j + 1) * width] for j in range(x.shape[1] // width)])


def _merge_heads(x):
    return jnp.concatenate([x[j] for j in range(x.shape[0])], axis=1)


def _gla_block(q, k, v, log_g, st):
    g, rows, dk = q.shape
    n = rows // CHUNK
    row = lax.broadcasted_iota(jnp.int32, (rows, rows), 0)
    col = lax.broadcasted_iota(jnp.int32, (rows, rows), 1)
    chunk_start = jnp.bitwise_and(row, -CHUNK)
    causal = jnp.logical_and(col <= row, col >= chunk_start)
    parts = [p for j in range(g) for p in _split3(log_g[j])]
    acc = jnp.dot(jnp.where(causal, 1.0, 0.0).astype(BF16), jnp.concatenate(parts, axis=1),
                  preferred_element_type=F32)
    cum = jnp.stack([acc[:, 3 * j * dk:(3 * j + 1) * dk] + acc[:, (3 * j + 1) * dk:(3 * j + 2) * dk]
                     + acc[:, (3 * j + 2) * dk:(3 * j + 3) * dk] for j in range(g)])
    lasts = [cum[:, (i + 1) * CHUNK - 1:(i + 1) * CHUNK, :] for i in range(n)]
    total = jnp.concatenate([jnp.broadcast_to(last, (g, CHUNK, dk)) for last in lasts], axis=1)
    q_dec = (q * jnp.exp(cum)).astype(BF16)
    k_inv = k * jnp.exp(-cum)
    k_end = (k * jnp.exp(total - cum)).astype(BF16)
    vb = v.astype(BF16)
    scores = jnp.where(causal, _bdot_nt(q_dec, k_inv), 0.0)
    o_intra = _bdot(scores, vb)
    o_inter = []
    for i in range(n):
        rs = slice(i * CHUNK, (i + 1) * CHUNK)
        o_inter.append(_bdot_nt(q_dec[:, rs, :], st))
        update = jnp.stack([_dot_tn(vb[j, rs, :], k_end[j, rs, :]) for j in range(g)])
        st = st * jnp.exp(lasts[i]) + update
    return o_intra + jnp.concatenate(o_inter, axis=1), st


def _gla_finish(o, norm_g, gate, o_ref, rows):
    o = o * lax.rsqrt(jnp.mean(o * o, axis=-1, keepdims=True) + 1e-6) * norm_g
    o_ref[rows, :] = (_merge_heads(o) * _silu(gate.astype(F32))).astype(o_ref.dtype)


def _gla_kernel(q_ref, k_ref, v_ref, g_ref, low_ref, wgk_ref, bgk_ref, ng_ref, o_ref, st_ref, *, n_blocks):
    st_ref[...] = jnp.zeros_like(st_ref)
    wgk = wgk_ref[...].astype(BF16)
    bgk = bgk_ref[...]
    norm_g = ng_ref[...]

    def body(c, carry):
        rows = pl.ds(pl.multiple_of(c * SUPER, SUPER), SUPER)
        logit = jnp.dot(low_ref[rows, :].astype(BF16), wgk, preferred_element_type=F32) + bgk
        log_g = _per_head(_log_sigmoid(logit) * (1.0 / GLA_GATE_NORMALIZER), GLA_DK)
        q = _per_head(q_ref[rows, :].astype(F32), GLA_DK) * (GLA_DK ** -0.5)
        o, st = _gla_block(q, _per_head(k_ref[rows, :].astype(F32), GLA_DK), _per_head(v_ref[rows, :], GLA_DV),
                           log_g, st_ref[...])
        st_ref[...] = st
        _gla_finish(o, norm_g, g_ref[rows, :], o_ref, rows)
        return carry

    lax.fori_loop(0, n_blocks, body, 0)


def _gla_mixer(proj, low, w_gk, b_gk, norm_g):
    bsz, s, _ = proj.shape
    dk, dv, h, gh = GLA_DK, GLA_DV, GLA_HEADS, MIX_GROUP
    groups = h // gh
    wgk_pad = jnp.zeros((LANES, h * dk), F32).at[:GLA_GATE_RANK].set(w_gk)
    seq = lambda width, off: pl.BlockSpec((None, s, gh * width), lambda b, i: (b, 0, off + i))
    return pl.pallas_call(
        functools.partial(_gla_kernel, n_blocks=s // SUPER),
        grid=(bsz, groups),
        in_specs=[seq(dk, 0), seq(dk, groups), seq(dv, (2 * h * dk) // (gh * dv)),
                  seq(dv, (2 * h * dk) // (gh * dv) + groups),
                  pl.BlockSpec((None, s, LANES), lambda b, i: (b, 0, 0)),
                  pl.BlockSpec((LANES, gh * dk), lambda b, i: (0, i)),
                  pl.BlockSpec((1, gh * dk), lambda b, i: (0, i)),
                  pl.BlockSpec((1, dv), lambda b, i: (0, 0))],
        out_specs=pl.BlockSpec((None, s, gh * dv), lambda b, i: (b, 0, i)),
        out_shape=jax.ShapeDtypeStruct((bsz, s, h * dv), BF16),
        scratch_shapes=[pltpu.VMEM((gh, dv, dk), F32)],
        compiler_params=_params("parallel", "parallel"),
        name="gla",
    )(proj, proj, proj, proj, low, wgk_pad, b_gk.reshape(1, -1), norm_g.reshape(1, -1))


def _hgrn_kernel(q_ref, f_ref, i_ref, g_ref, lb_ref, ng_ref, o_ref, st_ref, *, n_blocks, layer):
    st_ref[...] = jnp.zeros_like(st_ref)
    norm_g = ng_ref[...]
    raw = lb_ref[...]
    e = jnp.exp(raw - jnp.max(raw, axis=0, keepdims=True))
    soft = e / jnp.sum(e, axis=0, keepdims=True)
    lb = jnp.sum(soft[:layer + 1, :], axis=0, keepdims=True) - soft[0:1, :]

    def body(c, carry):
        rows = pl.ds(pl.multiple_of(c * SUPER, SUPER), SUPER)
        f = f_ref[rows, :]
        forget = lb + (1.0 - lb) * _sigmoid(f)
        k = (1.0 - lb) * _sigmoid(-f)
        o, st = _gla_block(_per_head(q_ref[rows, :].astype(F32), HEAD_DIM), _per_head(k, HEAD_DIM),
                           _per_head(i_ref[rows, :], HEAD_DIM), _per_head(jnp.log(forget), HEAD_DIM), st_ref[...])
        st_ref[...] = st
        _gla_finish(o, norm_g, g_ref[rows, :], o_ref, rows)
        return carry

    lax.fori_loop(0, n_blocks, body, 0)


def _hgrn_mixer(q, f_logit, ig, lower_bounds, norm_g, layer):
    bsz, s, _ = q.shape
    d, h, gh = HEAD_DIM, HGRN_HEADS, MIX_GROUP
    groups = h // gh
    seq = lambda off: pl.BlockSpec((None, s, gh * d), lambda b, i: (b, 0, off + i))
    return pl.pallas_call(
        functools.partial(_hgrn_kernel, n_blocks=s // SUPER, layer=layer),
        grid=(bsz, groups),
        in_specs=[seq(0), seq(0), seq(0), seq(groups),
                  pl.BlockSpec((DEPTH, gh * d), lambda b, i: (0, i)),
                  pl.BlockSpec((1, d), lambda b, i: (0, 0))],
        out_specs=pl.BlockSpec((None, s, gh * d), lambda b, i: (b, 0, i)),
        out_shape=jax.ShapeDtypeStruct((bsz, s, h * d), BF16),
        scratch_shapes=[pltpu.VMEM((gh, d, d), F32)],
        compiler_params=_params("parallel", "parallel"),
        name="hgrn2",
    )(q, f_logit, ig, ig, lower_bounds, norm_g.reshape(1, -1))


def _gdn_kernel(q_ref, k_ref, v_ref, z_ref, ab_ref, cwq_ref, cwk_ref, cwv_ref, alog_ref, dtb_ref, ng_ref,
                o_ref, st_ref, *, n_chunks):
    rn, cs, gh, hd = GDN_ROWS, GDN_CHUNK, GDN_GROUP, HEAD_DIM
    n_sub = rn // cs
    head0 = pl.program_id(1) * gh
    row = lax.broadcasted_iota(jnp.int32, (cs, cs), 0)
    col = lax.broadcasted_iota(jnp.int32, (cs, cs), 1)
    incl, strict, eye = row >= col, row > col, row == col
    eye_f = jnp.where(eye, 1.0, 0.0)
    block_bits = jnp.bitwise_xor(row, col)
    big_row = lax.broadcasted_iota(jnp.int32, (rn, rn), 0)
    big_col = lax.broadcasted_iota(jnp.int32, (rn, rn), 1)
    tril_b = jnp.where(jnp.logical_and(big_col <= big_row, big_col >= jnp.bitwise_and(big_row, -cs)),
                       1.0, 0.0).astype(BF16)
    lane = lax.broadcasted_iota(jnp.int32, (1, LANES), 1)
    norm_g = ng_ref[...]
    neg_a = -jnp.exp(alog_ref[...])
    dt_bias = dtb_ref[...]
    st_ref[...] = jnp.zeros_like(st_ref)

    def conv_silu(ref, cw_ref, c):
        r0 = pl.multiple_of(c * rn, rn)
        hist = BF16_ROWS
        cur = ref[pl.ds(r0, rn), :].astype(F32)
        prev = ref[pl.ds(pl.multiple_of(jnp.maximum(r0 - hist, 0), hist), hist), :].astype(F32)
        prev = jnp.where(c > 0, prev, 0.0)
        ext = jnp.concatenate([prev, cur], axis=0)
        w = cw_ref[...]
        y = ext[hist:, :] * w[GDN_CONV - 1:GDN_CONV, :]
        for j in range(GDN_CONV - 1):
            lo = hist - (GDN_CONV - 1) + j
            y = y + ext[lo:lo + rn, :] * w[j:j + 1, :]
        return _silu(y)

    def per_chunk(x):
        width = x.shape[1] // gh
        return jnp.stack([x[s * cs:(s + 1) * cs, j * width:(j + 1) * width]
                          for j in range(gh) for s in range(n_sub)])

    def pick(x, first):
        return jnp.concatenate([jnp.broadcast_to(jnp.sum(jnp.where(lane == first + j, x, 0.0), axis=1,
                                                         keepdims=True), (rn, hd)) for j in range(gh)], axis=1)

    def of_sub(x, s):
        return jnp.stack([x[j * n_sub + s] for j in range(gh)])

    def body(c, carry):
        rows = pl.ds(pl.multiple_of(c * rn, rn), rn)
        q = per_chunk(conv_silu(q_ref, cwq_ref, c))
        k = per_chunk(conv_silu(k_ref, cwk_ref, c))
        v = per_chunk(conv_silu(v_ref, cwv_ref, c))
        q = q * lax.rsqrt(jnp.sum(q * q, axis=-1, keepdims=True) + 1e-6) * (hd ** -0.5)
        k = k * lax.rsqrt(jnp.sum(k * k, axis=-1, keepdims=True) + 1e-6)
        ab = ab_ref[rows, :]
        g_wide = pick(neg_a * _softplus(ab + dt_bias), head0)
        beta = per_chunk(pick(_sigmoid(ab), head0 + GDN_HEADS))
        cum = per_chunk(sum(jnp.dot(tril_b, part, preferred_element_type=F32) for part in _split3(g_wide)))
        cum_row = jnp.sum(jnp.where(eye, cum, 0.0), axis=1, keepdims=True)
        grow = jnp.exp(jnp.where(incl, cum - cum_row, 0.0))
        k_beta = k * beta
        lower = jnp.where(strict, _bdot_nt(k_beta, k) * grow, 0.0)
        base_bits = int(np.log2(GDN_INV_BASE))
        power = -jnp.where(jnp.right_shift(block_bits, base_bits) == 0, lower, 0.0)
        inv = eye_f + power
        for _ in range(base_bits - 1):
            power = _bdot(power, power)
            inv = inv + _bdot(inv, power)
        for bits in range(base_bits, int(np.log2(cs))):
            cross = jnp.where(jnp.right_shift(block_bits, bits) == 1, lower, 0.0)
            inv = inv - _bdot(_bdot(inv, cross), inv)
        e_cum = jnp.exp(cum)
        sol = _bdot(inv, jnp.concatenate([v * beta, k_beta * e_cum], axis=2))
        u, w_qd = sol[:, :, :hd], jnp.concatenate([sol[:, :, hd:], q * e_cum], axis=1)
        qk = jnp.where(incl, _bdot_nt(q, k) * grow, 0.0)
        last = cum[:, cs - 1:cs, :]
        k_end = k * jnp.exp(last - cum)
        carry_decay = jnp.exp(last)
        st = st_ref[...]
        outs = []
        for s in range(n_sub):
            t = _bdot(of_sub(w_qd, s), st)
            v_new = of_sub(u, s) - t[:, :cs, :]
            outs.append(t[:, cs:, :] + _bdot(of_sub(qk, s), v_new))
            ke = of_sub(k_end, s)
            st = st * of_sub(carry_decay, s) + jnp.stack([_dot_tn(ke[j], v_new[j]) for j in range(gh)])
        st_ref[...] = st
        o = jnp.concatenate(outs, axis=1)
        o = o * lax.rsqrt(jnp.mean(o * o, axis=-1, keepdims=True) + 1e-6) * norm_g
        o_ref[rows, :] = (_merge_heads(o) * _silu(z_ref[rows, :].astype(F32))).astype(o_ref.dtype)
        return carry

    lax.fori_loop(0, n_chunks, body, 0)


def _gdn_mixer(proj, ab, conv_w, a_log, dt_bias, norm_g):
    bsz, s, _ = proj.shape
    d, h = HEAD_DIM, GDN_HEADS
    assert GDN_CHUNK == HEAD_DIM and s % GDN_ROWS == 0
    n_chunks = s // GDN_ROWS
    gh = GDN_GROUP
    groups = h // gh
    seq = lambda off: pl.BlockSpec((None, s, gh * d), lambda b, i: (b, 0, off + i))
    cw = lambda off: pl.BlockSpec((GDN_CONV, gh * d), lambda b, i: (0, off + i))
    vec = pl.BlockSpec((1, LANES), lambda b, i: (0, 0))
    pad = lambda a: jnp.zeros((1, LANES), F32).at[0, :h].set(a)
    return pl.pallas_call(
        functools.partial(_gdn_kernel, n_chunks=n_chunks),
        grid=(bsz, groups),
        in_specs=[seq(0), seq(groups), seq(2 * groups), seq(3 * groups),
                  pl.BlockSpec((None, s, LANES), lambda b, i: (b, 0, 0)),
                  cw(0), cw(groups), cw(2 * groups), vec, vec, vec],
        out_specs=pl.BlockSpec((None, s, gh * d), lambda b, i: (b, 0, i)),
        out_shape=jax.ShapeDtypeStruct((bsz, s, h * d), BF16),
        scratch_shapes=[pltpu.VMEM((gh, d, d), F32)],
        compiler_params=_params("parallel", "parallel"),
        name="gdn",
    )(proj, proj, proj, proj, ab, conv_w, conv_w, conv_w, pad(a_log), pad(dt_bias),
      norm_g.reshape(1, -1))


def _rope_table_kernel(pos_ref, invf_ref, cos_ref, sin_ref):
    half = ROPE_DIM // 2
    rest = LANES - ROPE_DIM
    for c0 in range(0, pos_ref.shape[1], LANES):
        ang = invf_ref[...] * pos_ref[:, c0:c0 + LANES].astype(F32)
        c, s = jnp.cos(ang), jnp.sin(ang)
        cos_t = jnp.concatenate([c, c, jnp.ones((rest, LANES), F32)], axis=0)
        sin_t = jnp.concatenate([-s, s, jnp.zeros((rest, LANES), F32)], axis=0)
        cos_ref[c0:c0 + LANES, :] = cos_t.T
        sin_ref[c0:c0 + LANES, :] = sin_t.T


def _rope_tables(positions):
    bsz, s = positions.shape
    half = ROPE_DIM // 2
    inv_freq = (np.float32(ROPE_THETA) ** (-np.arange(0, ROPE_DIM, 2, dtype=np.float32) / ROPE_DIM)).astype(np.float32)
    invf = np.ascontiguousarray(np.broadcast_to(inv_freq[:, None], (half, LANES)))
    out = jax.ShapeDtypeStruct((bsz, s, LANES), F32)
    return pl.pallas_call(
        _rope_table_kernel,
        grid=(bsz,),
        in_specs=[pl.BlockSpec((None, 1, s), lambda b: (b, 0, 0)),
                  pl.BlockSpec((half, LANES), lambda b: (0, 0))],
        out_specs=[pl.BlockSpec((None, s, LANES), lambda b: (b, 0, 0))] * 2,
        out_shape=[out, out],
        compiler_params=_params("parallel"),
        name="rope_table",
    )(positions.reshape(bsz, 1, s), jnp.asarray(invf))


def _moba_kernel(q_ref, k_ref, v_ref, cos_ref, sin_ref, o_ref, kr_s, vb_s, km_s, s_s, *, n_blocks):
    blk, hd, gh = MOBA_BLOCK, HEAD_DIM, MOBA_GROUP
    half = ROPE_DIM // 2
    scale = hd ** -0.5
    wide = gh * hd
    causal, _, _ = _tri_masks(blk)
    head_lane = jnp.bitwise_and(lax.broadcasted_iota(jnp.int32, (1, wide), 1), hd - 1)
    lane = lax.broadcasted_iota(jnp.int32, (1, hd), 1)
    blk_id = lax.broadcasted_iota(jnp.int32, (1, n_blocks, 1), 1)
    to_lanes = jnp.where(lax.broadcasted_iota(jnp.int32, (n_blocks, hd), 0)
                         == lax.broadcasted_iota(jnp.int32, (n_blocks, hd), 1), 1.0, 0.0).astype(BF16)

    def rope(x, rows):
        x = x.astype(F32)
        partner = jnp.where(head_lane < half, pltpu.roll(x, wide - half, 1), pltpu.roll(x, half, 1))
        cos = jnp.concatenate([cos_ref[rows, :]] * gh, axis=1)
        sin = jnp.concatenate([sin_ref[rows, :]] * gh, axis=1)
        return x * cos + partner * sin

    for j in range(n_blocks):
        rows = slice(j * blk, (j + 1) * blk)
        kr = _per_head(rope(k_ref[rows, :], rows), hd)
        kr_s[:, rows, :hd] = kr.astype(BF16)
        kr_s[:, rows, hd:] = jnp.broadcast_to(jnp.where(lane == j, 1.0, 0.0), (gh, blk, hd)).astype(BF16)
        km_s[:, j:j + 1, :] = jnp.mean(kr, axis=1, keepdims=True)
        vb_s[:, rows, :hd] = _per_head(v_ref[rows, :], hd).astype(BF16)
        vb_s[:, rows, hd:] = jnp.ones((gh, blk, hd), BF16)

    for qb in range(n_blocks):
        rows = slice(qb * blk, (qb + 1) * blk)
        qr = _per_head(rope(q_ref[rows, :], rows), hd)
        qb16 = (qr * (scale * LOG2_E)).astype(BF16)
        bias = jnp.zeros((gh, blk, hd), BF16)
        if qb > 0:
            gate = jnp.where(blk_id < qb, _bdot_nt_f32(km_s[...], qr), NEG_INF)
            picked = []
            for j in range(qb):
                gj = gate[:, j:j + 1, :]
                beats = jnp.where(blk_id < j, jnp.where(gate >= gj, 1.0, 0.0), jnp.where(gate > gj, 1.0, 0.0))
                rank = jnp.sum(beats, axis=1, keepdims=True)
                picked.append(jnp.where(rank < MOBA_TOPK, 1.0, 0.0))
            picked.append(jnp.zeros((gh, n_blocks - qb, blk), F32))
            picked = jnp.concatenate(picked, axis=1)
            keep = jnp.stack([_dot_tn(picked[g], to_lanes) for g in range(gh)])
            bias = jnp.where(lane < qb, (1.0 - keep) * NEG_INF, 0.0).astype(BF16)
        width = (qb + 1) * blk
        logits = _bdot_nt(jnp.concatenate([qb16, bias], axis=2), kr_s[:, :width, :])
        if qb > 0:
            s_s[:, :, :qb * blk] = logits[:, :, :qb * blk]
        s_s[:, :, rows] = jnp.where(causal, logits[:, :, qb * blk:], NEG_INF)
        top = s_s[:, :, :LANES]
        for c0 in range(LANES, width, LANES):
            top = jnp.maximum(top, s_s[:, :, c0:c0 + LANES])
        p = jnp.exp2(s_s[:, :, :width] - jnp.max(top, axis=2, keepdims=True))
        pv = _bdot(p, vb_s[:, :width, :])
        o_ref[rows, :] = _merge_heads(pv[:, :, :hd] / pv[:, :, hd:]).astype(o_ref.dtype)


def _moba_mixer(proj, cos, sin):
    bsz, s, _ = proj.shape
    d, h, gh = HEAD_DIM, MOBA_HEADS, MOBA_GROUP
    groups = h // gh
    seq = lambda off: pl.BlockSpec((None, s, gh * d), lambda b, i: (b, 0, off + i))
    tab = pl.BlockSpec((None, s, LANES), lambda b, i: (b, 0, 0))
    return pl.pallas_call(
        functools.partial(_moba_kernel, n_blocks=s // MOBA_BLOCK),
        grid=(bsz, groups),
        in_specs=[seq(0), seq(groups), seq(2 * groups), tab, tab],
        out_specs=pl.BlockSpec((None, s, gh * d), lambda b, i: (b, 0, i)),
        out_shape=jax.ShapeDtypeStruct((bsz, s, h * d), BF16),
        scratch_shapes=[pltpu.VMEM((gh, s, 2 * d), BF16),
                        pltpu.VMEM((gh, s, 2 * d), BF16),
                        pltpu.VMEM((gh, s // MOBA_BLOCK, d), F32),
                        pltpu.VMEM((gh, MOBA_BLOCK, s), F32)],
        compiler_params=_params("parallel", "parallel"),
        name="moba",
    )(proj, proj, proj, cos, sin)


def _pad_cols(w, multiple=LANES):
    n = w.shape[1]
    return jnp.pad(w, ((0, 0), (0, -n % multiple)))


def kernel(x, positions, gla_w_in, gla_w_gk, gla_b_gk, gla_norm_g, gla_w_out, moba_w_in, moba_w_out,
           gdn_w_in, gdn_conv_w, gdn_a_log, gdn_dt_bias, gdn_norm_g, gdn_w_out, hgrn_lower_bounds,
           hgrn_w_in, hgrn_norm_g, hgrn_w_out, ffn_w_gu, ffn_w_down, ln_g, ln_b):
    bsz, s, d = x.shape
    assert (s % MOBA_BLOCK, (bsz * s) % TOKEN_TILE, d) == (0, 0, D_MODEL)
    t = bsz * s
    xf = x.reshape(t, d)
    cos = sin = None
    for i in range(DEPTH):
        kind, j = i % N_MIXERS, i // N_MIXERS
        tokens = lambda a: a.reshape(bsz, s, -1)
        if kind == 0:
            n_main = sum(GLA_COLS[:4])
            low_w = _pad_cols(gla_w_in[j, :, n_main:])[None]
            proj, low = _proj(xf, [(gla_w_in, j, 0, n_main, BF16), (low_w, 0, 0, low_w.shape[2], F32)])
            o = _gla_mixer(tokens(proj), tokens(low), gla_w_gk[j], gla_b_gk[j], gla_norm_g[j])
            w_out = gla_w_out
        elif kind == 1:
            if cos is None:
                cos, sin = _rope_tables(positions)
            proj, = _proj(xf, [(moba_w_in, j, 0, moba_w_in.shape[2], BF16)])
            o = _moba_mixer(tokens(proj), cos, sin)
            w_out = moba_w_out
        elif kind == 2:
            n_main = sum(GDN_COLS[:2])
            ab_w = _pad_cols(gdn_w_in[j, :, n_main:])[None]
            proj, ab = _proj(xf, [(gdn_w_in, j, 0, n_main, BF16), (ab_w, 0, 0, ab_w.shape[2], F32)])
            o = _gdn_mixer(tokens(proj), tokens(ab), gdn_conv_w[j], gdn_a_log[j], gdn_dt_bias[j], gdn_norm_g[j])
            w_out = gdn_w_out
        else:
            nq, nf = HGRN_COLS[0], HGRN_COLS[1]
            q, f_logit, ig = _proj(xf, [(hgrn_w_in, j, 0, nq, BF16), (hgrn_w_in, j, nq, nf, F32),
                                        (hgrn_w_in, j, nq + nf, sum(HGRN_COLS[2:]), BF16)])
            o = _hgrn_mixer(tokens(q), tokens(f_logit), tokens(ig), hgrn_lower_bounds, hgrn_norm_g[j], i)
            w_out = hgrn_w_out
        xf = _layer_tail(o.reshape(t, d), w_out, j, xf, ffn_w_gu, ffn_w_down, i, ln_g, ln_b)
    return xf.reshape(bsz, s, d)
```

```python
import functools

import numpy as np
import jax
import jax.numpy as jnp
from jax import lax
from jax.experimental import pallas as pl
from jax.experimental.pallas import tpu as pltpu

F32 = jnp.float32
BF16 = jnp.bfloat16
HIGHEST = lax.Precision.HIGHEST

D_MODEL = 1024
DEPTH = 4
N_MIXERS = 4
HEAD_DIM = 128
CHUNK = 64
SUPER = 256
MIX_GROUP = 4
HGRN_GROUP = 8
MOBA_GROUP = 4
GLA_HEADS = 4
GLA_DK = 128
GLA_DV = 256
GLA_GATE_RANK = 16
GLA_GATE_NORMALIZER = 16.0
MOBA_HEADS = 8
MOBA_BLOCK = 256
MOBA_TOPK = 3
ROPE_THETA = 500000.0
ROPE_DIM = 32
NEG_INF = -1e30
LOG2_E = 1.4426950408889634
GDN_HEADS = 8
GDN_CONV = 4
GDN_ROWS = 256
GDN_CHUNK = 128
GDN_INV_BASE = 16
GDN_GROUP = 8
GLA_COLS = (512, 512, 1024, 1024, 16)
GDN_COLS = (3072, 1024, 8, 8)
HGRN_COLS = (1024, 1024, 1024, 1024)
HGRN_HEADS = 8
FFN_HIDDEN = 2816
DEEPNORM_ALPHA = (2.0 * DEPTH) ** 0.25

LANES = 128
SUBLANES = 8
BF16_ROWS = 16
VMEM_LIMIT = 56 * 1024 * 1024
TOKEN_TILE = 512
COL_CHUNK = 512
TAIL_TILE = 512
LN_SUBTILE = 256


def _dot_tn(a, b):
    return lax.dot_general(a.astype(BF16), b.astype(BF16), (((0,), (0,)), ((), ())),
                           preferred_element_type=F32)


def _bdot(a, b):
    return lax.dot_general(a.astype(BF16), b.astype(BF16), (((2,), (1,)), ((0,), (0,))),
                           preferred_element_type=F32)


def _bdot_nt(a, b):
    return lax.dot_general(a.astype(BF16), b.astype(BF16), (((2,), (2,)), ((0,), (0,))),
                           preferred_element_type=F32)


def _bdot_nt_f32(a, b):
    return lax.dot_general(a, b, (((2,), (2,)), ((0,), (0,))), precision=HIGHEST,
                           preferred_element_type=F32)


def _sigmoid(x):
    return 1.0 / (1.0 + jnp.exp(-x))


def _silu(x):
    return x * _sigmoid(x)


def _softplus(x):
    return jnp.maximum(x, 0.0) + jnp.log1p(jnp.exp(-jnp.abs(x)))


def _log_sigmoid(x):
    return -_softplus(-x)


def _col_chunks(n, width=COL_CHUNK):
    return tuple((c0, min(width, n - c0)) for c0 in range(0, n, width))


def _params(*semantics):
    return pltpu.CompilerParams(dimension_semantics=semantics, vmem_limit_bytes=VMEM_LIMIT)


def _resident(shape):
    zeros = (0,) * len(shape)
    return pl.BlockSpec(shape, lambda *_: zeros, pipeline_mode=pl.Buffered(1))


def _layer_norm(z, g, b):
    mu = jnp.mean(z, axis=-1, keepdims=True)
    zc = z - mu
    var = jnp.mean(zc * zc, axis=-1, keepdims=True)
    return zc * lax.rsqrt(var + 1e-5) * g + b


def _tri_masks(n):
    row = lax.broadcasted_iota(jnp.int32, (n, n), 0)
    col = lax.broadcasted_iota(jnp.int32, (n, n), 1)
    return row >= col, row > col, row == col


def _proj_kernel(x_ref, *refs):
    xb = x_ref[...].astype(BF16)
    n_out = len(refs) // 2
    for w_ref, o_ref in zip(refs[:n_out], refs[n_out:]):
        for c0, cw in _col_chunks(w_ref.shape[1]):
            o_ref[:, c0:c0 + cw] = jnp.dot(xb, w_ref[:, c0:c0 + cw].astype(BF16),
                                           preferred_element_type=F32).astype(o_ref.dtype)


def _layer_window(layer, k, n, col_block=0):
    return pl.BlockSpec((None, k, n), lambda *_: (layer, 0, col_block), pipeline_mode=pl.Buffered(1))


def _proj(x, windows):
    t, k = x.shape
    row = lambda i: (i, 0)
    w_specs = []
    for _, layer, c0, n, _ in windows:
        assert c0 % n == 0 and n % LANES == 0
        w_specs.append(_layer_window(layer, k, n, c0 // n))
    return pl.pallas_call(
        _proj_kernel,
        grid=(t // TOKEN_TILE,),
        in_specs=[pl.BlockSpec((TOKEN_TILE, k), row)] + w_specs,
        out_specs=[pl.BlockSpec((TOKEN_TILE, n), row) for _, _, _, n, _ in windows],
        out_shape=[jax.ShapeDtypeStruct((t, n), dt) for _, _, _, n, dt in windows],
        compiler_params=_params("parallel"),
        name="proj",
    )(x, *[w for w, _, _, _, _ in windows])


def _tail_kernel(o_ref, wo_ref, x_ref, g1_ref, b1_ref, wgu_ref, wd_ref, g2_ref, b2_ref, y_ref, h_ref, *, hidden):
    tile = y_ref.shape[0]
    wo = wo_ref[...].astype(BF16)
    for r0 in range(0, tile, LN_SUBTILE):
        rows = slice(r0, r0 + LN_SUBTILE)
        y = jnp.dot(o_ref[rows, :], wo, preferred_element_type=F32)
        y_ref[rows, :] = _layer_norm(DEEPNORM_ALPHA * x_ref[rows, :] + y, g1_ref[...], b1_ref[...])
    xb = y_ref[...].astype(BF16)
    for c0, cw in _col_chunks(hidden, 256):
        gate = jnp.dot(xb, wgu_ref[:, c0:c0 + cw].astype(BF16), preferred_element_type=F32)
        up = jnp.dot(xb, wgu_ref[:, hidden + c0:hidden + c0 + cw].astype(BF16), preferred_element_type=F32)
        h_ref[:, c0:c0 + cw] = (_silu(gate) * up).astype(BF16)
    for r0 in range(0, tile, LN_SUBTILE):
        rows = slice(r0, r0 + LN_SUBTILE)
        down = jnp.concatenate([jnp.dot(h_ref[rows, :], wd_ref[:, c0:c0 + cw].astype(BF16),
                                        preferred_element_type=F32) for c0, cw in _col_chunks(y_ref.shape[1])], axis=1)
        y_ref[rows, :] = _layer_norm(DEEPNORM_ALPHA * y_ref[rows, :] + down, g2_ref[...], b2_ref[...])


def _layer_tail(o, w_out, out_layer, x, w_gu, w_down, layer, ln_g, ln_b):
    t, k = o.shape
    d = x.shape[1]
    hidden = w_down.shape[1]
    row = lambda i: (i, 0)
    vec = lambda a: a.reshape(1, d)
    return pl.pallas_call(
        functools.partial(_tail_kernel, hidden=hidden),
        grid=(t // TAIL_TILE,),
        in_specs=[pl.BlockSpec((TAIL_TILE, k), row), _layer_window(out_layer, k, d),
                  pl.BlockSpec((TAIL_TILE, d), row), _resident((1, d)), _resident((1, d)),
                  _layer_window(layer, d, 2 * hidden), _layer_window(layer, hidden, d),
                  _resident((1, d)), _resident((1, d))],
        out_specs=pl.BlockSpec((TAIL_TILE, d), row),
        out_shape=jax.ShapeDtypeStruct((t, d), F32),
        scratch_shapes=[pltpu.VMEM((TAIL_TILE, hidden), BF16)],
        compiler_params=_params("parallel"),
        name="layer_tail",
    )(o, w_out, x, vec(ln_g[layer, 0]), vec(ln_b[layer, 0]), w_gu, w_down, vec(ln_g[layer, 1]), vec(ln_b[layer, 1]))


def _split3(x):
    hi = x.astype(BF16)
    rest = x - hi.astype(F32)
    mid = rest.astype(BF16)
    return hi, mid, (rest - mid.astype(F32)).astype(BF16)


def _per_head(x, width):
    return jnp.stack([x[:, j * width:(j + 1) * width] for j in range(x.shape[1] // width)])


def _merge_heads(x):
    return jnp.concatenate([x[j] for j in range(x.shape[0])], axis=1)


def _gla_block(q, k, v, log_g, st):
    g, rows, dk = q.shape
    n = rows // CHUNK
    row = lax.broadcasted_iota(jnp.int32, (rows, rows), 0)
    col = lax.broadcasted_iota(jnp.int32, (rows, rows), 1)
    chunk_start = jnp.bitwise_and(row, -CHUNK)
    causal = jnp.logical_and(col <= row, col >= chunk_start)
    parts = [p for j in range(g) for p in _split3(log_g[j])]
    acc = jnp.dot(jnp.where(causal, 1.0, 0.0).astype(BF16), jnp.concatenate(parts, axis=1),
                  preferred_element_type=F32)
    cum = jnp.stack([acc[:, 3 * j * dk:(3 * j + 1) * dk] + acc[:, (3 * j + 1) * dk:(3 * j + 2) * dk]
                     + acc[:, (3 * j + 2) * dk:(3 * j + 3) * dk] for j in range(g)])
    lasts = [cum[:, (i + 1) * CHUNK - 1:(i + 1) * CHUNK, :] for i in range(n)]
    total = jnp.concatenate([jnp.broadcast_to(last, (g, CHUNK, dk)) for last in lasts], axis=1)
    q_dec = (q * jnp.exp(cum)).astype(BF16)
    k_inv = k * jnp.exp(-cum)
    k_end = (k * jnp.exp(total - cum)).astype(BF16)
    vb = v.astype(BF16)
    scores = jnp.where(causal, _bdot_nt(q_dec, k_inv), 0.0)
    o_intra = _bdot(scores, vb)
    o_inter = []
    for i in range(n):
        rs = slice(i * CHUNK, (i + 1) * CHUNK)
        o_inter.append(_bdot_nt(q_dec[:, rs, :], st))
        update = jnp.stack([_dot_tn(vb[j, rs, :], k_end[j, rs, :]) for j in range(g)])
        st = st * jnp.exp(lasts[i]) + update
    return o_intra + jnp.concatenate(o_inter, axis=1), st


def _gla_finish(o, norm_g, gate, o_ref, rows):
    o = o * lax.rsqrt(jnp.mean(o * o, axis=-1, keepdims=True) + 1e-6) * norm_g
    o_ref[rows, :] = (_merge_heads(o) * _silu(gate.astype(F32))).astype(o_ref.dtype)


def _gla_kernel(q_ref, k_ref, v_ref, g_ref, low_ref, wgk_ref, bgk_ref, ng_ref, o_ref, st_ref, *, n_blocks):
    st_ref[...] = jnp.zeros_like(st_ref)
    wgk = wgk_ref[...].astype(BF16)
    bgk = bgk_ref[...]
    norm_g = ng_ref[...]

    def body(c, carry):
        rows = pl.ds(pl.multiple_of(c * SUPER, SUPER), SUPER)
        logit = jnp.dot(low_ref[rows, :].astype(BF16), wgk, preferred_element_type=F32) + bgk
        log_g = _per_head(_log_sigmoid(logit) * (1.0 / GLA_GATE_NORMALIZER), GLA_DK)
        q = _per_head(q_ref[rows, :].astype(F32), GLA_DK) * (GLA_DK ** -0.5)
        o, st = _gla_block(q, _per_head(k_ref[rows, :].astype(F32), GLA_DK), _per_head(v_ref[rows, :], GLA_DV),
                           log_g, st_ref[...])
        st_ref[...] = st
        _gla_finish(o, norm_g, g_ref[rows, :], o_ref, rows)
        return carry

    lax.fori_loop(0, n_blocks, body, 0)


def _gla_mixer(proj, low, w_gk, b_gk, norm_g):
    bsz, s, _ = proj.shape
    dk, dv, h, gh = GLA_DK, GLA_DV, GLA_HEADS, MIX_GROUP
    groups = h // gh
    wgk_pad = jnp.zeros((LANES, h * dk), F32).at[:GLA_GATE_RANK].set(w_gk)
    seq = lambda width, off: pl.BlockSpec((None, s, gh * width), lambda b, i: (b, 0, off + i))
    return pl.pallas_call(
        functools.partial(_gla_kernel, n_blocks=s // SUPER),
        grid=(bsz, groups),
        in_specs=[seq(dk, 0), seq(dk, groups), seq(dv, (2 * h * dk) // (gh * dv)),
                  seq(dv, (2 * h * dk) // (gh * dv) + groups),
                  pl.BlockSpec((None, s, LANES), lambda b, i: (b, 0, 0)),
                  pl.BlockSpec((LANES, gh * dk), lambda b, i: (0, i)),
                  pl.BlockSpec((1, gh * dk), lambda b, i: (0, i)),
                  pl.BlockSpec((1, dv), lambda b, i: (0, 0))],
        out_specs=pl.BlockSpec((None, s, gh * dv), lambda b, i: (b, 0, i)),
        out_shape=jax.ShapeDtypeStruct((bsz, s, h * dv), BF16),
        scratch_shapes=[pltpu.VMEM((gh, dv, dk), F32)],
        compiler_params=_params("parallel", "parallel"),
        name="gla",
    )(proj, proj, proj, proj, low, wgk_pad, b_gk.reshape(1, -1), norm_g.reshape(1, -1))


def _hgrn_kernel(q_ref, f_ref, i_ref, g_ref, lb_ref, ng_ref, o_ref, st_ref, *, n_blocks, layer):
    st_ref[...] = jnp.zeros_like(st_ref)
    norm_g = ng_ref[...]
    raw = lb_ref[...]
    e = jnp.exp(raw - jnp.max(raw, axis=0, keepdims=True))
    soft = e / jnp.sum(e, axis=0, keepdims=True)
    lb = jnp.sum(soft[:layer + 1, :], axis=0, keepdims=True) - soft[0:1, :]

    def body(c, carry):
        rows = pl.ds(pl.multiple_of(c * SUPER, SUPER), SUPER)
        f = f_ref[rows, :]
        forget = lb + (1.0 - lb) * _sigmoid(f)
        k = (1.0 - lb) * _sigmoid(-f)
        o, st = _gla_block(_per_head(q_ref[rows, :].astype(F32), HEAD_DIM), _per_head(k, HEAD_DIM),
                           _per_head(i_ref[rows, :], HEAD_DIM), _per_head(jnp.log(forget), HEAD_DIM), st_ref[...])
        st_ref[...] = st
        _gla_finish(o, norm_g, g_ref[rows, :], o_ref, rows)
        return carry

    lax.fori_loop(0, n_blocks, body, 0)


def _hgrn_mixer(q, f_logit, ig, lower_bounds, norm_g, layer):
    bsz, s, _ = q.shape
    d, h, gh = HEAD_DIM, HGRN_HEADS, HGRN_GROUP
    groups = h // gh
    seq = lambda off: pl.BlockSpec((None, s, gh * d), lambda b, i: (b, 0, off + i))
    return pl.pallas_call(
        functools.partial(_hgrn_kernel, n_blocks=s // SUPER, layer=layer),
        grid=(bsz, groups),
        in_specs=[seq(0), seq(0), seq(0), seq(groups),
                  pl.BlockSpec((DEPTH, gh * d), lambda b, i: (0, i)),
                  pl.BlockSpec((1, d), lambda b, i: (0, 0))],
        out_specs=pl.BlockSpec((None, s, gh * d), lambda b, i: (b, 0, i)),
        out_shape=jax.ShapeDtypeStruct((bsz, s, h * d), BF16),
        scratch_shapes=[pltpu.VMEM((gh, d, d), F32)],
        compiler_params=_params("parallel", "parallel"),
        name="hgrn2",
    )(q, f_logit, ig, ig, lower_bounds, norm_g.reshape(1, -1))


def _gdn_kernel(q_ref, k_ref, v_ref, z_ref, ab_ref, cwq_ref, cwk_ref, cwv_ref, alog_ref, dtb_ref, ng_ref,
                o_ref, st_ref, *, n_chunks):
    rn, cs, gh, hd = GDN_ROWS, GDN_CHUNK, GDN_GROUP, HEAD_DIM
    n_sub = rn // cs
    head0 = pl.program_id(1) * gh
    row = lax.broadcasted_iota(jnp.int32, (cs, cs), 0)
    col = lax.broadcasted_iota(jnp.int32, (cs, cs), 1)
    incl, strict, eye = row >= col, row > col, row == col
    eye_f = jnp.where(eye, 1.0, 0.0)
    block_bits = jnp.bitwise_xor(row, col)
    big_row = lax.broadcasted_iota(jnp.int32, (rn, rn), 0)
    big_col = lax.broadcasted_iota(jnp.int32, (rn, rn), 1)
    tril_b = jnp.where(jnp.logical_and(big_col <= big_row, big_col >= jnp.bitwise_and(big_row, -cs)),
                       1.0, 0.0).astype(BF16)
    lane = lax.broadcasted_iota(jnp.int32, (1, LANES), 1)
    norm_g = ng_ref[...]
    neg_a = -jnp.exp(alog_ref[...])
    dt_bias = dtb_ref[...]
    st_ref[...] = jnp.zeros_like(st_ref)

    def conv_silu(ref, cw_ref, c):
        r0 = pl.multiple_of(c * rn, rn)
        hist = BF16_ROWS
        cur = ref[pl.ds(r0, rn), :].astype(F32)
        prev = ref[pl.ds(pl.multiple_of(jnp.maximum(r0 - hist, 0), hist), hist), :].astype(F32)
        prev = jnp.where(c > 0, prev, 0.0)
        ext = jnp.concatenate([prev, cur], axis=0)
        w = cw_ref[...]
        y = ext[hist:, :] * w[GDN_CONV - 1:GDN_CONV, :]
        for j in range(GDN_CONV - 1):
            lo = hist - (GDN_CONV - 1) + j
            y = y + ext[lo:lo + rn, :] * w[j:j + 1, :]
        return _silu(y)

    def per_chunk(x):
        width = x.shape[1] // gh
        return jnp.stack([x[s * cs:(s + 1) * cs, j * width:(j + 1) * width]
                          for j in range(gh) for s in range(n_sub)])

    def pick(x, first):
        return jnp.concatenate([jnp.broadcast_to(jnp.sum(jnp.where(lane == first + j, x, 0.0), axis=1,
                                                         keepdims=True), (rn, hd)) for j in range(gh)], axis=1)

    def of_sub(x, s):
        return jnp.stack([x[j * n_sub + s] for j in range(gh)])

    def body(c, carry):
        rows = pl.ds(pl.multiple_of(c * rn, rn), rn)
        q = per_chunk(conv_silu(q_ref, cwq_ref, c))
        k = per_chunk(conv_silu(k_ref, cwk_ref, c))
        v = per_chunk(conv_silu(v_ref, cwv_ref, c))
        q = q * lax.rsqrt(jnp.sum(q * q, axis=-1, keepdims=True) + 1e-6) * (hd ** -0.5)
        k = k * lax.rsqrt(jnp.sum(k * k, axis=-1, keepdims=True) + 1e-6)
        ab = ab_ref[rows, :]
        g_wide = pick(neg_a * _softplus(ab + dt_bias), head0)
        beta = per_chunk(pick(_sigmoid(ab), head0 + GDN_HEADS))
        cum = per_chunk(sum(jnp.dot(tril_b, part, preferred_element_type=F32) for part in _split3(g_wide)))
        cum_row = jnp.sum(jnp.where(eye, cum, 0.0), axis=1, keepdims=True)
        grow = jnp.exp(jnp.where(incl, cum - cum_row, 0.0))
        k_beta = k * beta
        lower = jnp.where(strict, _bdot_nt(k_beta, k) * grow, 0.0)
        base_bits = int(np.log2(GDN_INV_BASE))
        power = -jnp.where(jnp.right_shift(block_bits, base_bits) == 0, lower, 0.0)
        inv = eye_f + power
        for _ in range(base_bits - 1):
            power = _bdot(power, power)
            inv = inv + _bdot(inv, power)
        for bits in range(base_bits, int(np.log2(cs))):
            cross = jnp.where(jnp.right_shift(block_bits, bits) == 1, lower, 0.0)
            inv = inv - _bdot(_bdot(inv, cross), inv)
        e_cum = jnp.exp(cum)
        sol = _bdot(inv, jnp.concatenate([v * beta, k_beta * e_cum], axis=2))
        u, w_qd = sol[:, :, :hd], jnp.concatenate([sol[:, :, hd:], q * e_cum], axis=1)
        qk = jnp.where(incl, _bdot_nt(q, k) * grow, 0.0)
        last = cum[:, cs - 1:cs, :]
        k_end = k * jnp.exp(last - cum)
        carry_decay = jnp.exp(last)
        st = st_ref[...]
        outs = []
        for s in range(n_sub):
            t = _bdot(of_sub(w_qd, s), st)
            v_new = of_sub(u, s) - t[:, :cs, :]
            outs.append(t[:, cs:, :] + _bdot(of_sub(qk, s), v_new))
            ke = of_sub(k_end, s)
            st = st * of_sub(carry_decay, s) + jnp.stack([_dot_tn(ke[j], v_new[j]) for j in range(gh)])
        st_ref[...] = st
        o = jnp.concatenate(outs, axis=1)
        o = o * lax.rsqrt(jnp.mean(o * o, axis=-1, keepdims=True) + 1e-6) * norm_g
        o_ref[rows, :] = (_merge_heads(o) * _silu(z_ref[rows, :].astype(F32))).astype(o_ref.dtype)
        return carry

    lax.fori_loop(0, n_chunks, body, 0)


def _gdn_mixer(proj, ab, conv_w, a_log, dt_bias, norm_g):
    bsz, s, _ = proj.shape
    d, h = HEAD_DIM, GDN_HEADS
    assert GDN_CHUNK == HEAD_DIM and s % GDN_ROWS == 0
    n_chunks = s // GDN_ROWS
    gh = GDN_GROUP
    groups = h // gh
    seq = lambda off: pl.BlockSpec((None, s, gh * d), lambda b, i: (b, 0, off + i))
    cw = lambda off: pl.BlockSpec((GDN_CONV, gh * d), lambda b, i: (0, off + i))
    vec = pl.BlockSpec((1, LANES), lambda b, i: (0, 0))
    pad = lambda a: jnp.zeros((1, LANES), F32).at[0, :h].set(a)
    return pl.pallas_call(
        functools.partial(_gdn_kernel, n_chunks=n_chunks),
        grid=(bsz, groups),
        in_specs=[seq(0), seq(groups), seq(2 * groups), seq(3 * groups),
                  pl.BlockSpec((None, s, LANES), lambda b, i: (b, 0, 0)),
                  cw(0), cw(groups), cw(2 * groups), vec, vec, vec],
        out_specs=pl.BlockSpec((None, s, gh * d), lambda b, i: (b, 0, i)),
        out_shape=jax.ShapeDtypeStruct((bsz, s, h * d), BF16),
        scratch_shapes=[pltpu.VMEM((gh, d, d), F32)],
        compiler_params=_params("parallel", "parallel"),
        name="gdn",
    )(proj, proj, proj, proj, ab, conv_w, conv_w, conv_w, pad(a_log), pad(dt_bias),
      norm_g.reshape(1, -1))


def _rope_table_kernel(pos_ref, invf_ref, cos_ref, sin_ref):
    half = ROPE_DIM // 2
    rest = LANES - ROPE_DIM
    for c0 in range(0, pos_ref.shape[1], LANES):
        ang = invf_ref[...] * pos_ref[:, c0:c0 + LANES].astype(F32)
        c, s = jnp.cos(ang), jnp.sin(ang)
        cos_t = jnp.concatenate([c, c, jnp.ones((rest, LANES), F32)], axis=0)
        sin_t = jnp.concatenate([-s, s, jnp.zeros((rest, LANES), F32)], axis=0)
        cos_ref[c0:c0 + LANES, :] = cos_t.T
        sin_ref[c0:c0 + LANES, :] = sin_t.T


def _rope_tables(positions):
    bsz, s = positions.shape
    half = ROPE_DIM // 2
    inv_freq = (np.float32(ROPE_THETA) ** (-np.arange(0, ROPE_DIM, 2, dtype=np.float32) / ROPE_DIM)).astype(np.float32)
    invf = np.ascontiguousarray(np.broadcast_to(inv_freq[:, None], (half, LANES)))
    out = jax.ShapeDtypeStruct((bsz, s, LANES), F32)
    return pl.pallas_call(
        _rope_table_kernel,
        grid=(bsz,),
        in_specs=[pl.BlockSpec((None, 1, s), lambda b: (b, 0, 0)),
                  pl.BlockSpec((half, LANES), lambda b: (0, 0))],
        out_specs=[pl.BlockSpec((None, s, LANES), lambda b: (b, 0, 0))] * 2,
        out_shape=[out, out],
        compiler_params=_params("parallel"),
        name="rope_table",
    )(positions.reshape(bsz, 1, s), jnp.asarray(invf))


def _moba_kernel(q_ref, k_ref, v_ref, cos_ref, sin_ref, o_ref, kr_s, vb_s, km_s, s_s, *, n_blocks):
    blk, hd, gh = MOBA_BLOCK, HEAD_DIM, MOBA_GROUP
    half = ROPE_DIM // 2
    scale = hd ** -0.5
    wide = gh * hd
    causal, _, _ = _tri_masks(blk)
    head_lane = jnp.bitwise_and(lax.broadcasted_iota(jnp.int32, (1, wide), 1), hd - 1)
    lane = lax.broadcasted_iota(jnp.int32, (1, hd), 1)
    blk_id = lax.broadcasted_iota(jnp.int32, (1, n_blocks, 1), 1)
    to_lanes = jnp.where(lax.broadcasted_iota(jnp.int32, (n_blocks, hd), 0)
                         == lax.broadcasted_iota(jnp.int32, (n_blocks, hd), 1), 1.0, 0.0).astype(BF16)

    def rope(x, rows):
        x = x.astype(F32)
        partner = jnp.where(head_lane < half, pltpu.roll(x, wide - half, 1), pltpu.roll(x, half, 1))
        cos = jnp.concatenate([cos_ref[rows, :]] * gh, axis=1)
        sin = jnp.concatenate([sin_ref[rows, :]] * gh, axis=1)
        return x * cos + partner * sin

    for j in range(n_blocks):
        rows = slice(j * blk, (j + 1) * blk)
        kr = _per_head(rope(k_ref[rows, :], rows), hd)
        kr_s[:, rows, :hd] = kr.astype(BF16)
        kr_s[:, rows, hd:] = jnp.broadcast_to(jnp.where(lane == j, 1.0, 0.0), (gh, blk, hd)).astype(BF16)
        km_s[:, j:j + 1, :] = jnp.mean(kr, axis=1, keepdims=True)
        vb_s[:, rows, :hd] = _per_head(v_ref[rows, :], hd).astype(BF16)
        vb_s[:, rows, hd:] = jnp.ones((gh, blk, hd), BF16)

    for qb in range(n_blocks):
        rows = slice(qb * blk, (qb + 1) * blk)
        qr = _per_head(rope(q_ref[rows, :], rows), hd)
        qb16 = (qr * (scale * LOG2_E)).astype(BF16)
        bias = jnp.zeros((gh, blk, hd), BF16)
        if qb > 0:
            gate = jnp.where(blk_id < qb, _bdot_nt_f32(km_s[...], qr), NEG_INF)
            picked = []
            for j in range(qb):
                gj = gate[:, j:j + 1, :]
                beats = jnp.where(blk_id < j, jnp.where(gate >= gj, 1.0, 0.0), jnp.where(gate > gj, 1.0, 0.0))
                rank = jnp.sum(beats, axis=1, keepdims=True)
                picked.append(jnp.where(rank < MOBA_TOPK, 1.0, 0.0))
            picked.append(jnp.zeros((gh, n_blocks - qb, blk), F32))
            picked = jnp.concatenate(picked, axis=1)
            keep = jnp.stack([_dot_tn(picked[g], to_lanes) for g in range(gh)])
            bias = jnp.where(lane < qb, (1.0 - keep) * NEG_INF, 0.0).astype(BF16)
        width = (qb + 1) * blk
        logits = _bdot_nt(jnp.concatenate([qb16, bias], axis=2), kr_s[:, :width, :])
        if qb > 0:
            s_s[:, :, :qb * blk] = logits[:, :, :qb * blk]
        s_s[:, :, rows] = jnp.where(causal, logits[:, :, qb * blk:], NEG_INF)
        top = s_s[:, :, :LANES]
        for c0 in range(LANES, width, LANES):
            top = jnp.maximum(top, s_s[:, :, c0:c0 + LANES])
        p = jnp.exp2(s_s[:, :, :width] - jnp.max(top, axis=2, keepdims=True))
        pv = _bdot(p, vb_s[:, :width, :])
        o_ref[rows, :] = _merge_heads(pv[:, :, :hd] / pv[:, :, hd:]).astype(o_ref.dtype)


def _moba_mixer(proj, cos, sin):
    bsz, s, _ = proj.shape
    d, h, gh = HEAD_DIM, MOBA_HEADS, MOBA_GROUP
    groups = h // gh
    seq = lambda off: pl.BlockSpec((None, s, gh * d), lambda b, i: (b, 0, off + i))
    tab = pl.BlockSpec((None, s, LANES), lambda b, i: (b, 0, 0))
    return pl.pallas_call(
        functools.partial(_moba_kernel, n_blocks=s // MOBA_BLOCK),
        grid=(bsz, groups),
        in_specs=[seq(0), seq(groups), seq(2 * groups), tab, tab],
        out_specs=pl.BlockSpec((None, s, gh * d), lambda b, i: (b, 0, i)),
        out_shape=jax.ShapeDtypeStruct((bsz, s, h * d), BF16),
        scratch_shapes=[pltpu.VMEM((gh, s, 2 * d), BF16),
                        pltpu.VMEM((gh, s, 2 * d), BF16),
                        pltpu.VMEM((gh, s // MOBA_BLOCK, d), F32),
                        pltpu.VMEM((gh, MOBA_BLOCK, s), F32)],
        compiler_params=_params("parallel", "parallel"),
        name="moba",
    )(proj, proj, proj, cos, sin)


def _pad_cols(w, multiple=LANES):
    n = w.shape[1]
    return jnp.pad(w, ((0, 0), (0, -n % multiple)))


def kernel(x, positions, gla_w_in, gla_w_gk, gla_b_gk, gla_norm_g, gla_w_out, moba_w_in, moba_w_out,
           gdn_w_in, gdn_conv_w, gdn_a_log, gdn_dt_bias, gdn_norm_g, gdn_w_out, hgrn_lower_bounds,
           hgrn_w_in, hgrn_norm_g, hgrn_w_out, ffn_w_gu, ffn_w_down, ln_g, ln_b):
    bsz, s, d = x.shape
    assert (s % MOBA_BLOCK, (bsz * s) % TOKEN_TILE, d) == (0, 0, D_MODEL)
    t = bsz * s
    xf = x.reshape(t, d)
    cos = sin = None
    for i in range(DEPTH):
        kind, j = i % N_MIXERS, i // N_MIXERS
        tokens = lambda a: a.reshape(bsz, s, -1)
        if kind == 0:
            n_main = sum(GLA_COLS[:4])
            low_w = _pad_cols(gla_w_in[j, :, n_main:])[None]
            proj, low = _proj(xf, [(gla_w_in, j, 0, n_main, BF16), (low_w, 0, 0, low_w.shape[2], F32)])
            o = _gla_mixer(tokens(proj), tokens(low), gla_w_gk[j], gla_b_gk[j], gla_norm_g[j])
            w_out = gla_w_out
        elif kind == 1:
            if cos is None:
                cos, sin = _rope_tables(positions)
            proj, = _proj(xf, [(moba_w_in, j, 0, moba_w_in.shape[2], BF16)])
            o = _moba_mixer(tokens(proj), cos, sin)
            w_out = moba_w_out
        elif kind == 2:
            n_main = sum(GDN_COLS[:2])
            ab_w = _pad_cols(gdn_w_in[j, :, n_main:])[None]
            proj, ab = _proj(xf, [(gdn_w_in, j, 0, n_main, BF16), (ab_w, 0, 0, ab_w.shape[2], F32)])
            o = _gdn_mixer(tokens(proj), tokens(ab), gdn_conv_w[j], gdn_a_log[j], gdn_dt_bias[j], gdn_norm_g[j])
            w_out = gdn_w_out
        else:
            nq, nf = HGRN_COLS[0], HGRN_COLS[1]
            q, f_logit, ig = _proj(xf, [(hgrn_w_in, j, 0, nq, BF16), (hgrn_w_in, j, nq, nf, F32),
                                        (hgrn_w_in, j, nq + nf, sum(HGRN_COLS[2:]), BF16)])
            o = _hgrn_mixer(tokens(q), tokens(f_logit), tokens(ig), hgrn_lower_bounds, hgrn_norm_g[j], i)
            w_out = hgrn_w_out
        xf = _layer_tail(o.reshape(t, d), w_out, j, xf, ffn_w_gu, ffn_w_down, i, ln_g, ln_b)
    return xf.reshape(bsz, s, d)
```

```python
import functools

import numpy as np
import jax
import jax.numpy as jnp
from jax import lax
from jax.experimental import pallas as pl
from jax.experimental.pallas import tpu as pltpu

F32 = jnp.float32
BF16 = jnp.bfloat16
HIGHEST = lax.Precision.HIGHEST

D_MODEL = 1024
DEPTH = 4
N_MIXERS = 4
HEAD_DIM = 128
CHUNK = 64
SUPER = 256
MIX_GROUP = 4
HGRN_GROUP = 8
MOBA_GROUP = 4
GLA_HEADS = 4
GLA_DK = 128
GLA_DV = 256
GLA_GATE_RANK = 16
GLA_GATE_NORMALIZER = 16.0
MOBA_HEADS = 8
MOBA_BLOCK = 256
MOBA_TOPK = 3
ROPE_THETA = 500000.0
ROPE_DIM = 32
NEG_INF = -1e30
LOG2_E = 1.4426950408889634
GDN_HEADS = 8
GDN_CONV = 4
GDN_ROWS = 256
GDN_CHUNK = 128
GDN_INV_BASE = 16
GDN_GROUP = 8
GLA_COLS = (512, 512, 1024, 1024, 16)
GDN_COLS = (3072, 1024, 8, 8)
HGRN_COLS = (1024, 1024, 1024, 1024)
HGRN_HEADS = 8
FFN_HIDDEN = 2816
DEEPNORM_ALPHA = (2.0 * DEPTH) ** 0.25

LANES = 128
SUBLANES = 8
BF16_ROWS = 16
VMEM_LIMIT = 56 * 1024 * 1024
TOKEN_TILE = 512
COL_CHUNK = 512
GDN_PROJ_CHUNK = 256
TAIL_TILE = 512
LN_SUBTILE = 256


def _dot_tn(a, b):
    return lax.dot_general(a.astype(BF16), b.astype(BF16), (((0,), (0,)), ((), ())),
                           preferred_element_type=F32)


def _bdot(a, b):
    return lax.dot_general(a.astype(BF16), b.astype(BF16), (((2,), (1,)), ((0,), (0,))),
                           preferred_element_type=F32)


def _bdot_nt(a, b):
    return lax.dot_general(a.astype(BF16), b.astype(BF16), (((2,), (2,)), ((0,), (0,))),
                           preferred_element_type=F32)


def _bdot_nt_f32(a, b):
    return lax.dot_general(a, b, (((2,), (2,)), ((0,), (0,))), precision=HIGHEST,
                           preferred_element_type=F32)


def _sigmoid(x):
    return 1.0 / (1.0 + jnp.exp(-x))


def _silu(x):
    return x * _sigmoid(x)


def _softplus(x):
    return jnp.maximum(x, 0.0) + jnp.log1p(jnp.exp(-jnp.abs(x)))


def _log_sigmoid(x):
    return -_softplus(-x)


def _col_chunks(n, width=COL_CHUNK):
    return tuple((c0, min(width, n - c0)) for c0 in range(0, n, width))


def _params(*semantics):
    return pltpu.CompilerParams(dimension_semantics=semantics, vmem_limit_bytes=VMEM_LIMIT)


def _resident(shape):
    zeros = (0,) * len(shape)
    return pl.BlockSpec(shape, lambda *_: zeros, pipeline_mode=pl.Buffered(1))


def _layer_norm(z, g, b):
    mu = jnp.mean(z, axis=-1, keepdims=True)
    zc = z - mu
    var = jnp.mean(zc * zc, axis=-1, keepdims=True)
    return zc * lax.rsqrt(var + 1e-5) * g + b


def _tri_masks(n):
    row = lax.broadcasted_iota(jnp.int32, (n, n), 0)
    col = lax.broadcasted_iota(jnp.int32, (n, n), 1)
    return row >= col, row > col, row == col


def _proj_kernel(x_ref, *refs):
    xb = x_ref[...].astype(BF16)
    n_out = len(refs) // 2
    for w_ref, o_ref in zip(refs[:n_out], refs[n_out:]):
        for c0, cw in _col_chunks(w_ref.shape[1]):
            o_ref[:, c0:c0 + cw] = jnp.dot(xb, w_ref[:, c0:c0 + cw].astype(BF16),
                                           preferred_element_type=F32).astype(o_ref.dtype)


def _layer_window(layer, k, n, col_block=0):
    return pl.BlockSpec((None, k, n), lambda *_: (layer, 0, col_block), pipeline_mode=pl.Buffered(1))


def _proj(x, windows):
    t, k = x.shape
    row = lambda i: (i, 0)
    w_specs = []
    for _, layer, c0, n, _ in windows:
        assert c0 % n == 0 and n % LANES == 0
        w_specs.append(_layer_window(layer, k, n, c0 // n))
    return pl.pallas_call(
        _proj_kernel,
        grid=(t // TOKEN_TILE,),
        in_specs=[pl.BlockSpec((TOKEN_TILE, k), row)] + w_specs,
        out_specs=[pl.BlockSpec((TOKEN_TILE, n), row) for _, _, _, n, _ in windows],
        out_shape=[jax.ShapeDtypeStruct((t, n), dt) for _, _, _, n, dt in windows],
        compiler_params=_params("parallel"),
        name="proj",
    )(x, *[w for w, _, _, _, _ in windows])


def _gdn_proj_kernel(x_ref, w_ref, wab_ref, cw_ref, o_ref, ab_ref, carry_ref, *, tiles_per_seq):
    i = pl.program_id(0)
    tile = x_ref.shape[0]
    hist = BF16_ROWS
    n_conv = cw_ref.shape[1]
    n_norm = 2 * n_conv // 3
    first = (i % tiles_per_seq) == 0

    @pl.when(i == 0)
    def _():
        carry_ref[...] = jnp.zeros_like(carry_ref)

    xb = x_ref[...].astype(BF16)
    for c0, cw in _col_chunks(w_ref.shape[1], GDN_PROJ_CHUNK):
        cols = slice(c0, c0 + cw)
        y = jnp.dot(xb, w_ref[:, cols].astype(BF16), preferred_element_type=F32)
        if c0 < n_conv:
            prev = jnp.where(first, 0.0, carry_ref[:, cols])
            carry_ref[:, cols] = y[tile - hist:, :]
            ext = jnp.concatenate([prev, y], axis=0)
            w = cw_ref[:, cols]
            y = ext[hist:, :] * w[GDN_CONV - 1:GDN_CONV, :]
            for j in range(GDN_CONV - 1):
                lo = hist - (GDN_CONV - 1) + j
                y = y + ext[lo:lo + tile, :] * w[j:j + 1, :]
        y = _silu(y)
        if c0 < n_norm:
            heads = _per_head(y, HEAD_DIM)
            heads = heads * lax.rsqrt(jnp.sum(heads * heads, axis=-1, keepdims=True) + 1e-6)
            y = _merge_heads(heads * (HEAD_DIM ** -0.5) if c0 < n_norm // 2 else heads)
        o_ref[:, cols] = y.astype(o_ref.dtype)
    ab_ref[...] = jnp.dot(xb, wab_ref[...].astype(BF16), preferred_element_type=F32)


def _gdn_proj(x, w_in, layer, ab_w, conv_w, seq_len):
    t, k = x.shape
    n_main = sum(GDN_COLS[:2])
    n_conv = GDN_COLS[0]
    assert n_conv % GDN_PROJ_CHUNK == 0 and (n_conv // 3) % GDN_PROJ_CHUNK == 0 and seq_len % TOKEN_TILE == 0
    row = lambda i: (i, 0)
    return pl.pallas_call(
        functools.partial(_gdn_proj_kernel, tiles_per_seq=seq_len // TOKEN_TILE),
        grid=(t // TOKEN_TILE,),
        in_specs=[pl.BlockSpec((TOKEN_TILE, k), row), _layer_window(layer, k, n_main),
                  _layer_window(0, k, ab_w.shape[2]), _layer_window(layer, GDN_CONV, n_conv)],
        out_specs=[pl.BlockSpec((TOKEN_TILE, n_main), row), pl.BlockSpec((TOKEN_TILE, ab_w.shape[2]), row)],
        out_shape=[jax.ShapeDtypeStruct((t, n_main), BF16), jax.ShapeDtypeStruct((t, ab_w.shape[2]), F32)],
        scratch_shapes=[pltpu.VMEM((BF16_ROWS, n_conv), F32)],
        compiler_params=_params("arbitrary"),
        name="gdn_proj",
    )(x, w_in, ab_w, conv_w)


def _moba_proj_kernel(x_ref, w_ref, cos_ref, sin_ref, o_ref):
    xb = x_ref[...].astype(BF16)
    half = ROPE_DIM // 2
    n_qk = 2 * (w_ref.shape[1] // 3)
    for c0, cw in _col_chunks(w_ref.shape[1]):
        cols = slice(c0, c0 + cw)
        y = jnp.dot(xb, w_ref[:, cols].astype(BF16), preferred_element_type=F32)
        if c0 < n_qk:
            head_lane = jnp.bitwise_and(lax.broadcasted_iota(jnp.int32, (1, cw), 1), HEAD_DIM - 1)
            partner = jnp.where(head_lane < half, pltpu.roll(y, cw - half, 1), pltpu.roll(y, half, 1))
            reps = cw // HEAD_DIM
            y = (y * jnp.concatenate([cos_ref[...]] * reps, axis=1)
                 + partner * jnp.concatenate([sin_ref[...]] * reps, axis=1))
            if c0 < n_qk // 2:
                y = y * (HEAD_DIM ** -0.5 * LOG2_E)
        o_ref[:, cols] = y.astype(o_ref.dtype)


def _moba_proj(x, w_in, layer, cos, sin):
    t, k = x.shape
    n = w_in.shape[2]
    assert (n // 3) % COL_CHUNK == 0
    row = lambda i: (i, 0)
    tab = pl.BlockSpec((TOKEN_TILE, LANES), row)
    return pl.pallas_call(
        _moba_proj_kernel,
        grid=(t // TOKEN_TILE,),
        in_specs=[pl.BlockSpec((TOKEN_TILE, k), row), _layer_window(layer, k, n), tab, tab],
        out_specs=pl.BlockSpec((TOKEN_TILE, n), row),
        out_shape=jax.ShapeDtypeStruct((t, n), BF16),
        compiler_params=_params("parallel"),
        name="moba_proj",
    )(x, w_in, cos.reshape(t, LANES), sin.reshape(t, LANES))


def _hgrn_proj_kernel(x_ref, wq_ref, wf_ref, wig_ref, lb_ref, q_ref, logf_ref, ig_ref, *, layer):
    xb = x_ref[...].astype(BF16)
    raw = lb_ref[...]
    e = jnp.exp(raw - jnp.max(raw, axis=0, keepdims=True))
    soft = e / jnp.sum(e, axis=0, keepdims=True)
    lb = jnp.sum(soft[:layer + 1, :], axis=0, keepdims=True) - soft[0:1, :]
    for c0, cw in _col_chunks(wq_ref.shape[1]):
        cols = slice(c0, c0 + cw)
        q_ref[:, cols] = jnp.dot(xb, wq_ref[:, cols].astype(BF16), preferred_element_type=F32).astype(q_ref.dtype)
        f = jnp.dot(xb, wf_ref[:, cols].astype(BF16), preferred_element_type=F32)
        lbc = lb[:, cols]
        logf_ref[:, cols] = jnp.log(lbc + (1.0 - lbc) * _sigmoid(f))
    n_i = wig_ref.shape[1] // 2
    for c0, cw in _col_chunks(wig_ref.shape[1]):
        cols = slice(c0, c0 + cw)
        y = jnp.dot(xb, wig_ref[:, cols].astype(BF16), preferred_element_type=F32)
        ig_ref[:, cols] = (_silu(y) if c0 >= n_i else y).astype(ig_ref.dtype)


def _hgrn_proj(x, w_in, layer, lower_bounds, depth_index):
    t, k = x.shape
    nq, nf, n_ig = HGRN_COLS[0], HGRN_COLS[1], sum(HGRN_COLS[2:])
    assert nq == nf and n_ig == 2 * nq and nq % COL_CHUNK == 0
    row = lambda i: (i, 0)
    out = lambda n: pl.BlockSpec((TOKEN_TILE, n), row)
    return pl.pallas_call(
        functools.partial(_hgrn_proj_kernel, layer=depth_index),
        grid=(t // TOKEN_TILE,),
        in_specs=[pl.BlockSpec((TOKEN_TILE, k), row), _layer_window(layer, k, nq, 0), _layer_window(layer, k, nf, 1),
                  _layer_window(layer, k, n_ig, 1), _resident(lower_bounds.shape)],
        out_specs=[out(nq), out(nf), out(n_ig)],
        out_shape=[jax.ShapeDtypeStruct((t, nq), BF16), jax.ShapeDtypeStruct((t, nf), F32),
                   jax.ShapeDtypeStruct((t, n_ig), BF16)],
        compiler_params=_params("parallel"),
        name="hgrn_proj",
    )(x, w_in, w_in, w_in, lower_bounds)


def _tail_kernel(o_ref, wo_ref, x_ref, g1_ref, b1_ref, wgu_ref, wd_ref, g2_ref, b2_ref, y_ref, h_ref, *, hidden):
    tile = y_ref.shape[0]
    wo = wo_ref[...].astype(BF16)
    for r0 in range(0, tile, LN_SUBTILE):
        rows = slice(r0, r0 + LN_SUBTILE)
        y = jnp.dot(o_ref[rows, :], wo, preferred_element_type=F32)
        y_ref[rows, :] = _layer_norm(DEEPNORM_ALPHA * x_ref[rows, :] + y, g1_ref[...], b1_ref[...])
    xb = y_ref[...].astype(BF16)
    for c0, cw in _col_chunks(hidden, 256):
        gate = jnp.dot(xb, wgu_ref[:, c0:c0 + cw].astype(BF16), preferred_element_type=F32)
        up = jnp.dot(xb, wgu_ref[:, hidden + c0:hidden + c0 + cw].astype(BF16), preferred_element_type=F32)
        h_ref[:, c0:c0 + cw] = (_silu(gate) * up).astype(BF16)
    for r0 in range(0, tile, LN_SUBTILE):
        rows = slice(r0, r0 + LN_SUBTILE)
        down = jnp.concatenate([jnp.dot(h_ref[rows, :], wd_ref[:, c0:c0 + cw].astype(BF16),
                                        preferred_element_type=F32) for c0, cw in _col_chunks(y_ref.shape[1])], axis=1)
        y_ref[rows, :] = _layer_norm(DEEPNORM_ALPHA * y_ref[rows, :] + down, g2_ref[...], b2_ref[...])


def _layer_tail(o, w_out, out_layer, x, w_gu, w_down, layer, ln_g, ln_b):
    t, k = o.shape
    d = x.shape[1]
    hidden = w_down.shape[1]
    row = lambda i: (i, 0)
    vec = lambda a: a.reshape(1, d)
    return pl.pallas_call(
        functools.partial(_tail_kernel, hidden=hidden),
        grid=(t // TAIL_TILE,),
        in_specs=[pl.BlockSpec((TAIL_TILE, k), row), _layer_window(out_layer, k, d),
                  pl.BlockSpec((TAIL_TILE, d), row), _resident((1, d)), _resident((1, d)),
                  _layer_window(layer, d, 2 * hidden), _layer_window(layer, hidden, d),
                  _resident((1, d)), _resident((1, d))],
        out_specs=pl.BlockSpec((TAIL_TILE, d), row),
        out_shape=jax.ShapeDtypeStruct((t, d), F32),
        scratch_shapes=[pltpu.VMEM((TAIL_TILE, hidden), BF16)],
        compiler_params=_params("parallel"),
        name="layer_tail",
    )(o, w_out, x, vec(ln_g[layer, 0]), vec(ln_b[layer, 0]), w_gu, w_down, vec(ln_g[layer, 1]), vec(ln_b[layer, 1]))


def _split3(x):
    hi = x.astype(BF16)
    rest = x - hi.astype(F32)
    mid = rest.astype(BF16)
    return hi, mid, (rest - mid.astype(F32)).astype(BF16)


def _per_head(x, width):
    return jnp.stack([x[:, j * width:(j + 1) * width] for j in range(x.shape[1] // width)])


def _merge_heads(x):
    return jnp.concatenate([x[j] for j in range(x.shape[0])], axis=1)


def _gla_block(q, k, v, log_g, st):
    g, rows, dk = q.shape
    n = rows // CHUNK
    row = lax.broadcasted_iota(jnp.int32, (rows, rows), 0)
    col = lax.broadcasted_iota(jnp.int32, (rows, rows), 1)
    chunk_start = jnp.bitwise_and(row, -CHUNK)
    causal = jnp.logical_and(col <= row, col >= chunk_start)
    parts = [p for j in range(g) for p in _split3(log_g[j])]
    acc = jnp.dot(jnp.where(causal, 1.0, 0.0).astype(BF16), jnp.concatenate(parts, axis=1),
                  preferred_element_type=F32)
    cum = jnp.stack([acc[:, 3 * j * dk:(3 * j + 1) * dk] + acc[:, (3 * j + 1) * dk:(3 * j + 2) * dk]
                     + acc[:, (3 * j + 2) * dk:(3 * j + 3) * dk] for j in range(g)])
    lasts = [cum[:, (i + 1) * CHUNK - 1:(i + 1) * CHUNK, :] for i in range(n)]
    total = jnp.concatenate([jnp.broadcast_to(last, (g, CHUNK, dk)) for last in lasts], axis=1)
    q_dec = (q * jnp.exp(cum)).astype(BF16)
    k_inv = k * jnp.exp(-cum)
    k_end = (k * jnp.exp(total - cum)).astype(BF16)
    vb = v.astype(BF16)
    scores = jnp.where(causal, _bdot_nt(q_dec, k_inv), 0.0)
    o_intra = _bdot(scores, vb)
    o_inter = []
    for i in range(n):
        rs = slice(i * CHUNK, (i + 1) * CHUNK)
        o_inter.append(_bdot_nt(q_dec[:, rs, :], st))
        update = jnp.stack([_dot_tn(vb[j, rs, :], k_end[j, rs, :]) for j in range(g)])
        st = st * jnp.exp(lasts[i]) + update
    return o_intra + jnp.concatenate(o_inter, axis=1), st


def _gla_finish(o, norm_g, gate, o_ref, rows, activated=False):
    o = o * lax.rsqrt(jnp.mean(o * o, axis=-1, keepdims=True) + 1e-6) * norm_g
    gate = gate.astype(F32)
    o_ref[rows, :] = (_merge_heads(o) * (gate if activated else _silu(gate))).astype(o_ref.dtype)


def _gla_kernel(q_ref, k_ref, v_ref, g_ref, low_ref, wgk_ref, bgk_ref, ng_ref, o_ref, st_ref, *, n_blocks):
    st_ref[...] = jnp.zeros_like(st_ref)
    wgk = wgk_ref[...].astype(BF16)
    bgk = bgk_ref[...]
    norm_g = ng_ref[...]

    def body(c, carry):
        rows = pl.ds(pl.multiple_of(c * SUPER, SUPER), SUPER)
        logit = jnp.dot(low_ref[rows, :].astype(BF16), wgk, preferred_element_type=F32) + bgk
        log_g = _per_head(_log_sigmoid(logit) * (1.0 / GLA_GATE_NORMALIZER), GLA_DK)
        q = _per_head(q_ref[rows, :].astype(F32), GLA_DK) * (GLA_DK ** -0.5)
        o, st = _gla_block(q, _per_head(k_ref[rows, :].astype(F32), GLA_DK), _per_head(v_ref[rows, :], GLA_DV),
                           log_g, st_ref[...])
        st_ref[...] = st
        _gla_finish(o, norm_g, g_ref[rows, :], o_ref, rows)
        return carry

    lax.fori_loop(0, n_blocks, body, 0)


def _gla_mixer(proj, low, w_gk, b_gk, norm_g):
    bsz, s, _ = proj.shape
    dk, dv, h, gh = GLA_DK, GLA_DV, GLA_HEADS, MIX_GROUP
    groups = h // gh
    wgk_pad = jnp.zeros((LANES, h * dk), F32).at[:GLA_GATE_RANK].set(w_gk)
    seq = lambda width, off: pl.BlockSpec((None, s, gh * width), lambda b, i: (b, 0, off + i))
    return pl.pallas_call(
        functools.partial(_gla_kernel, n_blocks=s // SUPER),
        grid=(bsz, groups),
        in_specs=[seq(dk, 0), seq(dk, groups), seq(dv, (2 * h * dk) // (gh * dv)),
                  seq(dv, (2 * h * dk) // (gh * dv) + groups),
                  pl.BlockSpec((None, s, LANES), lambda b, i: (b, 0, 0)),
                  pl.BlockSpec((LANES, gh * dk), lambda b, i: (0, i)),
                  pl.BlockSpec((1, gh * dk), lambda b, i: (0, i)),
                  pl.BlockSpec((1, dv), lambda b, i: (0, 0))],
        out_specs=pl.BlockSpec((None, s, gh * dv), lambda b, i: (b, 0, i)),
        out_shape=jax.ShapeDtypeStruct((bsz, s, h * dv), BF16),
        scratch_shapes=[pltpu.VMEM((gh, dv, dk), F32)],
        compiler_params=_params("parallel", "parallel"),
        name="gla",
    )(proj, proj, proj, proj, low, wgk_pad, b_gk.reshape(1, -1), norm_g.reshape(1, -1))


def _hgrn_kernel(q_ref, logf_ref, i_ref, g_ref, ng_ref, o_ref, st_ref, *, n_blocks):
    st_ref[...] = jnp.zeros_like(st_ref)
    norm_g = ng_ref[...]

    def body(c, carry):
        rows = pl.ds(pl.multiple_of(c * SUPER, SUPER), SUPER)
        log_f = logf_ref[rows, :]
        o, st = _gla_block(_per_head(q_ref[rows, :].astype(F32), HEAD_DIM), _per_head(1.0 - jnp.exp(log_f), HEAD_DIM),
                           _per_head(i_ref[rows, :], HEAD_DIM), _per_head(log_f, HEAD_DIM), st_ref[...])
        st_ref[...] = st
        _gla_finish(o, norm_g, g_ref[rows, :], o_ref, rows, activated=True)
        return carry

    lax.fori_loop(0, n_blocks, body, 0)


def _hgrn_mixer(q, log_f, ig, norm_g):
    bsz, s, _ = q.shape
    d, h, gh = HEAD_DIM, HGRN_HEADS, HGRN_GROUP
    groups = h // gh
    seq = lambda off: pl.BlockSpec((None, s, gh * d), lambda b, i: (b, 0, off + i))
    return pl.pallas_call(
        functools.partial(_hgrn_kernel, n_blocks=s // SUPER),
        grid=(bsz, groups),
        in_specs=[seq(0), seq(0), seq(0), seq(groups), pl.BlockSpec((1, d), lambda b, i: (0, 0))],
        out_specs=pl.BlockSpec((None, s, gh * d), lambda b, i: (b, 0, i)),
        out_shape=jax.ShapeDtypeStruct((bsz, s, h * d), BF16),
        scratch_shapes=[pltpu.VMEM((gh, d, d), F32)],
        compiler_params=_params("parallel", "parallel"),
        name="hgrn2",
    )(q, log_f, ig, ig, norm_g.reshape(1, -1))


def _gdn_kernel(q_ref, k_ref, v_ref, z_ref, ab_ref, alog_ref, dtb_ref, ng_ref, o_ref, st_ref, *, n_chunks):
    rn, cs, gh, hd = GDN_ROWS, GDN_CHUNK, GDN_GROUP, HEAD_DIM
    n_sub = rn // cs
    head0 = pl.program_id(1) * gh
    row = lax.broadcasted_iota(jnp.int32, (cs, cs), 0)
    col = lax.broadcasted_iota(jnp.int32, (cs, cs), 1)
    incl, strict, eye = row >= col, row > col, row == col
    eye_f = jnp.where(eye, 1.0, 0.0)
    block_bits = jnp.bitwise_xor(row, col)
    big_row = lax.broadcasted_iota(jnp.int32, (rn, rn), 0)
    big_col = lax.broadcasted_iota(jnp.int32, (rn, rn), 1)
    tril_b = jnp.where(jnp.logical_and(big_col <= big_row, big_col >= jnp.bitwise_and(big_row, -cs)),
                       1.0, 0.0).astype(BF16)
    lane = lax.broadcasted_iota(jnp.int32, (1, LANES), 1)
    norm_g = ng_ref[...]
    neg_a = -jnp.exp(alog_ref[...])
    dt_bias = dtb_ref[...]
    st_ref[...] = jnp.zeros_like(st_ref)

    def per_chunk(x):
        width = x.shape[1] // gh
        return jnp.stack([x[s * cs:(s + 1) * cs, j * width:(j + 1) * width]
                          for j in range(gh) for s in range(n_sub)])

    def pick(x, first):
        return jnp.concatenate([jnp.broadcast_to(jnp.sum(jnp.where(lane == first + j, x, 0.0), axis=1,
                                                         keepdims=True), (rn, hd)) for j in range(gh)], axis=1)

    def of_sub(x, s):
        return jnp.stack([x[j * n_sub + s] for j in range(gh)])

    def body(c, carry):
        rows = pl.ds(pl.multiple_of(c * rn, rn), rn)
        q = per_chunk(q_ref[rows, :].astype(F32))
        k = per_chunk(k_ref[rows, :].astype(F32))
        v = per_chunk(v_ref[rows, :].astype(F32))
        ab = ab_ref[rows, :]
        g_wide = pick(neg_a * _softplus(ab + dt_bias), head0)
        beta = per_chunk(pick(_sigmoid(ab), head0 + GDN_HEADS))
        cum = per_chunk(sum(jnp.dot(tril_b, part, preferred_element_type=F32) for part in _split3(g_wide)))
        cum_row = jnp.sum(jnp.where(eye, cum, 0.0), axis=1, keepdims=True)
        grow = jnp.exp(jnp.where(incl, cum - cum_row, 0.0))
        k_beta = k * beta
        lower = jnp.where(strict, _bdot_nt(k_beta, k) * grow, 0.0)
        base_bits = int(np.log2(GDN_INV_BASE))
        power = -jnp.where(jnp.right_shift(block_bits, base_bits) == 0, lower, 0.0)
        inv = eye_f + power
        for _ in range(base_bits - 1):
            power = _bdot(power, power)
            inv = inv + _bdot(inv, power)
        for bits in range(base_bits, int(np.log2(cs))):
            cross = jnp.where(jnp.right_shift(block_bits, bits) == 1, lower, 0.0)
            inv = inv - _bdot(_bdot(inv, cross), inv)
        e_cum = jnp.exp(cum)
        sol = _bdot(inv, jnp.concatenate([v * beta, k_beta * e_cum], axis=2))
        u, w_qd = sol[:, :, :hd], jnp.concatenate([sol[:, :, hd:], q * e_cum], axis=1)
        qk = jnp.where(incl, _bdot_nt(q, k) * grow, 0.0)
        last = cum[:, cs - 1:cs, :]
        k_end = k * jnp.exp(last - cum)
        carry_decay = jnp.exp(last)
        st = st_ref[...]
        outs = []
        for s in range(n_sub):
            t = _bdot(of_sub(w_qd, s), st)
            v_new = of_sub(u, s) - t[:, :cs, :]
            outs.append(t[:, cs:, :] + _bdot(of_sub(qk, s), v_new))
            ke = of_sub(k_end, s)
            st = st * of_sub(carry_decay, s) + jnp.stack([_dot_tn(ke[j], v_new[j]) for j in range(gh)])
        st_ref[...] = st
        o = jnp.concatenate(outs, axis=1)
        o = o * lax.rsqrt(jnp.mean(o * o, axis=-1, keepdims=True) + 1e-6) * norm_g
        o_ref[rows, :] = (_merge_heads(o) * z_ref[rows, :].astype(F32)).astype(o_ref.dtype)
        return carry

    lax.fori_loop(0, n_chunks, body, 0)


def _gdn_mixer(proj, ab, a_log, dt_bias, norm_g):
    bsz, s, _ = proj.shape
    d, h = HEAD_DIM, GDN_HEADS
    assert GDN_CHUNK == HEAD_DIM and s % GDN_ROWS == 0
    n_chunks = s // GDN_ROWS
    gh = GDN_GROUP
    groups = h // gh
    seq = lambda off: pl.BlockSpec((None, s, gh * d), lambda b, i: (b, 0, off + i))
    vec = pl.BlockSpec((1, LANES), lambda b, i: (0, 0))
    pad = lambda a: jnp.zeros((1, LANES), F32).at[0, :h].set(a)
    return pl.pallas_call(
        functools.partial(_gdn_kernel, n_chunks=n_chunks),
        grid=(bsz, groups),
        in_specs=[seq(0), seq(groups), seq(2 * groups), seq(3 * groups),
                  pl.BlockSpec((None, s, LANES), lambda b, i: (b, 0, 0)), vec, vec, vec],
        out_specs=pl.BlockSpec((None, s, gh * d), lambda b, i: (b, 0, i)),
        out_shape=jax.ShapeDtypeStruct((bsz, s, h * d), BF16),
        scratch_shapes=[pltpu.VMEM((gh, d, d), F32)],
        compiler_params=_params("parallel", "parallel"),
        name="gdn",
    )(proj, proj, proj, proj, ab, pad(a_log), pad(dt_bias), norm_g.reshape(1, -1))


def _rope_table_kernel(pos_ref, invf_ref, cos_ref, sin_ref):
    half = ROPE_DIM // 2
    rest = LANES - ROPE_DIM
    for c0 in range(0, pos_ref.shape[1], LANES):
        ang = invf_ref[...] * pos_ref[:, c0:c0 + LANES].astype(F32)
        c, s = jnp.cos(ang), jnp.sin(ang)
        cos_t = jnp.concatenate([c, c, jnp.ones((rest, LANES), F32)], axis=0)
        sin_t = jnp.concatenate([-s, s, jnp.zeros((rest, LANES), F32)], axis=0)
        cos_ref[c0:c0 + LANES, :] = cos_t.T
        sin_ref[c0:c0 + LANES, :] = sin_t.T


def _rope_tables(positions):
    bsz, s = positions.shape
    half = ROPE_DIM // 2
    inv_freq = (np.float32(ROPE_THETA) ** (-np.arange(0, ROPE_DIM, 2, dtype=np.float32) / ROPE_DIM)).astype(np.float32)
    invf = np.ascontiguousarray(np.broadcast_to(inv_freq[:, None], (half, LANES)))
    out = jax.ShapeDtypeStruct((bsz, s, LANES), F32)
    return pl.pallas_call(
        _rope_table_kernel,
        grid=(bsz,),
        in_specs=[pl.BlockSpec((None, 1, s), lambda b: (b, 0, 0)),
                  pl.BlockSpec((half, LANES), lambda b: (0, 0))],
        out_specs=[pl.BlockSpec((None, s, LANES), lambda b: (b, 0, 0))] * 2,
        out_shape=[out, out],
        compiler_params=_params("parallel"),
        name="rope_table",
    )(positions.reshape(bsz, 1, s), jnp.asarray(invf))


def _moba_kernel(q_ref, k_ref, v_ref, o_ref, kr_s, vb_s, km_s, s_s, *, n_blocks):
    blk, hd, gh = MOBA_BLOCK, HEAD_DIM, MOBA_GROUP
    causal, _, _ = _tri_masks(blk)
    lane = lax.broadcasted_iota(jnp.int32, (1, hd), 1)
    blk_id = lax.broadcasted_iota(jnp.int32, (1, n_blocks, 1), 1)
    to_lanes = jnp.where(lax.broadcasted_iota(jnp.int32, (n_blocks, hd), 0)
                         == lax.broadcasted_iota(jnp.int32, (n_blocks, hd), 1), 1.0, 0.0).astype(BF16)

    for j in range(n_blocks):
        rows = slice(j * blk, (j + 1) * blk)
        kr = _per_head(k_ref[rows, :], hd)
        kr_s[:, rows, :hd] = kr
        kr_s[:, rows, hd:] = jnp.broadcast_to(jnp.where(lane == j, 1.0, 0.0), (gh, blk, hd)).astype(BF16)
        km_s[:, j:j + 1, :] = jnp.mean(kr.astype(F32), axis=1, keepdims=True)
        vb_s[:, rows, :hd] = _per_head(v_ref[rows, :], hd).astype(BF16)
        vb_s[:, rows, hd:] = jnp.ones((gh, blk, hd), BF16)

    for qb in range(n_blocks):
        rows = slice(qb * blk, (qb + 1) * blk)
        qb16 = _per_head(q_ref[rows, :], hd)
        bias = jnp.zeros((gh, blk, hd), BF16)
        if qb > 0:
            gate = jnp.where(blk_id < qb, _bdot_nt_f32(km_s[...], qb16.astype(F32)), NEG_INF)
            picked = []
            for j in range(qb):
                gj = gate[:, j:j + 1, :]
                beats = jnp.where(blk_id < j, jnp.where(gate >= gj, 1.0, 0.0), jnp.where(gate > gj, 1.0, 0.0))
                rank = jnp.sum(beats, axis=1, keepdims=True)
                picked.append(jnp.where(rank < MOBA_TOPK, 1.0, 0.0))
            picked.append(jnp.zeros((gh, n_blocks - qb, blk), F32))
            picked = jnp.concatenate(picked, axis=1)
            keep = jnp.stack([_dot_tn(picked[g], to_lanes) for g in range(gh)])
            bias = jnp.where(lane < qb, (1.0 - keep) * NEG_INF, 0.0).astype(BF16)
        width = (qb + 1) * blk
        logits = _bdot_nt(jnp.concatenate([qb16, bias], axis=2), kr_s[:, :width, :])
        if qb > 0:
            s_s[:, :, :qb * blk] = logits[:, :, :qb * blk]
        s_s[:, :, rows] = jnp.where(causal, logits[:, :, qb * blk:], NEG_INF)
        top = s_s[:, :, :LANES]
        for c0 in range(LANES, width, LANES):
            top = jnp.maximum(top, s_s[:, :, c0:c0 + LANES])
        p = jnp.exp2(s_s[:, :, :width] - jnp.max(top, axis=2, keepdims=True))
        pv = _bdot(p, vb_s[:, :width, :])
        o_ref[rows, :] = _merge_heads(pv[:, :, :hd] / pv[:, :, hd:]).astype(o_ref.dtype)


def _moba_mixer(proj):
    bsz, s, _ = proj.shape
    d, h, gh = HEAD_DIM, MOBA_HEADS, MOBA_GROUP
    groups = h // gh
    seq = lambda off: pl.BlockSpec((None, s, gh * d), lambda b, i: (b, 0, off + i))
    return pl.pallas_call(
        functools.partial(_moba_kernel, n_blocks=s // MOBA_BLOCK),
        grid=(bsz, groups),
        in_specs=[seq(0), seq(groups), seq(2 * groups)],
        out_specs=pl.BlockSpec((None, s, gh * d), lambda b, i: (b, 0, i)),
        out_shape=jax.ShapeDtypeStruct((bsz, s, h * d), BF16),
        scratch_shapes=[pltpu.VMEM((gh, s, 2 * d), BF16),
                        pltpu.VMEM((gh, s, 2 * d), BF16),
                        pltpu.VMEM((gh, s // MOBA_BLOCK, d), F32),
                        pltpu.VMEM((gh, MOBA_BLOCK, s), F32)],
        compiler_params=_params("parallel", "parallel"),
        name="moba",
    )(proj, proj, proj)


def _pad_cols(w, multiple=LANES):
    n = w.shape[1]
    return jnp.pad(w, ((0, 0), (0, -n % multiple)))


def kernel(x, positions, gla_w_in, gla_w_gk, gla_b_gk, gla_norm_g, gla_w_out, moba_w_in, moba_w_out,
           gdn_w_in, gdn_conv_w, gdn_a_log, gdn_dt_bias, gdn_norm_g, gdn_w_out, hgrn_lower_bounds,
           hgrn_w_in, hgrn_norm_g, hgrn_w_out, ffn_w_gu, ffn_w_down, ln_g, ln_b):
    bsz, s, d = x.shape
    assert (s % MOBA_BLOCK, (bsz * s) % TOKEN_TILE, d) == (0, 0, D_MODEL)
    t = bsz * s
    xf = x.reshape(t, d)
    cos = sin = None
    for i in range(DEPTH):
        kind, j = i % N_MIXERS, i // N_MIXERS
        tokens = lambda a: a.reshape(bsz, s, -1)
        if kind == 0:
            n_main = sum(GLA_COLS[:4])
            low_w = _pad_cols(gla_w_in[j, :, n_main:])[None]
            proj, low = _proj(xf, [(gla_w_in, j, 0, n_main, BF16), (low_w, 0, 0, low_w.shape[2], F32)])
            o = _gla_mixer(tokens(proj), tokens(low), gla_w_gk[j], gla_b_gk[j], gla_norm_g[j])
            w_out = gla_w_out
        elif kind == 1:
            if cos is None:
                cos, sin = _rope_tables(positions)
            o = _moba_mixer(tokens(_moba_proj(xf, moba_w_in, j, cos, sin)))
            w_out = moba_w_out
        elif kind == 2:
            n_main = sum(GDN_COLS[:2])
            ab_w = _pad_cols(gdn_w_in[j, :, n_main:])[None]
            proj, ab = _gdn_proj(xf, gdn_w_in, j, ab_w, gdn_conv_w, s)
            o = _gdn_mixer(tokens(proj), tokens(ab), gdn_a_log[j], gdn_dt_bias[j], gdn_norm_g[j])
            w_out = gdn_w_out
        else:
            q, log_f, ig = _hgrn_proj(xf, hgrn_w_in, j, hgrn_lower_bounds, i)
            o = _hgrn_mixer(tokens(q), tokens(log_f), tokens(ig), hgrn_norm_g[j])
            w_out = hgrn_w_out
        xf = _layer_tail(o.reshape(t, d), w_out, j, xf, ffn_w_gu, ffn_w_down, i, ln_g, ln_b)
    return xf.reshape(bsz, s, d)
```

```python
import functools

import numpy as np
import jax
import jax.numpy as jnp
from jax import lax
from jax.experimental import pallas as pl
from jax.experimental.pallas import tpu as pltpu

F32 = jnp.float32
BF16 = jnp.bfloat16
HIGHEST = lax.Precision.HIGHEST

D_MODEL = 1024
DEPTH = 4
N_MIXERS = 4
HEAD_DIM = 128
CHUNK = 64
SUPER = 256
MIX_GROUP = 4
HGRN_GROUP = 8
MOBA_GROUP = 4
GLA_HEADS = 4
GLA_DK = 128
GLA_DV = 256
GLA_GATE_RANK = 16
GLA_GATE_NORMALIZER = 16.0
MOBA_HEADS = 8
MOBA_BLOCK = 256
MOBA_TOPK = 3
ROPE_THETA = 500000.0
ROPE_DIM = 32
NEG_INF = -1e30
LOG2_E = 1.4426950408889634
GDN_HEADS = 8
GDN_CONV = 4
GDN_ROWS = 256
GDN_CHUNK = 128
GDN_INV_BASE = 16
GDN_GROUP = 8
GLA_COLS = (512, 512, 1024, 1024, 16)
GDN_COLS = (3072, 1024, 8, 8)
HGRN_COLS = (1024, 1024, 1024, 1024)
HGRN_HEADS = 8
FFN_HIDDEN = 2816
DEEPNORM_ALPHA = (2.0 * DEPTH) ** 0.25

LANES = 128
SUBLANES = 8
BF16_ROWS = 16
VMEM_LIMIT = 56 * 1024 * 1024
TOKEN_TILE = 1024
COL_CHUNK = 512
GDN_PROJ_CHUNK = 256
TAIL_TILE = 512
LN_SUBTILE = 256


def _dot_tn(a, b):
    return lax.dot_general(a.astype(BF16), b.astype(BF16), (((0,), (0,)), ((), ())),
                           preferred_element_type=F32)


def _bdot(a, b):
    return lax.dot_general(a.astype(BF16), b.astype(BF16), (((2,), (1,)), ((0,), (0,))),
                           preferred_element_type=F32)


def _bdot_nt(a, b):
    return lax.dot_general(a.astype(BF16), b.astype(BF16), (((2,), (2,)), ((0,), (0,))),
                           preferred_element_type=F32)


def _bdot_nt_f32(a, b):
    return lax.dot_general(a, b, (((2,), (2,)), ((0,), (0,))), precision=HIGHEST,
                           preferred_element_type=F32)


def _sigmoid(x):
    return 1.0 / (1.0 + jnp.exp(-x))


def _silu(x):
    return x * _sigmoid(x)


def _softplus(x):
    return jnp.maximum(x, 0.0) + jnp.log1p(jnp.exp(-jnp.abs(x)))


def _log_sigmoid(x):
    return -_softplus(-x)


def _col_chunks(n, width=COL_CHUNK):
    return tuple((c0, min(width, n - c0)) for c0 in range(0, n, width))


def _params(*semantics):
    return pltpu.CompilerParams(dimension_semantics=semantics, vmem_limit_bytes=VMEM_LIMIT)


def _resident(shape):
    zeros = (0,) * len(shape)
    return pl.BlockSpec(shape, lambda *_: zeros, pipeline_mode=pl.Buffered(1))


def _layer_norm(z, g, b):
    mu = jnp.mean(z, axis=-1, keepdims=True)
    zc = z - mu
    var = jnp.mean(zc * zc, axis=-1, keepdims=True)
    return zc * lax.rsqrt(var + 1e-5) * g + b


def _tri_masks(n):
    row = lax.broadcasted_iota(jnp.int32, (n, n), 0)
    col = lax.broadcasted_iota(jnp.int32, (n, n), 1)
    return row >= col, row > col, row == col


def _layer_window(layer, k, n, col_block=0):
    return pl.BlockSpec((None, k, n), lambda *_: (layer, 0, col_block), pipeline_mode=pl.Buffered(1))


def _gdn_proj_kernel(x_ref, w_ref, wab_ref, cw_ref, o_ref, ab_ref, carry_ref, *, tiles_per_seq):
    i = pl.program_id(0)
    tile = x_ref.shape[0]
    hist = BF16_ROWS
    n_conv = cw_ref.shape[1]
    n_norm = 2 * n_conv // 3
    first = (i % tiles_per_seq) == 0

    @pl.when(i == 0)
    def _():
        carry_ref[...] = jnp.zeros_like(carry_ref)

    xb = x_ref[...].astype(BF16)
    for c0, cw in _col_chunks(w_ref.shape[1], GDN_PROJ_CHUNK):
        cols = slice(c0, c0 + cw)
        y = jnp.dot(xb, w_ref[:, cols].astype(BF16), preferred_element_type=F32)
        if c0 < n_conv:
            prev = jnp.where(first, 0.0, carry_ref[:, cols])
            carry_ref[:, cols] = y[tile - hist:, :]
            ext = jnp.concatenate([prev, y], axis=0)
            w = cw_ref[:, cols]
            y = y * w[GDN_CONV - 1:GDN_CONV, :]
            for j in range(GDN_CONV - 1):
                lo = hist - (GDN_CONV - 1) + j
                y = y + ext[lo:lo + tile, :] * w[j:j + 1, :]
        y = _silu(y)
        if c0 < n_norm:
            heads = _per_head(y, HEAD_DIM)
            heads = heads * lax.rsqrt(jnp.sum(heads * heads, axis=-1, keepdims=True) + 1e-6)
            y = _merge_heads(heads * (HEAD_DIM ** -0.5) if c0 < n_norm // 2 else heads)
        o_ref[:, cols] = y.astype(o_ref.dtype)
    ab_ref[...] = jnp.dot(xb, wab_ref[...].astype(BF16), preferred_element_type=F32)


def _gdn_proj(x, w_in, layer, ab_w, conv_w, seq_len):
    t, k = x.shape
    n_main = sum(GDN_COLS[:2])
    n_conv = GDN_COLS[0]
    assert n_conv % GDN_PROJ_CHUNK == 0 and (n_conv // 3) % GDN_PROJ_CHUNK == 0 and seq_len % TOKEN_TILE == 0
    row = lambda i: (i, 0)
    return pl.pallas_call(
        functools.partial(_gdn_proj_kernel, tiles_per_seq=seq_len // TOKEN_TILE),
        grid=(t // TOKEN_TILE,),
        in_specs=[pl.BlockSpec((TOKEN_TILE, k), row), _layer_window(layer, k, n_main),
                  _layer_window(0, k, ab_w.shape[2]), _layer_window(layer, GDN_CONV, n_conv)],
        out_specs=[pl.BlockSpec((TOKEN_TILE, n_main), row), pl.BlockSpec((TOKEN_TILE, ab_w.shape[2]), row)],
        out_shape=[jax.ShapeDtypeStruct((t, n_main), BF16), jax.ShapeDtypeStruct((t, ab_w.shape[2]), F32)],
        scratch_shapes=[pltpu.VMEM((BF16_ROWS, n_conv), F32)],
        compiler_params=_params("arbitrary"),
        name="gdn_proj",
    )(x, w_in, ab_w, conv_w)


def _moba_proj_kernel(x_ref, w_ref, cos_ref, sin_ref, o_ref):
    xb = x_ref[...].astype(BF16)
    half = ROPE_DIM // 2
    n_qk = 2 * (w_ref.shape[1] // 3)
    for c0, cw in _col_chunks(w_ref.shape[1]):
        cols = slice(c0, c0 + cw)
        y = jnp.dot(xb, w_ref[:, cols].astype(BF16), preferred_element_type=F32)
        if c0 < n_qk:
            head_lane = jnp.bitwise_and(lax.broadcasted_iota(jnp.int32, (1, cw), 1), HEAD_DIM - 1)
            partner = jnp.where(head_lane < half, pltpu.roll(y, cw - half, 1), pltpu.roll(y, half, 1))
            reps = cw // HEAD_DIM
            y = (y * jnp.concatenate([cos_ref[...]] * reps, axis=1)
                 + partner * jnp.concatenate([sin_ref[...]] * reps, axis=1))
            if c0 < n_qk // 2:
                y = y * (HEAD_DIM ** -0.5 * LOG2_E)
        o_ref[:, cols] = y.astype(o_ref.dtype)


def _moba_proj(x, w_in, layer, cos, sin):
    t, k = x.shape
    n = w_in.shape[2]
    assert (n // 3) % COL_CHUNK == 0
    row = lambda i: (i, 0)
    tab = pl.BlockSpec((TOKEN_TILE, LANES), row)
    return pl.pallas_call(
        _moba_proj_kernel,
        grid=(t // TOKEN_TILE,),
        in_specs=[pl.BlockSpec((TOKEN_TILE, k), row), _layer_window(layer, k, n), tab, tab],
        out_specs=pl.BlockSpec((TOKEN_TILE, n), row),
        out_shape=jax.ShapeDtypeStruct((t, n), BF16),
        compiler_params=_params("parallel"),
        name="moba_proj",
    )(x, w_in, cos.reshape(t, LANES), sin.reshape(t, LANES))


def _hgrn_proj_kernel(x_ref, wq_ref, wf_ref, wig_ref, lb_ref, q_ref, logf_ref, ig_ref, *, layer):
    xb = x_ref[...].astype(BF16)
    raw = lb_ref[...]
    e = jnp.exp(raw - jnp.max(raw, axis=0, keepdims=True))
    soft = e / jnp.sum(e, axis=0, keepdims=True)
    lb = jnp.sum(soft[:layer + 1, :], axis=0, keepdims=True) - soft[0:1, :]
    for c0, cw in _col_chunks(wq_ref.shape[1]):
        cols = slice(c0, c0 + cw)
        q_ref[:, cols] = jnp.dot(xb, wq_ref[:, cols].astype(BF16), preferred_element_type=F32).astype(q_ref.dtype)
        f = jnp.dot(xb, wf_ref[:, cols].astype(BF16), preferred_element_type=F32)
        lbc = lb[:, cols]
        logf_ref[:, cols] = jnp.log(lbc + (1.0 - lbc) * _sigmoid(f))
    n_i = wig_ref.shape[1] // 2
    for c0, cw in _col_chunks(wig_ref.shape[1]):
        cols = slice(c0, c0 + cw)
        y = jnp.dot(xb, wig_ref[:, cols].astype(BF16), preferred_element_type=F32)
        ig_ref[:, cols] = (_silu(y) if c0 >= n_i else y).astype(ig_ref.dtype)


def _hgrn_proj(x, w_in, layer, lower_bounds, depth_index):
    t, k = x.shape
    nq, nf, n_ig = HGRN_COLS[0], HGRN_COLS[1], sum(HGRN_COLS[2:])
    assert nq == nf and n_ig == 2 * nq and nq % COL_CHUNK == 0
    row = lambda i: (i, 0)
    out = lambda n: pl.BlockSpec((TOKEN_TILE, n), row)
    return pl.pallas_call(
        functools.partial(_hgrn_proj_kernel, layer=depth_index),
        grid=(t // TOKEN_TILE,),
        in_specs=[pl.BlockSpec((TOKEN_TILE, k), row), _layer_window(layer, k, nq, 0), _layer_window(layer, k, nf, 1),
                  _layer_window(layer, k, n_ig, 1), _resident(lower_bounds.shape)],
        out_specs=[out(nq), out(nf), out(n_ig)],
        out_shape=[jax.ShapeDtypeStruct((t, nq), BF16), jax.ShapeDtypeStruct((t, nf), F32),
                   jax.ShapeDtypeStruct((t, n_ig), BF16)],
        compiler_params=_params("parallel"),
        name="hgrn_proj",
    )(x, w_in, w_in, w_in, lower_bounds)


def _tail_kernel(o_ref, wo_ref, x_ref, g1_ref, b1_ref, wgu_ref, wd_ref, g2_ref, b2_ref, y_ref, h_ref, *, hidden):
    tile = y_ref.shape[0]
    wo = wo_ref[...].astype(BF16)
    for r0 in range(0, tile, LN_SUBTILE):
        rows = slice(r0, r0 + LN_SUBTILE)
        y = jnp.dot(o_ref[rows, :], wo, preferred_element_type=F32)
        y_ref[rows, :] = _layer_norm(DEEPNORM_ALPHA * x_ref[rows, :] + y, g1_ref[...], b1_ref[...])
    xb = y_ref[...].astype(BF16)
    for c0, cw in _col_chunks(hidden, 256):
        gate = jnp.dot(xb, wgu_ref[:, c0:c0 + cw].astype(BF16), preferred_element_type=F32)
        up = jnp.dot(xb, wgu_ref[:, hidden + c0:hidden + c0 + cw].astype(BF16), preferred_element_type=F32)
        h_ref[:, c0:c0 + cw] = (_silu(gate) * up).astype(BF16)
    for r0 in range(0, tile, LN_SUBTILE):
        rows = slice(r0, r0 + LN_SUBTILE)
        down = jnp.concatenate([jnp.dot(h_ref[rows, :], wd_ref[:, c0:c0 + cw].astype(BF16),
                                        preferred_element_type=F32) for c0, cw in _col_chunks(y_ref.shape[1])], axis=1)
        y_ref[rows, :] = _layer_norm(DEEPNORM_ALPHA * y_ref[rows, :] + down, g2_ref[...], b2_ref[...])


def _layer_tail(o, w_out, out_layer, x, w_gu, w_down, layer, ln_g, ln_b):
    t, k = o.shape
    d = x.shape[1]
    hidden = w_down.shape[1]
    row = lambda i: (i, 0)
    vec = lambda a: a.reshape(1, d)
    return pl.pallas_call(
        functools.partial(_tail_kernel, hidden=hidden),
        grid=(t // TAIL_TILE,),
        in_specs=[pl.BlockSpec((TAIL_TILE, k), row), _layer_window(out_layer, k, d),
                  pl.BlockSpec((TAIL_TILE, d), row), _resident((1, d)), _resident((1, d)),
                  _layer_window(layer, d, 2 * hidden), _layer_window(layer, hidden, d),
                  _resident((1, d)), _resident((1, d))],
        out_specs=pl.BlockSpec((TAIL_TILE, d), row),
        out_shape=jax.ShapeDtypeStruct((t, d), F32),
        scratch_shapes=[pltpu.VMEM((TAIL_TILE, hidden), BF16)],
        compiler_params=_params("parallel"),
        name="layer_tail",
    )(o, w_out, x, vec(ln_g[layer, 0]), vec(ln_b[layer, 0]), w_gu, w_down, vec(ln_g[layer, 1]), vec(ln_b[layer, 1]))


def _split3(x):
    hi = x.astype(BF16)
    rest = x - hi.astype(F32)
    mid = rest.astype(BF16)
    return hi, mid, (rest - mid.astype(F32)).astype(BF16)


def _per_head(x, width):
    return jnp.stack([x[:, j * width:(j + 1) * width] for j in range(x.shape[1] // width)])


def _merge_heads(x):
    return jnp.concatenate([x[j] for j in range(x.shape[0])], axis=1)


def _gla_block(q, k, v, log_g, st):
    g, rows, dk = q.shape
    n = rows // CHUNK
    row = lax.broadcasted_iota(jnp.int32, (rows, rows), 0)
    col = lax.broadcasted_iota(jnp.int32, (rows, rows), 1)
    chunk_start = jnp.bitwise_and(row, -CHUNK)
    causal = jnp.logical_and(col <= row, col >= chunk_start)
    parts = [p for j in range(g) for p in _split3(log_g[j])]
    acc = jnp.dot(jnp.where(causal, 1.0, 0.0).astype(BF16), jnp.concatenate(parts, axis=1),
                  preferred_element_type=F32)
    cum = jnp.stack([acc[:, 3 * j * dk:(3 * j + 1) * dk] + acc[:, (3 * j + 1) * dk:(3 * j + 2) * dk]
                     + acc[:, (3 * j + 2) * dk:(3 * j + 3) * dk] for j in range(g)])
    lasts = [cum[:, (i + 1) * CHUNK - 1:(i + 1) * CHUNK, :] for i in range(n)]
    total = jnp.concatenate([jnp.broadcast_to(last, (g, CHUNK, dk)) for last in lasts], axis=1)
    q_dec = (q * jnp.exp(cum)).astype(BF16)
    k_inv = k * jnp.exp(-cum)
    k_end = (k * jnp.exp(total - cum)).astype(BF16)
    vb = v.astype(BF16)
    scores = jnp.where(causal, _bdot_nt(q_dec, k_inv), 0.0)
    o_intra = _bdot(scores, vb)
    o_inter = []
    for i in range(n):
        rs = slice(i * CHUNK, (i + 1) * CHUNK)
        o_inter.append(_bdot_nt(q_dec[:, rs, :], st))
        update = jnp.stack([_dot_tn(vb[j, rs, :], k_end[j, rs, :]) for j in range(g)])
        st = st * jnp.exp(lasts[i]) + update
    return o_intra + jnp.concatenate(o_inter, axis=1), st


def _gla_finish(o, norm_g, gate, o_ref, rows):
    o = o * lax.rsqrt(jnp.mean(o * o, axis=-1, keepdims=True) + 1e-6) * norm_g
    o_ref[rows, :] = (_merge_heads(o) * gate.astype(F32)).astype(o_ref.dtype)


def _gla_proj_kernel(x_ref, w_ref, wlow_ref, wgk_ref, bgk_ref, o_ref, logg_ref):
    xb = x_ref[...].astype(BF16)
    n_q = GLA_COLS[0]
    n_qkv = sum(GLA_COLS[:3])
    for c0, cw in _col_chunks(w_ref.shape[1]):
        cols = slice(c0, c0 + cw)
        y = jnp.dot(xb, w_ref[:, cols].astype(BF16), preferred_element_type=F32)
        if c0 < n_q:
            y = y * (GLA_DK ** -0.5)
        elif c0 >= n_qkv:
            y = _silu(y)
        o_ref[:, cols] = y.astype(o_ref.dtype)
    low = jnp.dot(xb, wlow_ref[...].astype(BF16), preferred_element_type=F32)
    logit = jnp.dot(low.astype(BF16), wgk_ref[...].astype(BF16), preferred_element_type=F32) + bgk_ref[...]
    logg_ref[...] = _log_sigmoid(logit) * (1.0 / GLA_GATE_NORMALIZER)


def _gla_proj(x, w_in, layer, low_w, w_gk, b_gk):
    t, k = x.shape
    n_main = sum(GLA_COLS[:4])
    n_gate = w_gk.shape[1]
    assert GLA_COLS[0] % COL_CHUNK == 0 and sum(GLA_COLS[:3]) % COL_CHUNK == 0
    wgk_pad = jnp.zeros((low_w.shape[2], n_gate), F32).at[:GLA_GATE_RANK].set(w_gk)
    row = lambda i: (i, 0)
    return pl.pallas_call(
        _gla_proj_kernel,
        grid=(t // TOKEN_TILE,),
        in_specs=[pl.BlockSpec((TOKEN_TILE, k), row), _layer_window(layer, k, n_main),
                  _layer_window(0, k, low_w.shape[2]), _resident(wgk_pad.shape), _resident((1, n_gate))],
        out_specs=[pl.BlockSpec((TOKEN_TILE, n_main), row), pl.BlockSpec((TOKEN_TILE, n_gate), row)],
        out_shape=[jax.ShapeDtypeStruct((t, n_main), BF16), jax.ShapeDtypeStruct((t, n_gate), F32)],
        compiler_params=_params("parallel"),
        name="gla_proj",
    )(x, w_in, low_w, wgk_pad, b_gk.reshape(1, n_gate))


def _gla_kernel(q_ref, k_ref, v_ref, g_ref, logg_ref, ng_ref, o_ref, st_ref, *, n_blocks):
    st_ref[...] = jnp.zeros_like(st_ref)
    norm_g = ng_ref[...]

    def body(c, carry):
        rows = pl.ds(pl.multiple_of(c * SUPER, SUPER), SUPER)
        o, st = _gla_block(_per_head(q_ref[rows, :].astype(F32), GLA_DK), _per_head(k_ref[rows, :].astype(F32), GLA_DK),
                           _per_head(v_ref[rows, :], GLA_DV), _per_head(logg_ref[rows, :], GLA_DK), st_ref[...])
        st_ref[...] = st
        _gla_finish(o, norm_g, g_ref[rows, :], o_ref, rows)
        return carry

    lax.fori_loop(0, n_blocks, body, 0)


def _gla_mixer(proj, log_g, norm_g):
    bsz, s, _ = proj.shape
    dk, dv, h, gh = GLA_DK, GLA_DV, GLA_HEADS, MIX_GROUP
    groups = h // gh
    seq = lambda width, off: pl.BlockSpec((None, s, gh * width), lambda b, i: (b, 0, off + i))
    return pl.pallas_call(
        functools.partial(_gla_kernel, n_blocks=s // SUPER),
        grid=(bsz, groups),
        in_specs=[seq(dk, 0), seq(dk, groups), seq(dv, (2 * h * dk) // (gh * dv)),
                  seq(dv, (2 * h * dk) // (gh * dv) + groups), seq(dk, 0),
                  pl.BlockSpec((1, dv), lambda b, i: (0, 0))],
        out_specs=pl.BlockSpec((None, s, gh * dv), lambda b, i: (b, 0, i)),
        out_shape=jax.ShapeDtypeStruct((bsz, s, h * dv), BF16),
        scratch_shapes=[pltpu.VMEM((gh, dv, dk), F32)],
        compiler_params=_params("parallel", "parallel"),
        name="gla",
    )(proj, proj, proj, proj, log_g, norm_g.reshape(1, -1))


def _hgrn_kernel(q_ref, logf_ref, i_ref, g_ref, ng_ref, o_ref, st_ref, *, n_blocks):
    st_ref[...] = jnp.zeros_like(st_ref)
    norm_g = ng_ref[...]

    def body(c, carry):
        rows = pl.ds(pl.multiple_of(c * SUPER, SUPER), SUPER)
        log_f = logf_ref[rows, :]
        o, st = _gla_block(_per_head(q_ref[rows, :].astype(F32), HEAD_DIM), _per_head(1.0 - jnp.exp(log_f), HEAD_DIM),
                           _per_head(i_ref[rows, :], HEAD_DIM), _per_head(log_f, HEAD_DIM), st_ref[...])
        st_ref[...] = st
        _gla_finish(o, norm_g, g_ref[rows, :], o_ref, rows)
        return carry

    lax.fori_loop(0, n_blocks, body, 0)


def _hgrn_mixer(q, log_f, ig, norm_g):
    bsz, s, _ = q.shape
    d, h, gh = HEAD_DIM, HGRN_HEADS, HGRN_GROUP
    groups = h // gh
    seq = lambda off: pl.BlockSpec((None, s, gh * d), lambda b, i: (b, 0, off + i))
    return pl.pallas_call(
        functools.partial(_hgrn_kernel, n_blocks=s // SUPER),
        grid=(bsz, groups),
        in_specs=[seq(0), seq(0), seq(0), seq(groups), pl.BlockSpec((1, d), lambda b, i: (0, 0))],
        out_specs=pl.BlockSpec((None, s, gh * d), lambda b, i: (b, 0, i)),
        out_shape=jax.ShapeDtypeStruct((bsz, s, h * d), BF16),
        scratch_shapes=[pltpu.VMEM((gh, d, d), F32)],
        compiler_params=_params("parallel", "parallel"),
        name="hgrn2",
    )(q, log_f, ig, ig, norm_g.reshape(1, -1))


def _gdn_kernel(q_ref, k_ref, v_ref, z_ref, ab_ref, alog_ref, dtb_ref, ng_ref, o_ref, st_ref, *, n_chunks):
    rn, cs, gh, hd = GDN_ROWS, GDN_CHUNK, GDN_GROUP, HEAD_DIM
    n_sub = rn // cs
    head0 = pl.program_id(1) * gh
    row = lax.broadcasted_iota(jnp.int32, (cs, cs), 0)
    col = lax.broadcasted_iota(jnp.int32, (cs, cs), 1)
    incl, strict, eye = row >= col, row > col, row == col
    eye_f = jnp.where(eye, 1.0, 0.0)
    block_bits = jnp.bitwise_xor(row, col)
    big_row = lax.broadcasted_iota(jnp.int32, (rn, rn), 0)
    big_col = lax.broadcasted_iota(jnp.int32, (rn, rn), 1)
    tril_b = jnp.where(jnp.logical_and(big_col <= big_row, big_col >= jnp.bitwise_and(big_row, -cs)),
                       1.0, 0.0).astype(BF16)
    lane = lax.broadcasted_iota(jnp.int32, (1, LANES), 1)
    norm_g = ng_ref[...]
    neg_a = -jnp.exp(alog_ref[...])
    dt_bias = dtb_ref[...]
    st_ref[...] = jnp.zeros_like(st_ref)

    def per_chunk(x):
        width = x.shape[1] // gh
        return jnp.stack([x[s * cs:(s + 1) * cs, j * width:(j + 1) * width]
                          for j in range(gh) for s in range(n_sub)])

    def pick(x, first):
        return jnp.concatenate([jnp.broadcast_to(jnp.sum(jnp.where(lane == first + j, x, 0.0), axis=1,
                                                         keepdims=True), (rn, hd)) for j in range(gh)], axis=1)

    def of_sub(x, s):
        return jnp.stack([x[j * n_sub + s] for j in range(gh)])

    def body(c, carry):
        rows = pl.ds(pl.multiple_of(c * rn, rn), rn)
        q = per_chunk(q_ref[rows, :].astype(F32))
        k = per_chunk(k_ref[rows, :].astype(F32))
        v = per_chunk(v_ref[rows, :].astype(F32))
        ab = ab_ref[rows, :]
        g_wide = pick(neg_a * _softplus(ab + dt_bias), head0)
        beta = per_chunk(pick(_sigmoid(ab), head0 + GDN_HEADS))
        cum = per_chunk(sum(jnp.dot(tril_b, part, preferred_element_type=F32) for part in _split3(g_wide)))
        cum_row = jnp.sum(jnp.where(eye, cum, 0.0), axis=1, keepdims=True)
        grow = jnp.exp(jnp.where(incl, cum - cum_row, 0.0))
        k_beta = k * beta
        lower = jnp.where(strict, _bdot_nt(k_beta, k) * grow, 0.0)
        base_bits = int(np.log2(GDN_INV_BASE))
        power = -jnp.where(jnp.right_shift(block_bits, base_bits) == 0, lower, 0.0)
        inv = eye_f + power
        for _ in range(base_bits - 1):
            power = _bdot(power, power)
            inv = inv + _bdot(inv, power)
        for bits in range(base_bits, int(np.log2(cs))):
            cross = jnp.where(jnp.right_shift(block_bits, bits) == 1, lower, 0.0)
            inv = inv - _bdot(_bdot(inv, cross), inv)
        e_cum = jnp.exp(cum)
        sol = _bdot(inv, jnp.concatenate([v * beta, k_beta * e_cum], axis=2))
        u, w_qd = sol[:, :, :hd], jnp.concatenate([sol[:, :, hd:], q * e_cum], axis=1)
        qk = jnp.where(incl, _bdot_nt(q, k) * grow, 0.0)
        last = cum[:, cs - 1:cs, :]
        k_end = k * jnp.exp(last - cum)
        carry_decay = jnp.exp(last)
        st = st_ref[...]
        outs = []
        for s in range(n_sub):
            t = _bdot(of_sub(w_qd, s), st)
            v_new = of_sub(u, s) - t[:, :cs, :]
            outs.append(t[:, cs:, :] + _bdot(of_sub(qk, s), v_new))
            ke = of_sub(k_end, s)
            st = st * of_sub(carry_decay, s) + jnp.stack([_dot_tn(ke[j], v_new[j]) for j in range(gh)])
        st_ref[...] = st
        o = jnp.concatenate(outs, axis=1)
        o = o * lax.rsqrt(jnp.mean(o * o, axis=-1, keepdims=True) + 1e-6) * norm_g
        o_ref[rows, :] = (_merge_heads(o) * z_ref[rows, :].astype(F32)).astype(o_ref.dtype)
        return carry

    lax.fori_loop(0, n_chunks, body, 0)


def _gdn_mixer(proj, ab, a_log, dt_bias, norm_g):
    bsz, s, _ = proj.shape
    d, h = HEAD_DIM, GDN_HEADS
    assert GDN_CHUNK == HEAD_DIM and s % GDN_ROWS == 0
    n_chunks = s // GDN_ROWS
    gh = GDN_GROUP
    groups = h // gh
    seq = lambda off: pl.BlockSpec((None, s, gh * d), lambda b, i: (b, 0, off + i))
    vec = pl.BlockSpec((1, LANES), lambda b, i: (0, 0))
    pad = lambda a: jnp.zeros((1, LANES), F32).at[0, :h].set(a)
    return pl.pallas_call(
        functools.partial(_gdn_kernel, n_chunks=n_chunks),
        grid=(bsz, groups),
        in_specs=[seq(0), seq(groups), seq(2 * groups), seq(3 * groups),
                  pl.BlockSpec((None, s, LANES), lambda b, i: (b, 0, 0)), vec, vec, vec],
        out_specs=pl.BlockSpec((None, s, gh * d), lambda b, i: (b, 0, i)),
        out_shape=jax.ShapeDtypeStruct((bsz, s, h * d), BF16),
        scratch_shapes=[pltpu.VMEM((gh, d, d), F32)],
        compiler_params=_params("parallel", "parallel"),
        name="gdn",
    )(proj, proj, proj, proj, ab, pad(a_log), pad(dt_bias), norm_g.reshape(1, -1))


def _rope_table_kernel(pos_ref, invf_ref, cos_ref, sin_ref):
    half = ROPE_DIM // 2
    rest = LANES - ROPE_DIM
    for c0 in range(0, pos_ref.shape[1], LANES):
        ang = invf_ref[...] * pos_ref[:, c0:c0 + LANES].astype(F32)
        c, s = jnp.cos(ang), jnp.sin(ang)
        cos_t = jnp.concatenate([c, c, jnp.ones((rest, LANES), F32)], axis=0)
        sin_t = jnp.concatenate([-s, s, jnp.zeros((rest, LANES), F32)], axis=0)
        cos_ref[c0:c0 + LANES, :] = cos_t.T
        sin_ref[c0:c0 + LANES, :] = sin_t.T


def _rope_tables(positions):
    bsz, s = positions.shape
    half = ROPE_DIM // 2
    inv_freq = (np.float32(ROPE_THETA) ** (-np.arange(0, ROPE_DIM, 2, dtype=np.float32) / ROPE_DIM)).astype(np.float32)
    invf = np.ascontiguousarray(np.broadcast_to(inv_freq[:, None], (half, LANES)))
    out = jax.ShapeDtypeStruct((bsz, s, LANES), F32)
    return pl.pallas_call(
        _rope_table_kernel,
        grid=(bsz,),
        in_specs=[pl.BlockSpec((None, 1, s), lambda b: (b, 0, 0)),
                  pl.BlockSpec((half, LANES), lambda b: (0, 0))],
        out_specs=[pl.BlockSpec((None, s, LANES), lambda b: (b, 0, 0))] * 2,
        out_shape=[out, out],
        compiler_params=_params("parallel"),
        name="rope_table",
    )(positions.reshape(bsz, 1, s), jnp.asarray(invf))


def _moba_kernel(q_ref, k_ref, v_ref, o_ref, kr_s, vb_s, km_s, s_s, *, n_blocks):
    blk, hd, gh = MOBA_BLOCK, HEAD_DIM, MOBA_GROUP
    causal, _, _ = _tri_masks(blk)
    lane = lax.broadcasted_iota(jnp.int32, (1, hd), 1)
    blk_id = lax.broadcasted_iota(jnp.int32, (1, n_blocks, 1), 1)
    to_lanes = jnp.where(lax.broadcasted_iota(jnp.int32, (n_blocks, hd), 0)
                         == lax.broadcasted_iota(jnp.int32, (n_blocks, hd), 1), 1.0, 0.0).astype(BF16)

    for j in range(n_blocks):
        rows = slice(j * blk, (j + 1) * blk)
        kr = _per_head(k_ref[rows, :], hd)
        kr_s[:, rows, :hd] = kr
        kr_s[:, rows, hd:] = jnp.broadcast_to(jnp.where(lane == j, 1.0, 0.0), (gh, blk, hd)).astype(BF16)
        km_s[:, j:j + 1, :] = jnp.mean(kr.astype(F32), axis=1, keepdims=True)
        vb_s[:, rows, :hd] = _per_head(v_ref[rows, :], hd).astype(BF16)
        vb_s[:, rows, hd:] = jnp.ones((gh, blk, hd), BF16)

    for qb in range(n_blocks):
        rows = slice(qb * blk, (qb + 1) * blk)
        qb16 = _per_head(q_ref[rows, :], hd)
        bias = jnp.zeros((gh, blk, hd), BF16)
        if qb > 0:
            gate = jnp.where(blk_id < qb, _bdot_nt_f32(km_s[...], qb16.astype(F32)), NEG_INF)
            picked = []
            for j in range(qb):
                gj = gate[:, j:j + 1, :]
                beats = jnp.where(blk_id < j, jnp.where(gate >= gj, 1.0, 0.0), jnp.where(gate > gj, 1.0, 0.0))
                rank = jnp.sum(beats, axis=1, keepdims=True)
                picked.append(jnp.where(rank < MOBA_TOPK, 1.0, 0.0))
            picked.append(jnp.zeros((gh, n_blocks - qb, blk), F32))
            picked = jnp.concatenate(picked, axis=1)
            keep = jnp.stack([_dot_tn(picked[g], to_lanes) for g in range(gh)])
            bias = jnp.where(lane < qb, (1.0 - keep) * NEG_INF, 0.0).astype(BF16)
        width = (qb + 1) * blk
        logits = _bdot_nt(jnp.concatenate([qb16, bias], axis=2), kr_s[:, :width, :])
        if qb > 0:
            s_s[:, :, :qb * blk] = logits[:, :, :qb * blk]
        s_s[:, :, rows] = jnp.where(causal, logits[:, :, qb * blk:], NEG_INF)
        top = s_s[:, :, :LANES]
        for c0 in range(LANES, width, LANES):
            top = jnp.maximum(top, s_s[:, :, c0:c0 + LANES])
        p = jnp.exp2(s_s[:, :, :width] - jnp.max(top, axis=2, keepdims=True))
        pv = _bdot(p, vb_s[:, :width, :])
        o_ref[rows, :] = _merge_heads(pv[:, :, :hd] / pv[:, :, hd:]).astype(o_ref.dtype)


def _moba_mixer(proj):
    bsz, s, _ = proj.shape
    d, h, gh = HEAD_DIM, MOBA_HEADS, MOBA_GROUP
    groups = h // gh
    seq = lambda off: pl.BlockSpec((None, s, gh * d), lambda b, i: (b, 0, off + i))
    return pl.pallas_call(
        functools.partial(_moba_kernel, n_blocks=s // MOBA_BLOCK),
        grid=(bsz, groups),
        in_specs=[seq(0), seq(groups), seq(2 * groups)],
        out_specs=pl.BlockSpec((None, s, gh * d), lambda b, i: (b, 0, i)),
        out_shape=jax.ShapeDtypeStruct((bsz, s, h * d), BF16),
        scratch_shapes=[pltpu.VMEM((gh, s, 2 * d), BF16),
                        pltpu.VMEM((gh, s, 2 * d), BF16),
                        pltpu.VMEM((gh, s // MOBA_BLOCK, d), F32),
                        pltpu.VMEM((gh, MOBA_BLOCK, s), F32)],
        compiler_params=_params("parallel", "parallel"),
        name="moba",
    )(proj, proj, proj)


def _pad_cols(w, multiple=LANES):
    n = w.shape[1]
    return jnp.pad(w, ((0, 0), (0, -n % multiple)))


def kernel(x, positions, gla_w_in, gla_w_gk, gla_b_gk, gla_norm_g, gla_w_out, moba_w_in, moba_w_out,
           gdn_w_in, gdn_conv_w, gdn_a_log, gdn_dt_bias, gdn_norm_g, gdn_w_out, hgrn_lower_bounds,
           hgrn_w_in, hgrn_norm_g, hgrn_w_out, ffn_w_gu, ffn_w_down, ln_g, ln_b):
    bsz, s, d = x.shape
    assert (s % MOBA_BLOCK, (bsz * s) % TOKEN_TILE, d) == (0, 0, D_MODEL)
    t = bsz * s
    xf = x.reshape(t, d)
    cos = sin = None
    for i in range(DEPTH):
        kind, j = i % N_MIXERS, i // N_MIXERS
        tokens = lambda a: a.reshape(bsz, s, -1)
        if kind == 0:
            n_main = sum(GLA_COLS[:4])
            low_w = _pad_cols(gla_w_in[j, :, n_main:])[None]
            proj, log_g = _gla_proj(xf, gla_w_in, j, low_w, gla_w_gk[j], gla_b_gk[j])
            o = _gla_mixer(tokens(proj), tokens(log_g), gla_norm_g[j])
            w_out = gla_w_out
        elif kind == 1:
            if cos is None:
                cos, sin = _rope_tables(positions)
            o = _moba_mixer(tokens(_moba_proj(xf, moba_w_in, j, cos, sin)))
            w_out = moba_w_out
        elif kind == 2:
            n_main = sum(GDN_COLS[:2])
            ab_w = _pad_cols(gdn_w_in[j, :, n_main:])[None]
            proj, ab = _gdn_proj(xf, gdn_w_in, j, ab_w, gdn_conv_w, s)
            o = _gdn_mixer(tokens(proj), tokens(ab), gdn_a_log[j], gdn_dt_bias[j], gdn_norm_g[j])
            w_out = gdn_w_out
        else:
            q, log_f, ig = _hgrn_proj(xf, hgrn_w_in, j, hgrn_lower_bounds, i)
            o = _hgrn_mixer(tokens(q), tokens(log_f), tokens(ig), hgrn_norm_g[j])
            w_out = hgrn_w_out
        xf = _layer_tail(o.reshape(t, d), w_out, j, xf, ffn_w_gu, ffn_w_down, i, ln_g, ln_b)
    return xf.reshape(bsz, s, d)
```

```python
import functools

import numpy as np
import jax
import jax.numpy as jnp
from jax import lax
from jax.experimental import pallas as pl
from jax.experimental.pallas import tpu as pltpu

F32 = jnp.float32
BF16 = jnp.bfloat16

D_MODEL = 1024
DEPTH = 4
N_MIXERS = 4
HEAD_DIM = 128
CHUNK = 64
SUPER = 256
MIX_GROUP = 4
HGRN_GROUP = 8
MOBA_GROUP = 4
GLA_HEADS = 4
GLA_DK = 128
GLA_DV = 256
GLA_GATE_RANK = 16
GLA_GATE_NORMALIZER = 16.0
MOBA_HEADS = 8
MOBA_BLOCK = 256
MOBA_TOPK = 3
ROPE_THETA = 500000.0
ROPE_DIM = 32
NEG_INF = -1e30
LOG2_E = 1.4426950408889634
GDN_HEADS = 8
GDN_CONV = 4
GDN_ROWS = 256
GDN_CHUNK = 128
GDN_INV_BASE = 16
GDN_GROUP = 8
GLA_COLS = (512, 512, 1024, 1024, 16)
GDN_COLS = (3072, 1024, 8, 8)
HGRN_COLS = (1024, 1024, 1024, 1024)
HGRN_HEADS = 8
FFN_HIDDEN = 2816
DEEPNORM_ALPHA = (2.0 * DEPTH) ** 0.25

LANES = 128
SUBLANES = 8
BF16_ROWS = 16
VMEM_LIMIT = 56 * 1024 * 1024
TOKEN_TILE = 1024
COL_CHUNK = 512
GDN_PROJ_CHUNK = 256
TAIL_TILE = 512
LN_SUBTILE = 256


def _dot_tn(a, b):
    return lax.dot_general(a.astype(BF16), b.astype(BF16), (((0,), (0,)), ((), ())),
                           preferred_element_type=F32)


def _bdot(a, b):
    return lax.dot_general(a.astype(BF16), b.astype(BF16), (((2,), (1,)), ((0,), (0,))),
                           preferred_element_type=F32)


def _bdot_nt(a, b):
    return lax.dot_general(a.astype(BF16), b.astype(BF16), (((2,), (2,)), ((0,), (0,))),
                           preferred_element_type=F32)


def _sigmoid(x):
    return 1.0 / (1.0 + jnp.exp(-x))


def _silu(x):
    return x * _sigmoid(x)


def _softplus(x):
    return jnp.maximum(x, 0.0) + jnp.log1p(jnp.exp(-jnp.abs(x)))


def _log_sigmoid(x):
    return -_softplus(-x)


def _col_chunks(n, width=COL_CHUNK):
    return tuple((c0, min(width, n - c0)) for c0 in range(0, n, width))


def _params(*semantics):
    return pltpu.CompilerParams(dimension_semantics=semantics, vmem_limit_bytes=VMEM_LIMIT)


def _resident(shape):
    zeros = (0,) * len(shape)
    return pl.BlockSpec(shape, lambda *_: zeros, pipeline_mode=pl.Buffered(1))


def _layer_norm(z, g, b):
    mu = jnp.mean(z, axis=-1, keepdims=True)
    zc = z - mu
    var = jnp.mean(zc * zc, axis=-1, keepdims=True)
    return zc * lax.rsqrt(var + 1e-5) * g + b


def _tri_masks(n):
    row = lax.broadcasted_iota(jnp.int32, (n, n), 0)
    col = lax.broadcasted_iota(jnp.int32, (n, n), 1)
    return row >= col, row > col, row == col


def _layer_window(layer, k, n, col_block=0):
    return pl.BlockSpec((None, k, n), lambda *_: (layer, 0, col_block), pipeline_mode=pl.Buffered(1))


def _gdn_proj_kernel(x_ref, w_ref, wab_ref, cw_ref, o_ref, ab_ref, carry_ref, *, tiles_per_seq):
    i = pl.program_id(0)
    tile = x_ref.shape[0]
    hist = BF16_ROWS
    n_conv = cw_ref.shape[1]
    n_norm = 2 * n_conv // 3
    first = (i % tiles_per_seq) == 0

    @pl.when(i == 0)
    def _():
        carry_ref[...] = jnp.zeros_like(carry_ref)

    xb = x_ref[...].astype(BF16)
    for c0, cw in _col_chunks(w_ref.shape[1], GDN_PROJ_CHUNK):
        cols = slice(c0, c0 + cw)
        y = jnp.dot(xb, w_ref[:, cols].astype(BF16), preferred_element_type=F32)
        if c0 < n_conv:
            prev = jnp.where(first, 0.0, carry_ref[:, cols])
            carry_ref[:, cols] = y[tile - hist:, :]
            ext = jnp.concatenate([prev, y], axis=0)
            w = cw_ref[:, cols]
            y = y * w[GDN_CONV - 1:GDN_CONV, :]
            for j in range(GDN_CONV - 1):
                lo = hist - (GDN_CONV - 1) + j
                y = y + ext[lo:lo + tile, :] * w[j:j + 1, :]
        y = _silu(y)
        if c0 < n_norm:
            heads = _per_head(y, HEAD_DIM)
            heads = heads * lax.rsqrt(jnp.sum(heads * heads, axis=-1, keepdims=True) + 1e-6)
            y = _merge_heads(heads * (HEAD_DIM ** -0.5) if c0 < n_norm // 2 else heads)
        o_ref[:, cols] = y.astype(o_ref.dtype)
    ab_ref[...] = jnp.dot(xb, wab_ref[...].astype(BF16), preferred_element_type=F32)


def _gdn_proj(x, w_in, layer, ab_w, conv_w, seq_len):
    t, k = x.shape
    n_main = sum(GDN_COLS[:2])
    n_conv = GDN_COLS[0]
    assert n_conv % GDN_PROJ_CHUNK == 0 and (n_conv // 3) % GDN_PROJ_CHUNK == 0 and seq_len % TOKEN_TILE == 0
    row = lambda i: (i, 0)
    return pl.pallas_call(
        functools.partial(_gdn_proj_kernel, tiles_per_seq=seq_len // TOKEN_TILE),
        grid=(t // TOKEN_TILE,),
        in_specs=[pl.BlockSpec((TOKEN_TILE, k), row), _layer_window(layer, k, n_main),
                  _layer_window(0, k, ab_w.shape[2]), _layer_window(layer, GDN_CONV, n_conv)],
        out_specs=[pl.BlockSpec((TOKEN_TILE, n_main), row), pl.BlockSpec((TOKEN_TILE, ab_w.shape[2]), row)],
        out_shape=[jax.ShapeDtypeStruct((t, n_main), BF16), jax.ShapeDtypeStruct((t, ab_w.shape[2]), F32)],
        scratch_shapes=[pltpu.VMEM((BF16_ROWS, n_conv), F32)],
        compiler_params=_params("arbitrary"),
        name="gdn_proj",
    )(x, w_in, ab_w, conv_w)


def _moba_proj_kernel(x_ref, w_ref, cos_ref, sin_ref, o_ref):
    xb = x_ref[...].astype(BF16)
    half = ROPE_DIM // 2
    n_qk = 2 * (w_ref.shape[1] // 3)
    for c0, cw in _col_chunks(w_ref.shape[1]):
        cols = slice(c0, c0 + cw)
        y = jnp.dot(xb, w_ref[:, cols].astype(BF16), preferred_element_type=F32)
        if c0 < n_qk:
            head_lane = jnp.bitwise_and(lax.broadcasted_iota(jnp.int32, (1, cw), 1), HEAD_DIM - 1)
            partner = jnp.where(head_lane < half, pltpu.roll(y, cw - half, 1), pltpu.roll(y, half, 1))
            reps = cw // HEAD_DIM
            y = (y * jnp.concatenate([cos_ref[...]] * reps, axis=1)
                 + partner * jnp.concatenate([sin_ref[...]] * reps, axis=1))
            if c0 < n_qk // 2:
                y = y * (HEAD_DIM ** -0.5 * LOG2_E)
        o_ref[:, cols] = y.astype(o_ref.dtype)


def _moba_proj(x, w_in, layer, cos, sin):
    t, k = x.shape
    n = w_in.shape[2]
    assert (n // 3) % COL_CHUNK == 0
    row = lambda i: (i, 0)
    tab = pl.BlockSpec((TOKEN_TILE, LANES), row)
    return pl.pallas_call(
        _moba_proj_kernel,
        grid=(t // TOKEN_TILE,),
        in_specs=[pl.BlockSpec((TOKEN_TILE, k), row), _layer_window(layer, k, n), tab, tab],
        out_specs=pl.BlockSpec((TOKEN_TILE, n), row),
        out_shape=jax.ShapeDtypeStruct((t, n), BF16),
        compiler_params=_params("parallel"),
        name="moba_proj",
    )(x, w_in, cos.reshape(t, LANES), sin.reshape(t, LANES))


def _hgrn_proj_kernel(x_ref, wq_ref, wf_ref, wig_ref, lb_ref, q_ref, logf_ref, ig_ref, *, layer):
    xb = x_ref[...].astype(BF16)
    raw = lb_ref[...]
    e = jnp.exp(raw - jnp.max(raw, axis=0, keepdims=True))
    soft = e / jnp.sum(e, axis=0, keepdims=True)
    lb = jnp.sum(soft[:layer + 1, :], axis=0, keepdims=True) - soft[0:1, :]
    for c0, cw in _col_chunks(wq_ref.shape[1]):
        cols = slice(c0, c0 + cw)
        q_ref[:, cols] = jnp.dot(xb, wq_ref[:, cols].astype(BF16), preferred_element_type=F32).astype(q_ref.dtype)
        f = jnp.dot(xb, wf_ref[:, cols].astype(BF16), preferred_element_type=F32)
        lbc = lb[:, cols]
        logf_ref[:, cols] = jnp.log(lbc + (1.0 - lbc) * _sigmoid(f))
    n_i = wig_ref.shape[1] // 2
    for c0, cw in _col_chunks(wig_ref.shape[1]):
        cols = slice(c0, c0 + cw)
        y = jnp.dot(xb, wig_ref[:, cols].astype(BF16), preferred_element_type=F32)
        ig_ref[:, cols] = (_silu(y) if c0 >= n_i else y).astype(ig_ref.dtype)


def _hgrn_proj(x, w_in, layer, lower_bounds, depth_index):
    t, k = x.shape
    nq, nf, n_ig = HGRN_COLS[0], HGRN_COLS[1], sum(HGRN_COLS[2:])
    assert nq == nf and n_ig == 2 * nq and nq % COL_CHUNK == 0
    row = lambda i: (i, 0)
    out = lambda n: pl.BlockSpec((TOKEN_TILE, n), row)
    return pl.pallas_call(
        functools.partial(_hgrn_proj_kernel, layer=depth_index),
        grid=(t // TOKEN_TILE,),
        in_specs=[pl.BlockSpec((TOKEN_TILE, k), row), _layer_window(layer, k, nq, 0), _layer_window(layer, k, nf, 1),
                  _layer_window(layer, k, n_ig, 1), _resident(lower_bounds.shape)],
        out_specs=[out(nq), out(nf), out(n_ig)],
        out_shape=[jax.ShapeDtypeStruct((t, nq), BF16), jax.ShapeDtypeStruct((t, nf), F32),
                   jax.ShapeDtypeStruct((t, n_ig), BF16)],
        compiler_params=_params("parallel"),
        name="hgrn_proj",
    )(x, w_in, w_in, w_in, lower_bounds)


def _tail_kernel(o_ref, wo_ref, x_ref, g1_ref, b1_ref, wgu_ref, wd_ref, g2_ref, b2_ref, y_ref, h_ref, *, hidden):
    tile = y_ref.shape[0]
    wo = wo_ref[...].astype(BF16)
    for r0 in range(0, tile, LN_SUBTILE):
        rows = slice(r0, r0 + LN_SUBTILE)
        y = jnp.dot(o_ref[rows, :], wo, preferred_element_type=F32)
        y_ref[rows, :] = _layer_norm(DEEPNORM_ALPHA * x_ref[rows, :] + y, g1_ref[...], b1_ref[...])
    xb = y_ref[...].astype(BF16)
    for c0, cw in _col_chunks(hidden, 256):
        gate = jnp.dot(xb, wgu_ref[:, c0:c0 + cw].astype(BF16), preferred_element_type=F32)
        up = jnp.dot(xb, wgu_ref[:, hidden + c0:hidden + c0 + cw].astype(BF16), preferred_element_type=F32)
        h_ref[:, c0:c0 + cw] = (_silu(gate) * up).astype(BF16)
    for r0 in range(0, tile, LN_SUBTILE):
        rows = slice(r0, r0 + LN_SUBTILE)
        down = jnp.concatenate([jnp.dot(h_ref[rows, :], wd_ref[:, c0:c0 + cw].astype(BF16),
                                        preferred_element_type=F32) for c0, cw in _col_chunks(y_ref.shape[1])], axis=1)
        y_ref[rows, :] = _layer_norm(DEEPNORM_ALPHA * y_ref[rows, :] + down, g2_ref[...], b2_ref[...])


def _layer_tail(o, w_out, out_layer, x, w_gu, w_down, layer, ln_g, ln_b):
    t, k = o.shape
    d = x.shape[1]
    hidden = w_down.shape[1]
    row = lambda i: (i, 0)
    vec = lambda a: a.reshape(1, d)
    return pl.pallas_call(
        functools.partial(_tail_kernel, hidden=hidden),
        grid=(t // TAIL_TILE,),
        in_specs=[pl.BlockSpec((TAIL_TILE, k), row), _layer_window(out_layer, k, d),
                  pl.BlockSpec((TAIL_TILE, d), row), _resident((1, d)), _resident((1, d)),
                  _layer_window(layer, d, 2 * hidden), _layer_window(layer, hidden, d),
                  _resident((1, d)), _resident((1, d))],
        out_specs=pl.BlockSpec((TAIL_TILE, d), row),
        out_shape=jax.ShapeDtypeStruct((t, d), F32),
        scratch_shapes=[pltpu.VMEM((TAIL_TILE, hidden), BF16)],
        compiler_params=_params("parallel"),
        name="layer_tail",
    )(o, w_out, x, vec(ln_g[layer, 0]), vec(ln_b[layer, 0]), w_gu, w_down, vec(ln_g[layer, 1]), vec(ln_b[layer, 1]))


def _split3(x):
    hi = x.astype(BF16)
    rest = x - hi.astype(F32)
    mid = rest.astype(BF16)
    return hi, mid, (rest - mid.astype(F32)).astype(BF16)


def _per_head(x, width):
    return jnp.stack([x[:, j * width:(j + 1) * width] for j in range(x.shape[1] // width)])


def _merge_heads(x):
    return jnp.concatenate([x[j] for j in range(x.shape[0])], axis=1)


def _gla_block(q, k, v, log_g, st):
    g, rows, dk = q.shape
    n = rows // CHUNK
    row = lax.broadcasted_iota(jnp.int32, (rows, rows), 0)
    col = lax.broadcasted_iota(jnp.int32, (rows, rows), 1)
    chunk_start = jnp.bitwise_and(row, -CHUNK)
    causal = jnp.logical_and(col <= row, col >= chunk_start)
    parts = [p for j in range(g) for p in _split3(log_g[j])]
    acc = jnp.dot(jnp.where(causal, 1.0, 0.0).astype(BF16), jnp.concatenate(parts, axis=1),
                  preferred_element_type=F32)
    cum = jnp.stack([acc[:, 3 * j * dk:(3 * j + 1) * dk] + acc[:, (3 * j + 1) * dk:(3 * j + 2) * dk]
                     + acc[:, (3 * j + 2) * dk:(3 * j + 3) * dk] for j in range(g)])
    lasts = [cum[:, (i + 1) * CHUNK - 1:(i + 1) * CHUNK, :] for i in range(n)]
    total = jnp.concatenate([jnp.broadcast_to(last, (g, CHUNK, dk)) for last in lasts], axis=1)
    q_dec = (q * jnp.exp(cum)).astype(BF16)
    k_inv = k * jnp.exp(-cum)
    k_end = (k * jnp.exp(total - cum)).astype(BF16)
    vb = v.astype(BF16)
    scores = jnp.where(causal, _bdot_nt(q_dec, k_inv), 0.0)
    o_intra = _bdot(scores, vb)
    o_inter = []
    for i in range(n):
        rs = slice(i * CHUNK, (i + 1) * CHUNK)
        o_inter.append(_bdot_nt(q_dec[:, rs, :], st))
        update = jnp.stack([_dot_tn(vb[j, rs, :], k_end[j, rs, :]) for j in range(g)])
        st = st * jnp.exp(lasts[i]) + update
    return o_intra + jnp.concatenate(o_inter, axis=1), st


def _gla_finish(o, norm_g, gate, o_ref, rows):
    o = o * lax.rsqrt(jnp.mean(o * o, axis=-1, keepdims=True) + 1e-6) * norm_g
    o_ref[rows, :] = (_merge_heads(o) * gate.astype(F32)).astype(o_ref.dtype)


def _gla_proj_kernel(x_ref, w_ref, wlow_ref, wgk_ref, bgk_ref, o_ref, logg_ref):
    xb = x_ref[...].astype(BF16)
    n_q = GLA_COLS[0]
    n_qkv = sum(GLA_COLS[:3])
    for c0, cw in _col_chunks(w_ref.shape[1]):
        cols = slice(c0, c0 + cw)
        y = jnp.dot(xb, w_ref[:, cols].astype(BF16), preferred_element_type=F32)
        if c0 < n_q:
            y = y * (GLA_DK ** -0.5)
        elif c0 >= n_qkv:
            y = _silu(y)
        o_ref[:, cols] = y.astype(o_ref.dtype)
    low = jnp.dot(xb, wlow_ref[...].astype(BF16), preferred_element_type=F32)
    logit = jnp.dot(low.astype(BF16), wgk_ref[...].astype(BF16), preferred_element_type=F32) + bgk_ref[...]
    logg_ref[...] = _log_sigmoid(logit) * (1.0 / GLA_GATE_NORMALIZER)


def _gla_proj(x, w_in, layer, low_w, w_gk, b_gk):
    t, k = x.shape
    n_main = sum(GLA_COLS[:4])
    n_gate = w_gk.shape[1]
    assert GLA_COLS[0] % COL_CHUNK == 0 and sum(GLA_COLS[:3]) % COL_CHUNK == 0
    wgk_pad = jnp.zeros((low_w.shape[2], n_gate), F32).at[:GLA_GATE_RANK].set(w_gk)
    row = lambda i: (i, 0)
    return pl.pallas_call(
        _gla_proj_kernel,
        grid=(t // TOKEN_TILE,),
        in_specs=[pl.BlockSpec((TOKEN_TILE, k), row), _layer_window(layer, k, n_main),
                  _layer_window(0, k, low_w.shape[2]), _resident(wgk_pad.shape), _resident((1, n_gate))],
        out_specs=[pl.BlockSpec((TOKEN_TILE, n_main), row), pl.BlockSpec((TOKEN_TILE, n_gate), row)],
        out_shape=[jax.ShapeDtypeStruct((t, n_main), BF16), jax.ShapeDtypeStruct((t, n_gate), F32)],
        compiler_params=_params("parallel"),
        name="gla_proj",
    )(x, w_in, low_w, wgk_pad, b_gk.reshape(1, n_gate))


def _gla_kernel(q_ref, k_ref, v_ref, g_ref, logg_ref, ng_ref, o_ref, st_ref, *, n_blocks):
    st_ref[...] = jnp.zeros_like(st_ref)
    norm_g = ng_ref[...]

    def body(c, carry):
        rows = pl.ds(pl.multiple_of(c * SUPER, SUPER), SUPER)
        o, st = _gla_block(_per_head(q_ref[rows, :].astype(F32), GLA_DK), _per_head(k_ref[rows, :].astype(F32), GLA_DK),
                           _per_head(v_ref[rows, :], GLA_DV), _per_head(logg_ref[rows, :], GLA_DK), st_ref[...])
        st_ref[...] = st
        _gla_finish(o, norm_g, g_ref[rows, :], o_ref, rows)
        return carry

    lax.fori_loop(0, n_blocks, body, 0)


def _gla_mixer(proj, log_g, norm_g):
    bsz, s, _ = proj.shape
    dk, dv, h, gh = GLA_DK, GLA_DV, GLA_HEADS, MIX_GROUP
    groups = h // gh
    seq = lambda width, off: pl.BlockSpec((None, s, gh * width), lambda b, i: (b, 0, off + i))
    return pl.pallas_call(
        functools.partial(_gla_kernel, n_blocks=s // SUPER),
        grid=(bsz, groups),
        in_specs=[seq(dk, 0), seq(dk, groups), seq(dv, (2 * h * dk) // (gh * dv)),
                  seq(dv, (2 * h * dk) // (gh * dv) + groups), seq(dk, 0),
                  pl.BlockSpec((1, dv), lambda b, i: (0, 0))],
        out_specs=pl.BlockSpec((None, s, gh * dv), lambda b, i: (b, 0, i)),
        out_shape=jax.ShapeDtypeStruct((bsz, s, h * dv), BF16),
        scratch_shapes=[pltpu.VMEM((gh, dv, dk), F32)],
        compiler_params=_params("parallel", "parallel"),
        name="gla",
    )(proj, proj, proj, proj, log_g, norm_g.reshape(1, -1))


def _hgrn_kernel(q_ref, logf_ref, i_ref, g_ref, ng_ref, o_ref, st_ref, *, n_blocks):
    st_ref[...] = jnp.zeros_like(st_ref)
    norm_g = ng_ref[...]

    def body(c, carry):
        rows = pl.ds(pl.multiple_of(c * SUPER, SUPER), SUPER)
        log_f = logf_ref[rows, :]
        o, st = _gla_block(_per_head(q_ref[rows, :].astype(F32), HEAD_DIM), _per_head(1.0 - jnp.exp(log_f), HEAD_DIM),
                           _per_head(i_ref[rows, :], HEAD_DIM), _per_head(log_f, HEAD_DIM), st_ref[...])
        st_ref[...] = st
        _gla_finish(o, norm_g, g_ref[rows, :], o_ref, rows)
        return carry

    lax.fori_loop(0, n_blocks, body, 0)


def _hgrn_mixer(q, log_f, ig, norm_g):
    bsz, s, _ = q.shape
    d, h, gh = HEAD_DIM, HGRN_HEADS, HGRN_GROUP
    groups = h // gh
    seq = lambda off: pl.BlockSpec((None, s, gh * d), lambda b, i: (b, 0, off + i))
    return pl.pallas_call(
        functools.partial(_hgrn_kernel, n_blocks=s // SUPER),
        grid=(bsz, groups),
        in_specs=[seq(0), seq(0), seq(0), seq(groups), pl.BlockSpec((1, d), lambda b, i: (0, 0))],
        out_specs=pl.BlockSpec((None, s, gh * d), lambda b, i: (b, 0, i)),
        out_shape=jax.ShapeDtypeStruct((bsz, s, h * d), BF16),
        scratch_shapes=[pltpu.VMEM((gh, d, d), F32)],
        compiler_params=_params("parallel", "parallel"),
        name="hgrn2",
    )(q, log_f, ig, ig, norm_g.reshape(1, -1))


def _gdn_kernel(q_ref, k_ref, v_ref, z_ref, ab_ref, alog_ref, dtb_ref, ng_ref, o_ref, st_ref, *, n_chunks):
    rn, cs, gh, hd = GDN_ROWS, GDN_CHUNK, GDN_GROUP, HEAD_DIM
    n_sub = rn // cs
    head0 = pl.program_id(1) * gh
    row = lax.broadcasted_iota(jnp.int32, (cs, cs), 0)
    col = lax.broadcasted_iota(jnp.int32, (cs, cs), 1)
    incl, strict, eye = row >= col, row > col, row == col
    eye_f = jnp.where(eye, 1.0, 0.0)
    block_bits = jnp.bitwise_xor(row, col)
    big_row = lax.broadcasted_iota(jnp.int32, (rn, rn), 0)
    big_col = lax.broadcasted_iota(jnp.int32, (rn, rn), 1)
    tril_b = jnp.where(jnp.logical_and(big_col <= big_row, big_col >= jnp.bitwise_and(big_row, -cs)),
                       1.0, 0.0).astype(BF16)
    lane = lax.broadcasted_iota(jnp.int32, (1, LANES), 1)
    norm_g = ng_ref[...]
    neg_a = -jnp.exp(alog_ref[...])
    dt_bias = dtb_ref[...]
    st_ref[...] = jnp.zeros_like(st_ref)

    def per_chunk(x):
        width = x.shape[1] // gh
        return jnp.stack([x[s * cs:(s + 1) * cs, j * width:(j + 1) * width]
                          for j in range(gh) for s in range(n_sub)])

    def pick(x, first):
        return jnp.concatenate([jnp.broadcast_to(jnp.sum(jnp.where(lane == first + j, x, 0.0), axis=1,
                                                         keepdims=True), (rn, hd)) for j in range(gh)], axis=1)

    def of_sub(x, s):
        return jnp.stack([x[j * n_sub + s] for j in range(gh)])

    def body(c, carry):
        rows = pl.ds(pl.multiple_of(c * rn, rn), rn)
        q = per_chunk(q_ref[rows, :].astype(F32))
        k = per_chunk(k_ref[rows, :].astype(F32))
        v = per_chunk(v_ref[rows, :].astype(F32))
        ab = ab_ref[rows, :]
        g_wide = pick(neg_a * _softplus(ab + dt_bias), head0)
        beta = per_chunk(pick(_sigmoid(ab), head0 + GDN_HEADS))
        cum = per_chunk(sum(jnp.dot(tril_b, part, preferred_element_type=F32) for part in _split3(g_wide)))
        cum_row = jnp.sum(jnp.where(eye, cum, 0.0), axis=1, keepdims=True)
        grow = jnp.exp(jnp.where(incl, cum - cum_row, 0.0))
        k_beta = k * beta
        lower = jnp.where(strict, _bdot_nt(k_beta, k) * grow, 0.0)
        base_bits = int(np.log2(GDN_INV_BASE))
        power = -jnp.where(jnp.right_shift(block_bits, base_bits) == 0, lower, 0.0)
        inv = eye_f + power
        for _ in range(base_bits - 1):
            power = _bdot(power, power)
            inv = inv + _bdot(inv, power)
        for bits in range(base_bits, int(np.log2(cs))):
            cross = jnp.where(jnp.right_shift(block_bits, bits) == 1, lower, 0.0)
            inv = inv - _bdot(_bdot(inv, cross), inv)
        e_cum = jnp.exp(cum)
        sol = _bdot(inv, jnp.concatenate([v * beta, k_beta * e_cum], axis=2))
        u, w_qd = sol[:, :, :hd], jnp.concatenate([sol[:, :, hd:], q * e_cum], axis=1)
        qk = jnp.where(incl, _bdot_nt(q, k) * grow, 0.0)
        last = cum[:, cs - 1:cs, :]
        k_end = k * jnp.exp(last - cum)
        carry_decay = jnp.exp(last)
        st = st_ref[...]
        outs = []
        for s in range(n_sub):
            t = _bdot(of_sub(w_qd, s), st)
            v_new = of_sub(u, s) - t[:, :cs, :]
            outs.append(t[:, cs:, :] + _bdot(of_sub(qk, s), v_new))
            ke = of_sub(k_end, s)
            st = st * of_sub(carry_decay, s) + jnp.stack([_dot_tn(ke[j], v_new[j]) for j in range(gh)])
        st_ref[...] = st
        o = jnp.concatenate(outs, axis=1)
        o = o * lax.rsqrt(jnp.mean(o * o, axis=-1, keepdims=True) + 1e-6) * norm_g
        o_ref[rows, :] = (_merge_heads(o) * z_ref[rows, :].astype(F32)).astype(o_ref.dtype)
        return carry

    lax.fori_loop(0, n_chunks, body, 0)


def _gdn_mixer(proj, ab, a_log, dt_bias, norm_g):
    bsz, s, _ = proj.shape
    d, h = HEAD_DIM, GDN_HEADS
    assert GDN_CHUNK == HEAD_DIM and s % GDN_ROWS == 0
    n_chunks = s // GDN_ROWS
    gh = GDN_GROUP
    groups = h // gh
    seq = lambda off: pl.BlockSpec((None, s, gh * d), lambda b, i: (b, 0, off + i))
    vec = pl.BlockSpec((1, LANES), lambda b, i: (0, 0))
    pad = lambda a: jnp.zeros((1, LANES), F32).at[0, :h].set(a)
    return pl.pallas_call(
        functools.partial(_gdn_kernel, n_chunks=n_chunks),
        grid=(bsz, groups),
        in_specs=[seq(0), seq(groups), seq(2 * groups), seq(3 * groups),
                  pl.BlockSpec((None, s, LANES), lambda b, i: (b, 0, 0)), vec, vec, vec],
        out_specs=pl.BlockSpec((None, s, gh * d), lambda b, i: (b, 0, i)),
        out_shape=jax.ShapeDtypeStruct((bsz, s, h * d), BF16),
        scratch_shapes=[pltpu.VMEM((gh, d, d), F32)],
        compiler_params=_params("parallel", "parallel"),
        name="gdn",
    )(proj, proj, proj, proj, ab, pad(a_log), pad(dt_bias), norm_g.reshape(1, -1))


def _rope_table_kernel(pos_ref, invf_ref, cos_ref, sin_ref):
    half = ROPE_DIM // 2
    rest = LANES - ROPE_DIM
    for c0 in range(0, pos_ref.shape[1], LANES):
        ang = invf_ref[...] * pos_ref[:, c0:c0 + LANES].astype(F32)
        c, s = jnp.cos(ang), jnp.sin(ang)
        cos_t = jnp.concatenate([c, c, jnp.ones((rest, LANES), F32)], axis=0)
        sin_t = jnp.concatenate([-s, s, jnp.zeros((rest, LANES), F32)], axis=0)
        cos_ref[c0:c0 + LANES, :] = cos_t.T
        sin_ref[c0:c0 + LANES, :] = sin_t.T


def _rope_tables(positions):
    bsz, s = positions.shape
    half = ROPE_DIM // 2
    inv_freq = (np.float32(ROPE_THETA) ** (-np.arange(0, ROPE_DIM, 2, dtype=np.float32) / ROPE_DIM)).astype(np.float32)
    invf = np.ascontiguousarray(np.broadcast_to(inv_freq[:, None], (half, LANES)))
    out = jax.ShapeDtypeStruct((bsz, s, LANES), F32)
    return pl.pallas_call(
        _rope_table_kernel,
        grid=(bsz,),
        in_specs=[pl.BlockSpec((None, 1, s), lambda b: (b, 0, 0)),
                  pl.BlockSpec((half, LANES), lambda b: (0, 0))],
        out_specs=[pl.BlockSpec((None, s, LANES), lambda b: (b, 0, 0))] * 2,
        out_shape=[out, out],
        compiler_params=_params("parallel"),
        name="rope_table",
    )(positions.reshape(bsz, 1, s), jnp.asarray(invf))


def _moba_kernel(q_ref, k_ref, v_ref, o_ref, kr_s, vb_s, km_s, s_s, *, n_blocks):
    blk, hd, gh = MOBA_BLOCK, HEAD_DIM, MOBA_GROUP
    causal, _, _ = _tri_masks(blk)
    lane = lax.broadcasted_iota(jnp.int32, (1, hd), 1)
    blk_id = lax.broadcasted_iota(jnp.int32, (1, n_blocks, 1), 1)

    for j in range(n_blocks):
        rows = slice(j * blk, (j + 1) * blk)
        kr = _per_head(k_ref[rows, :], hd)
        kr_s[:, rows, :hd] = kr
        kr_s[:, rows, hd:] = jnp.broadcast_to(jnp.where(lane == j, 1.0, 0.0), (gh, blk, hd)).astype(BF16)
        for part, piece in enumerate(_split3(jnp.mean(kr.astype(F32), axis=1, keepdims=True))):
            km_s[:, part * n_blocks + j:part * n_blocks + j + 1, :] = piece
        vb_s[:, rows, :hd] = _per_head(v_ref[rows, :], hd).astype(BF16)
        vb_s[:, rows, hd:] = jnp.ones((gh, blk, hd), BF16)

    for qb in range(n_blocks):
        rows = slice(qb * blk, (qb + 1) * blk)
        qb16 = _per_head(q_ref[rows, :], hd)
        bias = jnp.zeros((gh, blk, hd), BF16)
        if qb > 0:
            parts = _bdot_nt(km_s[...], qb16)
            gate = parts[:, :n_blocks] + parts[:, n_blocks:2 * n_blocks] + parts[:, 2 * n_blocks:]
            gate = jnp.where(blk_id < qb, gate, NEG_INF)
            picked = []
            for j in range(qb):
                gj = gate[:, j:j + 1, :]
                beats = jnp.where(blk_id < j, jnp.where(gate >= gj, 1.0, 0.0), jnp.where(gate > gj, 1.0, 0.0))
                rank = jnp.sum(beats, axis=1, keepdims=True)
                picked.append(jnp.where(rank < MOBA_TOPK, 1.0, 0.0))
            picked.append(jnp.zeros((gh, hd - qb, blk), F32))
            picked = jnp.concatenate(picked, axis=1)
            keep = jnp.stack([picked[g].T for g in range(gh)])
            bias = jnp.where(lane < qb, (1.0 - keep) * NEG_INF, 0.0).astype(BF16)
        width = (qb + 1) * blk
        logits = _bdot_nt(jnp.concatenate([qb16, bias], axis=2), kr_s[:, :width, :])
        if qb > 0:
            s_s[:, :, :qb * blk] = logits[:, :, :qb * blk]
        s_s[:, :, rows] = jnp.where(causal, logits[:, :, qb * blk:], NEG_INF)
        top = s_s[:, :, :LANES]
        for c0 in range(LANES, width, LANES):
            top = jnp.maximum(top, s_s[:, :, c0:c0 + LANES])
        p = jnp.exp2(s_s[:, :, :width] - jnp.max(top, axis=2, keepdims=True))
        pv = _bdot(p, vb_s[:, :width, :])
        o_ref[rows, :] = _merge_heads(pv[:, :, :hd] / pv[:, :, hd:]).astype(o_ref.dtype)


def _moba_mixer(proj):
    bsz, s, _ = proj.shape
    d, h, gh = HEAD_DIM, MOBA_HEADS, MOBA_GROUP
    groups = h // gh
    seq = lambda off: pl.BlockSpec((None, s, gh * d), lambda b, i: (b, 0, off + i))
    return pl.pallas_call(
        functools.partial(_moba_kernel, n_blocks=s // MOBA_BLOCK),
        grid=(bsz, groups),
        in_specs=[seq(0), seq(groups), seq(2 * groups)],
        out_specs=pl.BlockSpec((None, s, gh * d), lambda b, i: (b, 0, i)),
        out_shape=jax.ShapeDtypeStruct((bsz, s, h * d), BF16),
        scratch_shapes=[pltpu.VMEM((gh, s, 2 * d), BF16),
                        pltpu.VMEM((gh, s, 2 * d), BF16),
                        pltpu.VMEM((gh, 3 * (s // MOBA_BLOCK), d), BF16),
                        pltpu.VMEM((gh, MOBA_BLOCK, s), F32)],
        compiler_params=_params("parallel", "parallel"),
        name="moba",
    )(proj, proj, proj)


def _pad_cols(w, multiple=LANES):
    n = w.shape[1]
    return jnp.pad(w, ((0, 0), (0, -n % multiple)))


def kernel(x, positions, gla_w_in, gla_w_gk, gla_b_gk, gla_norm_g, gla_w_out, moba_w_in, moba_w_out,
           gdn_w_in, gdn_conv_w, gdn_a_log, gdn_dt_bias, gdn_norm_g, gdn_w_out, hgrn_lower_bounds,
           hgrn_w_in, hgrn_norm_g, hgrn_w_out, ffn_w_gu, ffn_w_down, ln_g, ln_b):
    bsz, s, d = x.shape
    assert (s % MOBA_BLOCK, (bsz * s) % TOKEN_TILE, d) == (0, 0, D_MODEL)
    t = bsz * s
    xf = x.reshape(t, d)
    cos = sin = None
    for i in range(DEPTH):
        kind, j = i % N_MIXERS, i // N_MIXERS
        tokens = lambda a: a.reshape(bsz, s, -1)
        if kind == 0:
            n_main = sum(GLA_COLS[:4])
            low_w = _pad_cols(gla_w_in[j, :, n_main:])[None]
            proj, log_g = _gla_proj(xf, gla_w_in, j, low_w, gla_w_gk[j], gla_b_gk[j])
            o = _gla_mixer(tokens(proj), tokens(log_g), gla_norm_g[j])
            w_out = gla_w_out
        elif kind == 1:
            if cos is None:
                cos, sin = _rope_tables(positions)
            o = _moba_mixer(tokens(_moba_proj(xf, moba_w_in, j, cos, sin)))
            w_out = moba_w_out
        elif kind == 2:
            n_main = sum(GDN_COLS[:2])
            ab_w = _pad_cols(gdn_w_in[j, :, n_main:])[None]
            proj, ab = _gdn_proj(xf, gdn_w_in, j, ab_w, gdn_conv_w, s)
            o = _gdn_mixer(tokens(proj), tokens(ab), gdn_a_log[j], gdn_dt_bias[j], gdn_norm_g[j])
            w_out = gdn_w_out
        else:
            q, log_f, ig = _hgrn_proj(xf, hgrn_w_in, j, hgrn_lower_bounds, i)
            o = _hgrn_mixer(tokens(q), tokens(log_f), tokens(ig), hgrn_norm_g[j])
            w_out = hgrn_w_out
        xf = _layer_tail(o.reshape(t, d), w_out, j, xf, ffn_w_gu, ffn_w_down, i, ln_g, ln_b)
    return xf.reshape(bsz, s, d)
```

```python
import functools

import numpy as np
import jax
import jax.numpy as jnp
from jax import lax
from jax.experimental import pallas as pl
from jax.experimental.pallas import tpu as pltpu

F32 = jnp.float32
BF16 = jnp.bfloat16

D_MODEL = 1024
DEPTH = 4
N_MIXERS = 4
HEAD_DIM = 128
CHUNK = 64
SUPER = 256
MIX_GROUP = 4
HGRN_GROUP = 8
MOBA_GROUP = 4
GLA_HEADS = 4
GLA_DK = 128
GLA_DV = 256
GLA_GATE_RANK = 16
GLA_GATE_NORMALIZER = 16.0
MOBA_HEADS = 8
MOBA_BLOCK = 256
MOBA_TOPK = 3
ROPE_THETA = 500000.0
ROPE_DIM = 32
NEG_INF = -1e30
LOG2_E = 1.4426950408889634
GDN_HEADS = 8
GDN_CONV = 4
GDN_ROWS = 256
GDN_CHUNK = 128
GDN_INV_BASE = 16
GDN_GROUP = 8
GLA_COLS = (512, 512, 1024, 1024, 16)
GDN_COLS = (3072, 1024, 8, 8)
HGRN_COLS = (1024, 1024, 1024, 1024)
HGRN_HEADS = 8
FFN_HIDDEN = 2816
DEEPNORM_ALPHA = (2.0 * DEPTH) ** 0.25

LANES = 128
SUBLANES = 8
BF16_ROWS = 16
VMEM_LIMIT = 56 * 1024 * 1024
TOKEN_TILE = 1024
COL_CHUNK = 512
GDN_PROJ_CHUNK = 256
TAIL_TILE = 512
LN_SUBTILE = 256


def _dot_tn(a, b):
    return lax.dot_general(a.astype(BF16), b.astype(BF16), (((0,), (0,)), ((), ())),
                           preferred_element_type=F32)


def _bdot(a, b):
    return lax.dot_general(a.astype(BF16), b.astype(BF16), (((2,), (1,)), ((0,), (0,))),
                           preferred_element_type=F32)


def _bdot_nt(a, b):
    return lax.dot_general(a.astype(BF16), b.astype(BF16), (((2,), (2,)), ((0,), (0,))),
                           preferred_element_type=F32)


def _sigmoid(x):
    return 1.0 / (1.0 + jnp.exp(-x))


def _silu(x):
    return x * _sigmoid(x)


def _softplus(x):
    return jnp.maximum(x, 0.0) + jnp.log1p(jnp.exp(-jnp.abs(x)))


def _log_sigmoid(x):
    return -_softplus(-x)


def _col_chunks(n, width=COL_CHUNK):
    return tuple((c0, min(width, n - c0)) for c0 in range(0, n, width))


def _params(*semantics):
    return pltpu.CompilerParams(dimension_semantics=semantics, vmem_limit_bytes=VMEM_LIMIT)


def _resident(shape):
    zeros = (0,) * len(shape)
    return pl.BlockSpec(shape, lambda *_: zeros, pipeline_mode=pl.Buffered(1))


def _layer_norm(z, g, b):
    mu = jnp.mean(z, axis=-1, keepdims=True)
    zc = z - mu
    var = jnp.mean(zc * zc, axis=-1, keepdims=True)
    return zc * lax.rsqrt(var + 1e-5) * g + b


def _tri_masks(n):
    row = lax.broadcasted_iota(jnp.int32, (n, n), 0)
    col = lax.broadcasted_iota(jnp.int32, (n, n), 1)
    return row >= col, row > col, row == col


def _layer_window(layer, k, n, col_block=0):
    return pl.BlockSpec((None, k, n), lambda *_: (layer, 0, col_block), pipeline_mode=pl.Buffered(1))


def _gdn_proj_kernel(x_ref, w_ref, wab_ref, cw_ref, o_ref, ab_ref, carry_ref, *, tiles_per_seq):
    i = pl.program_id(0)
    tile = x_ref.shape[0]
    hist = BF16_ROWS
    n_conv = cw_ref.shape[1]
    n_norm = 2 * n_conv // 3
    first = (i % tiles_per_seq) == 0

    @pl.when(i == 0)
    def _():
        carry_ref[...] = jnp.zeros_like(carry_ref)

    xb = x_ref[...].astype(BF16)
    for c0, cw in _col_chunks(w_ref.shape[1], GDN_PROJ_CHUNK):
        cols = slice(c0, c0 + cw)
        y = jnp.dot(xb, w_ref[:, cols].astype(BF16), preferred_element_type=F32)
        if c0 < n_conv:
            prev = jnp.where(first, 0.0, carry_ref[:, cols])
            carry_ref[:, cols] = y[tile - hist:, :]
            ext = jnp.concatenate([prev, y], axis=0)
            w = cw_ref[:, cols]
            y = y * w[GDN_CONV - 1:GDN_CONV, :]
            for j in range(GDN_CONV - 1):
                lo = hist - (GDN_CONV - 1) + j
                y = y + ext[lo:lo + tile, :] * w[j:j + 1, :]
        y = _silu(y)
        if c0 < n_norm:
            heads = _per_head(y, HEAD_DIM)
            heads = heads * lax.rsqrt(jnp.sum(heads * heads, axis=-1, keepdims=True) + 1e-6)
            y = _merge_heads(heads * (HEAD_DIM ** -0.5) if c0 < n_norm // 2 else heads)
        o_ref[:, cols] = y.astype(o_ref.dtype)
    ab_ref[...] = jnp.dot(xb, wab_ref[...].astype(BF16), preferred_element_type=F32)


def _gdn_proj(x, w_in, layer, ab_w, conv_w, seq_len):
    t, k = x.shape
    n_main = sum(GDN_COLS[:2])
    n_conv = GDN_COLS[0]
    assert n_conv % GDN_PROJ_CHUNK == 0 and (n_conv // 3) % GDN_PROJ_CHUNK == 0 and seq_len % TOKEN_TILE == 0
    row = lambda i: (i, 0)
    return pl.pallas_call(
        functools.partial(_gdn_proj_kernel, tiles_per_seq=seq_len // TOKEN_TILE),
        grid=(t // TOKEN_TILE,),
        in_specs=[pl.BlockSpec((TOKEN_TILE, k), row), _layer_window(layer, k, n_main),
                  _layer_window(0, k, ab_w.shape[2]), _layer_window(layer, GDN_CONV, n_conv)],
        out_specs=[pl.BlockSpec((TOKEN_TILE, n_main), row), pl.BlockSpec((TOKEN_TILE, ab_w.shape[2]), row)],
        out_shape=[jax.ShapeDtypeStruct((t, n_main), BF16), jax.ShapeDtypeStruct((t, ab_w.shape[2]), F32)],
        scratch_shapes=[pltpu.VMEM((BF16_ROWS, n_conv), F32)],
        compiler_params=_params("arbitrary"),
        name="gdn_proj",
    )(x, w_in, ab_w, conv_w)


def _moba_proj_kernel(x_ref, w_ref, cos_ref, sin_ref, o_ref):
    xb = x_ref[...].astype(BF16)
    half = ROPE_DIM // 2
    n_qk = 2 * (w_ref.shape[1] // 3)
    for c0, cw in _col_chunks(w_ref.shape[1]):
        cols = slice(c0, c0 + cw)
        y = jnp.dot(xb, w_ref[:, cols].astype(BF16), preferred_element_type=F32)
        if c0 < n_qk:
            head_lane = jnp.bitwise_and(lax.broadcasted_iota(jnp.int32, (1, cw), 1), HEAD_DIM - 1)
            partner = jnp.where(head_lane < half, pltpu.roll(y, cw - half, 1), pltpu.roll(y, half, 1))
            reps = cw // HEAD_DIM
            y = (y * jnp.concatenate([cos_ref[...]] * reps, axis=1)
                 + partner * jnp.concatenate([sin_ref[...]] * reps, axis=1))
            if c0 < n_qk // 2:
                y = y * (HEAD_DIM ** -0.5 * LOG2_E)
        o_ref[:, cols] = y.astype(o_ref.dtype)


def _moba_proj(x, w_in, layer, cos, sin):
    t, k = x.shape
    n = w_in.shape[2]
    assert (n // 3) % COL_CHUNK == 0
    row = lambda i: (i, 0)
    tab = pl.BlockSpec((TOKEN_TILE, LANES), row)
    return pl.pallas_call(
        _moba_proj_kernel,
        grid=(t // TOKEN_TILE,),
        in_specs=[pl.BlockSpec((TOKEN_TILE, k), row), _layer_window(layer, k, n), tab, tab],
        out_specs=pl.BlockSpec((TOKEN_TILE, n), row),
        out_shape=jax.ShapeDtypeStruct((t, n), BF16),
        compiler_params=_params("parallel"),
        name="moba_proj",
    )(x, w_in, cos.reshape(t, LANES), sin.reshape(t, LANES))


def _hgrn_proj_kernel(x_ref, wq_ref, wf_ref, wig_ref, lb_ref, q_ref, logf_ref, ig_ref, *, layer):
    xb = x_ref[...].astype(BF16)
    raw = lb_ref[...]
    e = jnp.exp(raw - jnp.max(raw, axis=0, keepdims=True))
    soft = e / jnp.sum(e, axis=0, keepdims=True)
    lb = jnp.sum(soft[:layer + 1, :], axis=0, keepdims=True) - soft[0:1, :]
    for c0, cw in _col_chunks(wq_ref.shape[1]):
        cols = slice(c0, c0 + cw)
        q_ref[:, cols] = jnp.dot(xb, wq_ref[:, cols].astype(BF16), preferred_element_type=F32).astype(q_ref.dtype)
        f = jnp.dot(xb, wf_ref[:, cols].astype(BF16), preferred_element_type=F32)
        lbc = lb[:, cols]
        logf_ref[:, cols] = jnp.log(lbc + (1.0 - lbc) * _sigmoid(f))
    n_i = wig_ref.shape[1] // 2
    for c0, cw in _col_chunks(wig_ref.shape[1]):
        cols = slice(c0, c0 + cw)
        y = jnp.dot(xb, wig_ref[:, cols].astype(BF16), preferred_element_type=F32)
        ig_ref[:, cols] = (_silu(y) if c0 >= n_i else y).astype(ig_ref.dtype)


def _hgrn_proj(x, w_in, layer, lower_bounds, depth_index):
    t, k = x.shape
    nq, nf, n_ig = HGRN_COLS[0], HGRN_COLS[1], sum(HGRN_COLS[2:])
    assert nq == nf and n_ig == 2 * nq and nq % COL_CHUNK == 0
    row = lambda i: (i, 0)
    out = lambda n: pl.BlockSpec((TOKEN_TILE, n), row)
    return pl.pallas_call(
        functools.partial(_hgrn_proj_kernel, layer=depth_index),
        grid=(t // TOKEN_TILE,),
        in_specs=[pl.BlockSpec((TOKEN_TILE, k), row), _layer_window(layer, k, nq, 0), _layer_window(layer, k, nf, 1),
                  _layer_window(layer, k, n_ig, 1), _resident(lower_bounds.shape)],
        out_specs=[out(nq), out(nf), out(n_ig)],
        out_shape=[jax.ShapeDtypeStruct((t, nq), BF16), jax.ShapeDtypeStruct((t, nf), F32),
                   jax.ShapeDtypeStruct((t, n_ig), BF16)],
        compiler_params=_params("parallel"),
        name="hgrn_proj",
    )(x, w_in, w_in, w_in, lower_bounds)


def _tail_kernel(o_ref, wo_ref, x_ref, g1_ref, b1_ref, wgu_ref, wd_ref, g2_ref, b2_ref, y_ref, h_ref, *, hidden):
    tile = y_ref.shape[0]
    wo = wo_ref[...].astype(BF16)
    for r0 in range(0, tile, LN_SUBTILE):
        rows = slice(r0, r0 + LN_SUBTILE)
        y = jnp.dot(o_ref[rows, :], wo, preferred_element_type=F32)
        y_ref[rows, :] = _layer_norm(DEEPNORM_ALPHA * x_ref[rows, :] + y, g1_ref[...], b1_ref[...])
    xb = y_ref[...].astype(BF16)
    for c0, cw in _col_chunks(hidden, 256):
        gate = jnp.dot(xb, wgu_ref[:, c0:c0 + cw].astype(BF16), preferred_element_type=F32)
        up = jnp.dot(xb, wgu_ref[:, hidden + c0:hidden + c0 + cw].astype(BF16), preferred_element_type=F32)
        h_ref[:, c0:c0 + cw] = (_silu(gate) * up).astype(BF16)
    for r0 in range(0, tile, LN_SUBTILE):
        rows = slice(r0, r0 + LN_SUBTILE)
        down = jnp.concatenate([jnp.dot(h_ref[rows, :], wd_ref[:, c0:c0 + cw].astype(BF16),
                                        preferred_element_type=F32) for c0, cw in _col_chunks(y_ref.shape[1])], axis=1)
        y_ref[rows, :] = _layer_norm(DEEPNORM_ALPHA * y_ref[rows, :] + down, g2_ref[...], b2_ref[...])


def _layer_tail(o, w_out, out_layer, x, w_gu, w_down, layer, ln_g, ln_b):
    t, k = o.shape
    d = x.shape[1]
    hidden = w_down.shape[1]
    row = lambda i: (i, 0)
    vec = lambda a: a.reshape(1, d)
    return pl.pallas_call(
        functools.partial(_tail_kernel, hidden=hidden),
        grid=(t // TAIL_TILE,),
        in_specs=[pl.BlockSpec((TAIL_TILE, k), row), _layer_window(out_layer, k, d),
                  pl.BlockSpec((TAIL_TILE, d), row), _resident((1, d)), _resident((1, d)),
                  _layer_window(layer, d, 2 * hidden), _layer_window(layer, hidden, d),
                  _resident((1, d)), _resident((1, d))],
        out_specs=pl.BlockSpec((TAIL_TILE, d), row),
        out_shape=jax.ShapeDtypeStruct((t, d), F32),
        scratch_shapes=[pltpu.VMEM((TAIL_TILE, hidden), BF16)],
        compiler_params=_params("parallel"),
        name="layer_tail",
    )(o, w_out, x, vec(ln_g[layer, 0]), vec(ln_b[layer, 0]), w_gu, w_down, vec(ln_g[layer, 1]), vec(ln_b[layer, 1]))


def _split3(x):
    hi = x.astype(BF16)
    rest = x - hi.astype(F32)
    mid = rest.astype(BF16)
    return hi, mid, (rest - mid.astype(F32)).astype(BF16)


def _per_head(x, width):
    return jnp.stack([x[:, j * width:(j + 1) * width] for j in range(x.shape[1] // width)])


def _merge_heads(x):
    return jnp.concatenate([x[j] for j in range(x.shape[0])], axis=1)


def _gla_block(q, k, v, log_g, st):
    g, rows, dk = q.shape
    n = rows // CHUNK
    row = lax.broadcasted_iota(jnp.int32, (rows, rows), 0)
    col = lax.broadcasted_iota(jnp.int32, (rows, rows), 1)
    chunk_start = jnp.bitwise_and(row, -CHUNK)
    causal = jnp.logical_and(col <= row, col >= chunk_start)
    parts = [p for j in range(g) for p in _split3(log_g[j])]
    acc = jnp.dot(jnp.where(causal, 1.0, 0.0).astype(BF16), jnp.concatenate(parts, axis=1),
                  preferred_element_type=F32)
    cum = jnp.stack([acc[:, 3 * j * dk:(3 * j + 1) * dk] + acc[:, (3 * j + 1) * dk:(3 * j + 2) * dk]
                     + acc[:, (3 * j + 2) * dk:(3 * j + 3) * dk] for j in range(g)])
    lasts = [cum[:, (i + 1) * CHUNK - 1:(i + 1) * CHUNK, :] for i in range(n)]
    total = jnp.concatenate([jnp.broadcast_to(last, (g, CHUNK, dk)) for last in lasts], axis=1)
    q_dec = (q * jnp.exp(cum)).astype(BF16)
    k_inv = k * jnp.exp(-cum)
    k_end = (k * jnp.exp(total - cum)).astype(BF16)
    vb = v.astype(BF16)
    scores = jnp.where(causal, _bdot_nt(q_dec, k_inv), 0.0)
    o_intra = _bdot(scores, vb)
    o_inter = []
    for i in range(n):
        rs = slice(i * CHUNK, (i + 1) * CHUNK)
        o_inter.append(_bdot_nt(q_dec[:, rs, :], st))
        update = jnp.stack([_dot_tn(vb[j, rs, :], k_end[j, rs, :]) for j in range(g)])
        st = st * jnp.exp(lasts[i]) + update
    return o_intra + jnp.concatenate(o_inter, axis=1), st


def _gla_finish(o, norm_g, gate, o_ref, rows):
    o = o * lax.rsqrt(jnp.mean(o * o, axis=-1, keepdims=True) + 1e-6) * norm_g
    o_ref[rows, :] = (_merge_heads(o) * gate.astype(F32)).astype(o_ref.dtype)


def _gla_proj_kernel(x_ref, w_ref, wlow_ref, wgk_ref, bgk_ref, o_ref, logg_ref):
    xb = x_ref[...].astype(BF16)
    n_q = GLA_COLS[0]
    n_qkv = sum(GLA_COLS[:3])
    for c0, cw in _col_chunks(w_ref.shape[1]):
        cols = slice(c0, c0 + cw)
        y = jnp.dot(xb, w_ref[:, cols].astype(BF16), preferred_element_type=F32)
        if c0 < n_q:
            y = y * (GLA_DK ** -0.5)
        elif c0 >= n_qkv:
            y = _silu(y)
        o_ref[:, cols] = y.astype(o_ref.dtype)
    low = jnp.dot(xb, wlow_ref[...].astype(BF16), preferred_element_type=F32)
    logit = jnp.dot(low.astype(BF16), wgk_ref[...].astype(BF16), preferred_element_type=F32) + bgk_ref[...]
    logg_ref[...] = _log_sigmoid(logit) * (1.0 / GLA_GATE_NORMALIZER)


def _gla_proj(x, w_in, layer, low_w, w_gk, b_gk):
    t, k = x.shape
    n_main = sum(GLA_COLS[:4])
    n_gate = w_gk.shape[1]
    assert GLA_COLS[0] % COL_CHUNK == 0 and sum(GLA_COLS[:3]) % COL_CHUNK == 0
    wgk_pad = jnp.zeros((low_w.shape[2], n_gate), F32).at[:GLA_GATE_RANK].set(w_gk)
    row = lambda i: (i, 0)
    return pl.pallas_call(
        _gla_proj_kernel,
        grid=(t // TOKEN_TILE,),
        in_specs=[pl.BlockSpec((TOKEN_TILE, k), row), _layer_window(layer, k, n_main),
                  _layer_window(0, k, low_w.shape[2]), _resident(wgk_pad.shape), _resident((1, n_gate))],
        out_specs=[pl.BlockSpec((TOKEN_TILE, n_main), row), pl.BlockSpec((TOKEN_TILE, n_gate), row)],
        out_shape=[jax.ShapeDtypeStruct((t, n_main), BF16), jax.ShapeDtypeStruct((t, n_gate), F32)],
        compiler_params=_params("parallel"),
        name="gla_proj",
    )(x, w_in, low_w, wgk_pad, b_gk.reshape(1, n_gate))


def _gla_kernel(q_ref, k_ref, v_ref, g_ref, logg_ref, ng_ref, o_ref, st_ref, *, n_blocks):
    st_ref[...] = jnp.zeros_like(st_ref)
    norm_g = ng_ref[...]

    def body(c, carry):
        rows = pl.ds(pl.multiple_of(c * SUPER, SUPER), SUPER)
        o, st = _gla_block(_per_head(q_ref[rows, :].astype(F32), GLA_DK), _per_head(k_ref[rows, :].astype(F32), GLA_DK),
                           _per_head(v_ref[rows, :], GLA_DV), _per_head(logg_ref[rows, :], GLA_DK), st_ref[...])
        st_ref[...] = st
        _gla_finish(o, norm_g, g_ref[rows, :], o_ref, rows)
        return carry

    lax.fori_loop(0, n_blocks, body, 0)


def _gla_mixer(proj, log_g, norm_g):
    bsz, s, _ = proj.shape
    dk, dv, h, gh = GLA_DK, GLA_DV, GLA_HEADS, MIX_GROUP
    groups = h // gh
    seq = lambda width, off: pl.BlockSpec((None, s, gh * width), lambda b, i: (b, 0, off + i))
    return pl.pallas_call(
        functools.partial(_gla_kernel, n_blocks=s // SUPER),
        grid=(bsz, groups),
        in_specs=[seq(dk, 0), seq(dk, groups), seq(dv, (2 * h * dk) // (gh * dv)),
                  seq(dv, (2 * h * dk) // (gh * dv) + groups), seq(dk, 0),
                  pl.BlockSpec((1, dv), lambda b, i: (0, 0))],
        out_specs=pl.BlockSpec((None, s, gh * dv), lambda b, i: (b, 0, i)),
        out_shape=jax.ShapeDtypeStruct((bsz, s, h * dv), BF16),
        scratch_shapes=[pltpu.VMEM((gh, dv, dk), F32)],
        compiler_params=_params("parallel", "parallel"),
        name="gla",
    )(proj, proj, proj, proj, log_g, norm_g.reshape(1, -1))


def _hgrn_kernel(q_ref, logf_ref, i_ref, g_ref, ng_ref, o_ref, st_ref, *, n_blocks):
    st_ref[...] = jnp.zeros_like(st_ref)
    norm_g = ng_ref[...]

    def body(c, carry):
        rows = pl.ds(pl.multiple_of(c * SUPER, SUPER), SUPER)
        log_f = logf_ref[rows, :]
        o, st = _gla_block(_per_head(q_ref[rows, :].astype(F32), HEAD_DIM), _per_head(1.0 - jnp.exp(log_f), HEAD_DIM),
                           _per_head(i_ref[rows, :], HEAD_DIM), _per_head(log_f, HEAD_DIM), st_ref[...])
        st_ref[...] = st
        _gla_finish(o, norm_g, g_ref[rows, :], o_ref, rows)
        return carry

    lax.fori_loop(0, n_blocks, body, 0)


def _hgrn_mixer(q, log_f, ig, norm_g):
    bsz, s, _ = q.shape
    d, h, gh = HEAD_DIM, HGRN_HEADS, HGRN_GROUP
    groups = h // gh
    seq = lambda off: pl.BlockSpec((None, s, gh * d), lambda b, i: (b, 0, off + i))
    return pl.pallas_call(
        functools.partial(_hgrn_kernel, n_blocks=s // SUPER),
        grid=(bsz, groups),
        in_specs=[seq(0), seq(0), seq(0), seq(groups), pl.BlockSpec((1, d), lambda b, i: (0, 0))],
        out_specs=pl.BlockSpec((None, s, gh * d), lambda b, i: (b, 0, i)),
        out_shape=jax.ShapeDtypeStruct((bsz, s, h * d), BF16),
        scratch_shapes=[pltpu.VMEM((gh, d, d), F32)],
        compiler_params=_params("parallel", "parallel"),
        name="hgrn2",
    )(q, log_f, ig, ig, norm_g.reshape(1, -1))


def _gdn_kernel(q_ref, k_ref, v_ref, z_ref, ab_ref, alog_ref, dtb_ref, ng_ref, o_ref, st_ref, *, n_chunks):
    rn, cs, gh, hd = GDN_ROWS, GDN_CHUNK, GDN_GROUP, HEAD_DIM
    n_sub = rn // cs
    head0 = pl.program_id(1) * gh
    row = lax.broadcasted_iota(jnp.int32, (cs, cs), 0)
    col = lax.broadcasted_iota(jnp.int32, (cs, cs), 1)
    incl, strict, eye = row >= col, row > col, row == col
    eye_f = jnp.where(eye, 1.0, 0.0)
    block_bits = jnp.bitwise_xor(row, col)
    big_row = lax.broadcasted_iota(jnp.int32, (rn, rn), 0)
    big_col = lax.broadcasted_iota(jnp.int32, (rn, rn), 1)
    tril_b = jnp.where(jnp.logical_and(big_col <= big_row, big_col >= jnp.bitwise_and(big_row, -cs)),
                       1.0, 0.0).astype(BF16)
    lane = lax.broadcasted_iota(jnp.int32, (1, LANES), 1)
    norm_g = ng_ref[...]
    neg_a = -jnp.exp(alog_ref[...])
    dt_bias = dtb_ref[...]
    st_ref[...] = jnp.zeros_like(st_ref)

    def per_chunk(x):
        width = x.shape[1] // gh
        return jnp.stack([x[s * cs:(s + 1) * cs, j * width:(j + 1) * width]
                          for j in range(gh) for s in range(n_sub)])

    def pick(x, first):
        return jnp.concatenate([jnp.broadcast_to(jnp.sum(jnp.where(lane == first + j, x, 0.0), axis=1,
                                                         keepdims=True), (rn, hd)) for j in range(gh)], axis=1)

    def of_sub(x, s):
        return jnp.stack([x[j * n_sub + s] for j in range(gh)])

    def body(c, carry):
        rows = pl.ds(pl.multiple_of(c * rn, rn), rn)
        q = per_chunk(q_ref[rows, :].astype(F32))
        k = per_chunk(k_ref[rows, :].astype(F32))
        v = per_chunk(v_ref[rows, :].astype(F32))
        ab = ab_ref[rows, :]
        beta = per_chunk(pick(_sigmoid(ab), head0 + GDN_HEADS))
        g_all = neg_a * _softplus(ab + dt_bias)
        cum_all = sum(jnp.dot(tril_b, part, preferred_element_type=F32) for part in _split3(g_all))
        cum = per_chunk(pick(cum_all, head0))
        cum_row = jnp.sum(jnp.where(eye, cum, 0.0), axis=1, keepdims=True)
        grow = jnp.exp(jnp.where(incl, cum - cum_row, 0.0))
        k_beta = k * beta
        lower = jnp.where(strict, _bdot_nt(k_beta, k) * grow, 0.0)
        base_bits = int(np.log2(GDN_INV_BASE))
        power = -jnp.where(jnp.right_shift(block_bits, base_bits) == 0, lower, 0.0)
        inv = eye_f + power
        for _ in range(base_bits - 1):
            power = _bdot(power, power)
            inv = inv + _bdot(inv, power)
        for bits in range(base_bits, int(np.log2(cs))):
            cross = jnp.where(jnp.right_shift(block_bits, bits) == 1, lower, 0.0)
            inv = inv - _bdot(_bdot(inv, cross), inv)
        e_cum = jnp.exp(cum)
        sol = _bdot(inv, jnp.concatenate([v * beta, k_beta * e_cum], axis=2))
        u, w_qd = sol[:, :, :hd], jnp.concatenate([sol[:, :, hd:], q * e_cum], axis=1)
        qk = jnp.where(incl, _bdot_nt(q, k) * grow, 0.0)
        last = cum[:, cs - 1:cs, :]
        k_end = k * jnp.exp(last - cum)
        carry_decay = jnp.exp(last)
        st = st_ref[...]
        outs = []
        for s in range(n_sub):
            t = _bdot(of_sub(w_qd, s), st)
            v_new = of_sub(u, s) - t[:, :cs, :]
            outs.append(t[:, cs:, :] + _bdot(of_sub(qk, s), v_new))
            ke = of_sub(k_end, s)
            st = st * of_sub(carry_decay, s) + jnp.stack([_dot_tn(ke[j], v_new[j]) for j in range(gh)])
        st_ref[...] = st
        o = jnp.concatenate(outs, axis=1)
        o = o * lax.rsqrt(jnp.mean(o * o, axis=-1, keepdims=True) + 1e-6) * norm_g
        o_ref[rows, :] = (_merge_heads(o) * z_ref[rows, :].astype(F32)).astype(o_ref.dtype)
        return carry

    lax.fori_loop(0, n_chunks, body, 0)


def _gdn_mixer(proj, ab, a_log, dt_bias, norm_g):
    bsz, s, _ = proj.shape
    d, h = HEAD_DIM, GDN_HEADS
    assert GDN_CHUNK == HEAD_DIM and s % GDN_ROWS == 0
    n_chunks = s // GDN_ROWS
    gh = GDN_GROUP
    groups = h // gh
    seq = lambda off: pl.BlockSpec((None, s, gh * d), lambda b, i: (b, 0, off + i))
    vec = pl.BlockSpec((1, LANES), lambda b, i: (0, 0))
    pad = lambda a: jnp.zeros((1, LANES), F32).at[0, :h].set(a)
    return pl.pallas_call(
        functools.partial(_gdn_kernel, n_chunks=n_chunks),
        grid=(bsz, groups),
        in_specs=[seq(0), seq(groups), seq(2 * groups), seq(3 * groups),
                  pl.BlockSpec((None, s, LANES), lambda b, i: (b, 0, 0)), vec, vec, vec],
        out_specs=pl.BlockSpec((None, s, gh * d), lambda b, i: (b, 0, i)),
        out_shape=jax.ShapeDtypeStruct((bsz, s, h * d), BF16),
        scratch_shapes=[pltpu.VMEM((gh, d, d), F32)],
        compiler_params=_params("parallel", "parallel"),
        name="gdn",
    )(proj, proj, proj, proj, ab, pad(a_log), pad(dt_bias), norm_g.reshape(1, -1))


def _rope_table_kernel(pos_ref, invf_ref, cos_ref, sin_ref):
    half = ROPE_DIM // 2
    rest = LANES - ROPE_DIM
    for c0 in range(0, pos_ref.shape[1], LANES):
        ang = invf_ref[...] * pos_ref[:, c0:c0 + LANES].astype(F32)
        c, s = jnp.cos(ang), jnp.sin(ang)
        cos_t = jnp.concatenate([c, c, jnp.ones((rest, LANES), F32)], axis=0)
        sin_t = jnp.concatenate([-s, s, jnp.zeros((rest, LANES), F32)], axis=0)
        cos_ref[c0:c0 + LANES, :] = cos_t.T
        sin_ref[c0:c0 + LANES, :] = sin_t.T


def _rope_tables(positions):
    bsz, s = positions.shape
    half = ROPE_DIM // 2
    inv_freq = (np.float32(ROPE_THETA) ** (-np.arange(0, ROPE_DIM, 2, dtype=np.float32) / ROPE_DIM)).astype(np.float32)
    invf = np.ascontiguousarray(np.broadcast_to(inv_freq[:, None], (half, LANES)))
    out = jax.ShapeDtypeStruct((bsz, s, LANES), F32)
    return pl.pallas_call(
        _rope_table_kernel,
        grid=(bsz,),
        in_specs=[pl.BlockSpec((None, 1, s), lambda b: (b, 0, 0)),
                  pl.BlockSpec((half, LANES), lambda b: (0, 0))],
        out_specs=[pl.BlockSpec((None, s, LANES), lambda b: (b, 0, 0))] * 2,
        out_shape=[out, out],
        compiler_params=_params("parallel"),
        name="rope_table",
    )(positions.reshape(bsz, 1, s), jnp.asarray(invf))


def _moba_kernel(q_ref, k_ref, v_ref, o_ref, kr_s, vb_s, km_s, s_s, *, n_blocks):
    blk, hd, gh = MOBA_BLOCK, HEAD_DIM, MOBA_GROUP
    causal, _, _ = _tri_masks(blk)
    lane = lax.broadcasted_iota(jnp.int32, (1, hd), 1)
    blk_id = lax.broadcasted_iota(jnp.int32, (1, n_blocks, 1), 1)

    for j in range(n_blocks):
        rows = slice(j * blk, (j + 1) * blk)
        kr = _per_head(k_ref[rows, :], hd)
        kr_s[:, rows, :hd] = kr
        kr_s[:, rows, hd:] = jnp.broadcast_to(jnp.where(lane == j, 1.0, 0.0), (gh, blk, hd)).astype(BF16)
        for part, piece in enumerate(_split3(jnp.mean(kr.astype(F32), axis=1, keepdims=True))):
            km_s[:, part * n_blocks + j:part * n_blocks + j + 1, :] = piece
        vb_s[:, rows, :hd] = _per_head(v_ref[rows, :], hd).astype(BF16)
        vb_s[:, rows, hd:] = jnp.ones((gh, blk, hd), BF16)

    for qb in range(n_blocks):
        rows = slice(qb * blk, (qb + 1) * blk)
        qb16 = _per_head(q_ref[rows, :], hd)
        bias = jnp.zeros((gh, blk, hd), BF16)
        if qb > 0:
            parts = _bdot_nt(km_s[...], qb16)
            gate = parts[:, :n_blocks] + parts[:, n_blocks:2 * n_blocks] + parts[:, 2 * n_blocks:]
            gate = jnp.where(blk_id < qb, gate, NEG_INF)
            picked = []
            for j in range(qb):
                gj = gate[:, j:j + 1, :]
                beats = jnp.where(blk_id < j, jnp.where(gate >= gj, 1.0, 0.0), jnp.where(gate > gj, 1.0, 0.0))
                rank = jnp.sum(beats, axis=1, keepdims=True)
                picked.append(jnp.where(rank < MOBA_TOPK, 1.0, 0.0))
            picked.append(jnp.zeros((gh, hd - qb, blk), F32))
            picked = jnp.concatenate(picked, axis=1)
            keep = jnp.stack([picked[g].T for g in range(gh)])
            bias = jnp.where(lane < qb, (1.0 - keep) * NEG_INF, 0.0).astype(BF16)
        width = (qb + 1) * blk
        logits = _bdot_nt(jnp.concatenate([qb16, bias], axis=2), kr_s[:, :width, :])
        if qb > 0:
            s_s[:, :, :qb * blk] = logits[:, :, :qb * blk]
        s_s[:, :, rows] = jnp.where(causal, logits[:, :, qb * blk:], NEG_INF)
        top = s_s[:, :, :LANES]
        for c0 in range(LANES, width, LANES):
            top = jnp.maximum(top, s_s[:, :, c0:c0 + LANES])
        p = jnp.exp2(s_s[:, :, :width] - jnp.max(top, axis=2, keepdims=True))
        pv = _bdot(p, vb_s[:, :width, :])
        o_ref[rows, :] = _merge_heads(pv[:, :, :hd] / pv[:, :, hd:]).astype(o_ref.dtype)


def _moba_mixer(proj):
    bsz, s, _ = proj.shape
    d, h, gh = HEAD_DIM, MOBA_HEADS, MOBA_GROUP
    groups = h // gh
    seq = lambda off: pl.BlockSpec((None, s, gh * d), lambda b, i: (b, 0, off + i))
    return pl.pallas_call(
        functools.partial(_moba_kernel, n_blocks=s // MOBA_BLOCK),
        grid=(bsz, groups),
        in_specs=[seq(0), seq(groups), seq(2 * groups)],
        out_specs=pl.BlockSpec((None, s, gh * d), lambda b, i: (b, 0, i)),
        out_shape=jax.ShapeDtypeStruct((bsz, s, h * d), BF16),
        scratch_shapes=[pltpu.VMEM((gh, s, 2 * d), BF16),
                        pltpu.VMEM((gh, s, 2 * d), BF16),
                        pltpu.VMEM((gh, 3 * (s // MOBA_BLOCK), d), BF16),
                        pltpu.VMEM((gh, MOBA_BLOCK, s), F32)],
        compiler_params=_params("parallel", "parallel"),
        name="moba",
    )(proj, proj, proj)


def _pad_cols(w, multiple=LANES):
    n = w.shape[1]
    return jnp.pad(w, ((0, 0), (0, -n % multiple)))


def kernel(x, positions, gla_w_in, gla_w_gk, gla_b_gk, gla_norm_g, gla_w_out, moba_w_in, moba_w_out,
           gdn_w_in, gdn_conv_w, gdn_a_log, gdn_dt_bias, gdn_norm_g, gdn_w_out, hgrn_lower_bounds,
           hgrn_w_in, hgrn_norm_g, hgrn_w_out, ffn_w_gu, ffn_w_down, ln_g, ln_b):
    bsz, s, d = x.shape
    assert (s % MOBA_BLOCK, (bsz * s) % TOKEN_TILE, d) == (0, 0, D_MODEL)
    t = bsz * s
    xf = x.reshape(t, d)
    cos = sin = None
    for i in range(DEPTH):
        kind, j = i % N_MIXERS, i // N_MIXERS
        tokens = lambda a: a.reshape(bsz, s, -1)
        if kind == 0:
            n_main = sum(GLA_COLS[:4])
            low_w = _pad_cols(gla_w_in[j, :, n_main:])[None]
            proj, log_g = _gla_proj(xf, gla_w_in, j, low_w, gla_w_gk[j], gla_b_gk[j])
            o = _gla_mixer(tokens(proj), tokens(log_g), gla_norm_g[j])
            w_out = gla_w_out
        elif kind == 1:
            if cos is None:
                cos, sin = _rope_tables(positions)
            o = _moba_mixer(tokens(_moba_proj(xf, moba_w_in, j, cos, sin)))
            w_out = moba_w_out
        elif kind == 2:
            n_main = sum(GDN_COLS[:2])
            ab_w = _pad_cols(gdn_w_in[j, :, n_main:])[None]
            proj, ab = _gdn_proj(xf, gdn_w_in, j, ab_w, gdn_conv_w, s)
            o = _gdn_mixer(tokens(proj), tokens(ab), gdn_a_log[j], gdn_dt_bias[j], gdn_norm_g[j])
            w_out = gdn_w_out
        else:
            q, log_f, ig = _hgrn_proj(xf, hgrn_w_in, j, hgrn_lower_bounds, i)
            o = _hgrn_mixer(tokens(q), tokens(log_f), tokens(ig), hgrn_norm_g[j])
            w_out = hgrn_w_out
        xf = _layer_tail(o.reshape(t, d), w_out, j, xf, ffn_w_gu, ffn_w_down, i, ln_g, ln_b)
    return xf.reshape(bsz, s, d)
```

```python
import functools

import numpy as np
import jax
import jax.numpy as jnp
from jax import lax
from jax.experimental import pallas as pl
from jax.experimental.pallas import tpu as pltpu

F32 = jnp.float32
BF16 = jnp.bfloat16

D_MODEL = 1024
DEPTH = 4
N_MIXERS = 4
HEAD_DIM = 128
CHUNK = 64
SUPER = 256
MIX_GROUP = 4
HGRN_GROUP = 8
MOBA_GROUP = 4
GLA_HEADS = 4
GLA_DK = 128
GLA_DV = 256
GLA_GATE_RANK = 16
GLA_GATE_NORMALIZER = 16.0
MOBA_HEADS = 8
MOBA_BLOCK = 256
MOBA_TOPK = 3
ROPE_THETA = 500000.0
ROPE_DIM = 32
NEG_INF = -1e30
LOG2_E = 1.4426950408889634
GDN_HEADS = 8
GDN_CONV = 4
GDN_ROWS = 256
GDN_CHUNK = 128
GDN_INV_BASE = 16
GDN_GROUP = 8
GLA_COLS = (512, 512, 1024, 1024, 16)
GDN_COLS = (3072, 1024, 8, 8)
HGRN_COLS = (1024, 1024, 1024, 1024)
HGRN_HEADS = 8
DEEPNORM_ALPHA = (2.0 * DEPTH) ** 0.25

LANES = 128
BF16_ROWS = 16
VMEM_LIMIT = 56 * 1024 * 1024
TOKEN_TILE = 1024
COL_CHUNK = 512
GDN_PROJ_CHUNK = 256
TAIL_TILE = 512
LN_SUBTILE = 256


def _dot_tn(a, b):
    return lax.dot_general(a.astype(BF16), b.astype(BF16), (((0,), (0,)), ((), ())),
                           preferred_element_type=F32)


def _bdot(a, b):
    return lax.dot_general(a.astype(BF16), b.astype(BF16), (((2,), (1,)), ((0,), (0,))),
                           preferred_element_type=F32)


def _bdot_nt(a, b):
    return lax.dot_general(a.astype(BF16), b.astype(BF16), (((2,), (2,)), ((0,), (0,))),
                           preferred_element_type=F32)


def _sigmoid(x):
    return 1.0 / (1.0 + jnp.exp(-x))


def _silu(x):
    return x * _sigmoid(x)


def _softplus(x):
    return jnp.maximum(x, 0.0) + jnp.log1p(jnp.exp(-jnp.abs(x)))


def _log_sigmoid(x):
    return -_softplus(-x)


def _col_chunks(n, width=COL_CHUNK):
    return tuple((c0, min(width, n - c0)) for c0 in range(0, n, width))


def _params(*semantics):
    return pltpu.CompilerParams(dimension_semantics=semantics, vmem_limit_bytes=VMEM_LIMIT)


def _resident(shape):
    zeros = (0,) * len(shape)
    return pl.BlockSpec(shape, lambda *_: zeros, pipeline_mode=pl.Buffered(1))


def _layer_norm(z, g, b):
    mu = jnp.mean(z, axis=-1, keepdims=True)
    zc = z - mu
    var = jnp.mean(zc * zc, axis=-1, keepdims=True)
    return zc * lax.rsqrt(var + 1e-5) * g + b


def _tri_masks(n):
    row = lax.broadcasted_iota(jnp.int32, (n, n), 0)
    col = lax.broadcasted_iota(jnp.int32, (n, n), 1)
    return row >= col, row > col, row == col


def _layer_window(layer, k, n, col_block=0):
    return pl.BlockSpec((None, k, n), lambda *_: (layer, 0, col_block), pipeline_mode=pl.Buffered(1))


def _gdn_proj_kernel(x_ref, w_ref, wab_ref, cw_ref, o_ref, ab_ref, carry_ref, *, tiles_per_seq):
    i = pl.program_id(0)
    tile = x_ref.shape[0]
    hist = BF16_ROWS
    n_conv = cw_ref.shape[1]
    n_norm = 2 * n_conv // 3
    first = (i % tiles_per_seq) == 0

    @pl.when(i == 0)
    def _():
        carry_ref[...] = jnp.zeros_like(carry_ref)

    xb = x_ref[...].astype(BF16)
    for c0, cw in _col_chunks(w_ref.shape[1], GDN_PROJ_CHUNK):
        cols = slice(c0, c0 + cw)
        y = jnp.dot(xb, w_ref[:, cols].astype(BF16), preferred_element_type=F32)
        if c0 < n_conv:
            prev = jnp.where(first, 0.0, carry_ref[:, cols])
            carry_ref[:, cols] = y[tile - hist:, :]
            ext = jnp.concatenate([prev, y], axis=0)
            w = cw_ref[:, cols]
            y = y * w[GDN_CONV - 1:GDN_CONV, :]
            for j in range(GDN_CONV - 1):
                lo = hist - (GDN_CONV - 1) + j
                y = y + ext[lo:lo + tile, :] * w[j:j + 1, :]
        y = _silu(y)
        if c0 < n_norm:
            heads = _per_head(y, HEAD_DIM)
            heads = heads * lax.rsqrt(jnp.sum(heads * heads, axis=-1, keepdims=True) + 1e-6)
            y = _merge_heads(heads * (HEAD_DIM ** -0.5) if c0 < n_norm // 2 else heads)
        o_ref[:, cols] = y.astype(o_ref.dtype)
    ab_ref[...] = jnp.dot(xb, wab_ref[...].astype(BF16), preferred_element_type=F32)


def _gdn_proj(x, w_in, layer, ab_w, conv_w, seq_len):
    t, k = x.shape
    n_main = sum(GDN_COLS[:2])
    n_conv = GDN_COLS[0]
    assert n_conv % GDN_PROJ_CHUNK == 0 and (n_conv // 3) % GDN_PROJ_CHUNK == 0 and seq_len % TOKEN_TILE == 0
    row = lambda i: (i, 0)
    return pl.pallas_call(
        functools.partial(_gdn_proj_kernel, tiles_per_seq=seq_len // TOKEN_TILE),
        grid=(t // TOKEN_TILE,),
        in_specs=[pl.BlockSpec((TOKEN_TILE, k), row), _layer_window(layer, k, n_main),
                  _layer_window(0, k, ab_w.shape[2]), _layer_window(layer, GDN_CONV, n_conv)],
        out_specs=[pl.BlockSpec((TOKEN_TILE, n_main), row), pl.BlockSpec((TOKEN_TILE, ab_w.shape[2]), row)],
        out_shape=[jax.ShapeDtypeStruct((t, n_main), BF16), jax.ShapeDtypeStruct((t, ab_w.shape[2]), F32)],
        scratch_shapes=[pltpu.VMEM((BF16_ROWS, n_conv), F32)],
        compiler_params=_params("arbitrary"),
        name="gdn_proj",
    )(x, w_in, ab_w, conv_w)


def _moba_proj_kernel(x_ref, w_ref, cos_ref, sin_ref, o_ref):
    xb = x_ref[...].astype(BF16)
    half = ROPE_DIM // 2
    n_qk = 2 * (w_ref.shape[1] // 3)
    for c0, cw in _col_chunks(w_ref.shape[1]):
        cols = slice(c0, c0 + cw)
        y = jnp.dot(xb, w_ref[:, cols].astype(BF16), preferred_element_type=F32)
        if c0 < n_qk:
            head_lane = jnp.bitwise_and(lax.broadcasted_iota(jnp.int32, (1, cw), 1), HEAD_DIM - 1)
            partner = jnp.where(head_lane < half, pltpu.roll(y, cw - half, 1), pltpu.roll(y, half, 1))
            reps = cw // HEAD_DIM
            y = (y * jnp.concatenate([cos_ref[...]] * reps, axis=1)
                 + partner * jnp.concatenate([sin_ref[...]] * reps, axis=1))
            if c0 < n_qk // 2:
                y = y * (HEAD_DIM ** -0.5 * LOG2_E)
        o_ref[:, cols] = y.astype(o_ref.dtype)


def _moba_proj(x, w_in, layer, cos, sin):
    t, k = x.shape
    n = w_in.shape[2]
    assert (n // 3) % COL_CHUNK == 0
    row = lambda i: (i, 0)
    tab = pl.BlockSpec((TOKEN_TILE, LANES), row)
    return pl.pallas_call(
        _moba_proj_kernel,
        grid=(t // TOKEN_TILE,),
        in_specs=[pl.BlockSpec((TOKEN_TILE, k), row), _layer_window(layer, k, n), tab, tab],
        out_specs=pl.BlockSpec((TOKEN_TILE, n), row),
        out_shape=jax.ShapeDtypeStruct((t, n), BF16),
        compiler_params=_params("parallel"),
        name="moba_proj",
    )(x, w_in, cos.reshape(t, LANES), sin.reshape(t, LANES))


def _hgrn_proj_kernel(x_ref, wq_ref, wf_ref, wig_ref, lb_ref, q_ref, logf_ref, ig_ref, *, layer):
    xb = x_ref[...].astype(BF16)
    raw = lb_ref[...]
    e = jnp.exp(raw - jnp.max(raw, axis=0, keepdims=True))
    soft = e / jnp.sum(e, axis=0, keepdims=True)
    lb = jnp.sum(soft[:layer + 1, :], axis=0, keepdims=True) - soft[0:1, :]
    for c0, cw in _col_chunks(wq_ref.shape[1]):
        cols = slice(c0, c0 + cw)
        q_ref[:, cols] = jnp.dot(xb, wq_ref[:, cols].astype(BF16), preferred_element_type=F32).astype(q_ref.dtype)
        f = jnp.dot(xb, wf_ref[:, cols].astype(BF16), preferred_element_type=F32)
        lbc = lb[:, cols]
        logf_ref[:, cols] = jnp.log(lbc + (1.0 - lbc) * _sigmoid(f))
    n_i = wig_ref.shape[1] // 2
    for c0, cw in _col_chunks(wig_ref.shape[1]):
        cols = slice(c0, c0 + cw)
        y = jnp.dot(xb, wig_ref[:, cols].astype(BF16), preferred_element_type=F32)
        ig_ref[:, cols] = (_silu(y) if c0 >= n_i else y).astype(ig_ref.dtype)


def _hgrn_proj(x, w_in, layer, lower_bounds, depth_index):
    t, k = x.shape
    nq, nf, n_ig = HGRN_COLS[0], HGRN_COLS[1], sum(HGRN_COLS[2:])
    assert nq == nf and n_ig == 2 * nq and nq % COL_CHUNK == 0
    row = lambda i: (i, 0)
    out = lambda n: pl.BlockSpec((TOKEN_TILE, n), row)
    return pl.pallas_call(
        functools.partial(_hgrn_proj_kernel, layer=depth_index),
        grid=(t // TOKEN_TILE,),
        in_specs=[pl.BlockSpec((TOKEN_TILE, k), row), _layer_window(layer, k, nq, 0), _layer_window(layer, k, nf, 1),
                  _layer_window(layer, k, n_ig, 1), _resident(lower_bounds.shape)],
        out_specs=[out(nq), out(nf), out(n_ig)],
        out_shape=[jax.ShapeDtypeStruct((t, nq), BF16), jax.ShapeDtypeStruct((t, nf), F32),
                   jax.ShapeDtypeStruct((t, n_ig), BF16)],
        compiler_params=_params("parallel"),
        name="hgrn_proj",
    )(x, w_in, w_in, w_in, lower_bounds)


def _tail_kernel(o_ref, wo_ref, x_ref, g1_ref, b1_ref, wgu_ref, wd_ref, g2_ref, b2_ref, y_ref, h_ref, *, hidden):
    tile = y_ref.shape[0]
    wo = wo_ref[...].astype(BF16)
    for r0 in range(0, tile, LN_SUBTILE):
        rows = slice(r0, r0 + LN_SUBTILE)
        y = jnp.dot(o_ref[rows, :], wo, preferred_element_type=F32)
        y_ref[rows, :] = _layer_norm(DEEPNORM_ALPHA * x_ref[rows, :] + y, g1_ref[...], b1_ref[...])
    xb = y_ref[...].astype(BF16)
    for c0, cw in _col_chunks(hidden, 256):
        gate = jnp.dot(xb, wgu_ref[:, c0:c0 + cw].astype(BF16), preferred_element_type=F32)
        up = jnp.dot(xb, wgu_ref[:, hidden + c0:hidden + c0 + cw].astype(BF16), preferred_element_type=F32)
        h_ref[:, c0:c0 + cw] = (_silu(gate) * up).astype(BF16)
    for r0 in range(0, tile, LN_SUBTILE):
        rows = slice(r0, r0 + LN_SUBTILE)
        down = jnp.concatenate([jnp.dot(h_ref[rows, :], wd_ref[:, c0:c0 + cw].astype(BF16),
                                        preferred_element_type=F32) for c0, cw in _col_chunks(y_ref.shape[1])], axis=1)
        y_ref[rows, :] = _layer_norm(DEEPNORM_ALPHA * y_ref[rows, :] + down, g2_ref[...], b2_ref[...])


def _layer_tail(o, w_out, out_layer, x, w_gu, w_down, layer, ln_g, ln_b):
    t, k = o.shape
    d = x.shape[1]
    hidden = w_down.shape[1]
    row = lambda i: (i, 0)
    vec = lambda a: a.reshape(1, d)
    return pl.pallas_call(
        functools.partial(_tail_kernel, hidden=hidden),
        grid=(t // TAIL_TILE,),
        in_specs=[pl.BlockSpec((TAIL_TILE, k), row), _layer_window(out_layer, k, d),
                  pl.BlockSpec((TAIL_TILE, d), row), _resident((1, d)), _resident((1, d)),
                  _layer_window(layer, d, 2 * hidden), _layer_window(layer, hidden, d),
                  _resident((1, d)), _resident((1, d))],
        out_specs=pl.BlockSpec((TAIL_TILE, d), row),
        out_shape=jax.ShapeDtypeStruct((t, d), F32),
        scratch_shapes=[pltpu.VMEM((TAIL_TILE, hidden), BF16)],
        compiler_params=_params("parallel"),
        name="layer_tail",
    )(o, w_out, x, vec(ln_g[layer, 0]), vec(ln_b[layer, 0]), w_gu, w_down, vec(ln_g[layer, 1]), vec(ln_b[layer, 1]))


def _split3(x):
    hi = x.astype(BF16)
    rest = x - hi.astype(F32)
    mid = rest.astype(BF16)
    return hi, mid, (rest - mid.astype(F32)).astype(BF16)


def _per_head(x, width):
    return jnp.stack([x[:, j * width:(j + 1) * width] for j in range(x.shape[1] // width)])


def _merge_heads(x):
    return jnp.concatenate([x[j] for j in range(x.shape[0])], axis=1)


def _gla_block(q, k, v, log_g, st):
    g, rows, dk = q.shape
    n = rows // CHUNK
    row = lax.broadcasted_iota(jnp.int32, (rows, rows), 0)
    col = lax.broadcasted_iota(jnp.int32, (rows, rows), 1)
    chunk_start = jnp.bitwise_and(row, -CHUNK)
    causal = jnp.logical_and(col <= row, col >= chunk_start)
    parts = [p for j in range(g) for p in _split3(log_g[j])]
    acc = jnp.dot(jnp.where(causal, 1.0, 0.0).astype(BF16), jnp.concatenate(parts, axis=1),
                  preferred_element_type=F32)
    cum = jnp.stack([acc[:, 3 * j * dk:(3 * j + 1) * dk] + acc[:, (3 * j + 1) * dk:(3 * j + 2) * dk]
                     + acc[:, (3 * j + 2) * dk:(3 * j + 3) * dk] for j in range(g)])
    lasts = [cum[:, (i + 1) * CHUNK - 1:(i + 1) * CHUNK, :] for i in range(n)]
    total = jnp.concatenate([jnp.broadcast_to(last, (g, CHUNK, dk)) for last in lasts], axis=1)
    q_dec = (q * jnp.exp(cum)).astype(BF16)
    k_inv = k * jnp.exp(-cum)
    k_end = (k * jnp.exp(total - cum)).astype(BF16)
    vb = v.astype(BF16)
    scores = jnp.where(causal, _bdot_nt(q_dec, k_inv), 0.0)
    o_intra = _bdot(scores, vb)
    o_inter = []
    for i in range(n):
        rs = slice(i * CHUNK, (i + 1) * CHUNK)
        o_inter.append(_bdot_nt(q_dec[:, rs, :], st))
        update = jnp.stack([_dot_tn(vb[j, rs, :], k_end[j, rs, :]) for j in range(g)])
        st = st * jnp.exp(lasts[i]) + update
    return o_intra + jnp.concatenate(o_inter, axis=1), st


def _gla_finish(o, norm_g, gate, o_ref, rows):
    o = o * lax.rsqrt(jnp.mean(o * o, axis=-1, keepdims=True) + 1e-6) * norm_g
    o_ref[rows, :] = (_merge_heads(o) * gate.astype(F32)).astype(o_ref.dtype)


def _gla_proj_kernel(x_ref, w_ref, wlow_ref, wgk_ref, bgk_ref, o_ref, logg_ref):
    xb = x_ref[...].astype(BF16)
    n_q = GLA_COLS[0]
    n_qkv = sum(GLA_COLS[:3])
    for c0, cw in _col_chunks(w_ref.shape[1]):
        cols = slice(c0, c0 + cw)
        y = jnp.dot(xb, w_ref[:, cols].astype(BF16), preferred_element_type=F32)
        if c0 < n_q:
            y = y * (GLA_DK ** -0.5)
        elif c0 >= n_qkv:
            y = _silu(y)
        o_ref[:, cols] = y.astype(o_ref.dtype)
    low = jnp.dot(xb, wlow_ref[...].astype(BF16), preferred_element_type=F32)
    logit = jnp.dot(low.astype(BF16), wgk_ref[...].astype(BF16), preferred_element_type=F32) + bgk_ref[...]
    logg_ref[...] = _log_sigmoid(logit) * (1.0 / GLA_GATE_NORMALIZER)


def _gla_proj(x, w_in, layer, low_w, w_gk, b_gk):
    t, k = x.shape
    n_main = sum(GLA_COLS[:4])
    n_gate = w_gk.shape[1]
    assert GLA_COLS[0] % COL_CHUNK == 0 and sum(GLA_COLS[:3]) % COL_CHUNK == 0
    wgk_pad = jnp.zeros((low_w.shape[2], n_gate), F32).at[:GLA_GATE_RANK].set(w_gk)
    row = lambda i: (i, 0)
    return pl.pallas_call(
        _gla_proj_kernel,
        grid=(t // TOKEN_TILE,),
        in_specs=[pl.BlockSpec((TOKEN_TILE, k), row), _layer_window(layer, k, n_main),
                  _layer_window(0, k, low_w.shape[2]), _resident(wgk_pad.shape), _resident((1, n_gate))],
        out_specs=[pl.BlockSpec((TOKEN_TILE, n_main), row), pl.BlockSpec((TOKEN_TILE, n_gate), row)],
        out_shape=[jax.ShapeDtypeStruct((t, n_main), BF16), jax.ShapeDtypeStruct((t, n_gate), F32)],
        compiler_params=_params("parallel"),
        name="gla_proj",
    )(x, w_in, low_w, wgk_pad, b_gk.reshape(1, n_gate))


def _gla_kernel(q_ref, k_ref, v_ref, g_ref, logg_ref, ng_ref, o_ref, st_ref, *, n_blocks):
    st_ref[...] = jnp.zeros_like(st_ref)
    norm_g = ng_ref[...]

    def body(c, carry):
        rows = pl.ds(pl.multiple_of(c * SUPER, SUPER), SUPER)
        o, st = _gla_block(_per_head(q_ref[rows, :].astype(F32), GLA_DK), _per_head(k_ref[rows, :].astype(F32), GLA_DK),
                           _per_head(v_ref[rows, :], GLA_DV), _per_head(logg_ref[rows, :], GLA_DK), st_ref[...])
        st_ref[...] = st
        _gla_finish(o, norm_g, g_ref[rows, :], o_ref, rows)
        return carry

    lax.fori_loop(0, n_blocks, body, 0)


def _gla_mixer(proj, log_g, norm_g):
    bsz, s, _ = proj.shape
    dk, dv, h, gh = GLA_DK, GLA_DV, GLA_HEADS, MIX_GROUP
    groups = h // gh
    seq = lambda width, off: pl.BlockSpec((None, s, gh * width), lambda b, i: (b, 0, off + i))
    return pl.pallas_call(
        functools.partial(_gla_kernel, n_blocks=s // SUPER),
        grid=(bsz, groups),
        in_specs=[seq(dk, 0), seq(dk, groups), seq(dv, (2 * h * dk) // (gh * dv)),
                  seq(dv, (2 * h * dk) // (gh * dv) + groups), seq(dk, 0),
                  pl.BlockSpec((1, dv), lambda b, i: (0, 0))],
        out_specs=pl.BlockSpec((None, s, gh * dv), lambda b, i: (b, 0, i)),
        out_shape=jax.ShapeDtypeStruct((bsz, s, h * dv), BF16),
        scratch_shapes=[pltpu.VMEM((gh, dv, dk), F32)],
        compiler_params=_params("parallel", "parallel"),
        name="gla",
    )(proj, proj, proj, proj, log_g, norm_g.reshape(1, -1))


def _hgrn_kernel(q_ref, logf_ref, i_ref, g_ref, ng_ref, o_ref, st_ref, *, n_blocks):
    st_ref[...] = jnp.zeros_like(st_ref)
    norm_g = ng_ref[...]

    def body(c, carry):
        rows = pl.ds(pl.multiple_of(c * SUPER, SUPER), SUPER)
        log_f = logf_ref[rows, :]
        o, st = _gla_block(_per_head(q_ref[rows, :].astype(F32), HEAD_DIM), _per_head(1.0 - jnp.exp(log_f), HEAD_DIM),
                           _per_head(i_ref[rows, :], HEAD_DIM), _per_head(log_f, HEAD_DIM), st_ref[...])
        st_ref[...] = st
        _gla_finish(o, norm_g, g_ref[rows, :], o_ref, rows)
        return carry

    lax.fori_loop(0, n_blocks, body, 0)


def _hgrn_mixer(q, log_f, ig, norm_g):
    bsz, s, _ = q.shape
    d, h, gh = HEAD_DIM, HGRN_HEADS, HGRN_GROUP
    groups = h // gh
    seq = lambda off: pl.BlockSpec((None, s, gh * d), lambda b, i: (b, 0, off + i))
    return pl.pallas_call(
        functools.partial(_hgrn_kernel, n_blocks=s // SUPER),
        grid=(bsz, groups),
        in_specs=[seq(0), seq(0), seq(0), seq(groups), pl.BlockSpec((1, d), lambda b, i: (0, 0))],
        out_specs=pl.BlockSpec((None, s, gh * d), lambda b, i: (b, 0, i)),
        out_shape=jax.ShapeDtypeStruct((bsz, s, h * d), BF16),
        scratch_shapes=[pltpu.VMEM((gh, d, d), F32)],
        compiler_params=_params("parallel", "parallel"),
        name="hgrn2",
    )(q, log_f, ig, ig, norm_g.reshape(1, -1))


def _gdn_kernel(q_ref, k_ref, v_ref, z_ref, ab_ref, alog_ref, dtb_ref, ng_ref, o_ref, st_ref, *, n_chunks):
    rn, cs, gh, hd = GDN_ROWS, GDN_CHUNK, GDN_GROUP, HEAD_DIM
    n_sub = rn // cs
    head0 = pl.program_id(1) * gh
    row = lax.broadcasted_iota(jnp.int32, (cs, cs), 0)
    col = lax.broadcasted_iota(jnp.int32, (cs, cs), 1)
    incl, strict, eye = row >= col, row > col, row == col
    eye_f = jnp.where(eye, 1.0, 0.0)
    block_bits = jnp.bitwise_xor(row, col)
    big_row = lax.broadcasted_iota(jnp.int32, (rn, rn), 0)
    big_col = lax.broadcasted_iota(jnp.int32, (rn, rn), 1)
    tril_b = jnp.where(jnp.logical_and(big_col <= big_row, big_col >= jnp.bitwise_and(big_row, -cs)),
                       1.0, 0.0).astype(BF16)
    lane = lax.broadcasted_iota(jnp.int32, (1, LANES), 1)
    norm_g = ng_ref[...]
    neg_a = -jnp.exp(alog_ref[...])
    dt_bias = dtb_ref[...]
    st_ref[...] = jnp.zeros_like(st_ref)

    def per_chunk(x):
        width = x.shape[1] // gh
        return jnp.stack([x[s * cs:(s + 1) * cs, j * width:(j + 1) * width]
                          for j in range(gh) for s in range(n_sub)])

    def pick(x, first):
        return jnp.concatenate([jnp.broadcast_to(jnp.sum(jnp.where(lane == first + j, x, 0.0), axis=1,
                                                         keepdims=True), (rn, hd)) for j in range(gh)], axis=1)

    def of_sub(x, s):
        return jnp.stack([x[j * n_sub + s] for j in range(gh)])

    def body(c, carry):
        rows = pl.ds(pl.multiple_of(c * rn, rn), rn)
        q = per_chunk(q_ref[rows, :].astype(F32))
        k = per_chunk(k_ref[rows, :].astype(F32))
        v = per_chunk(v_ref[rows, :].astype(F32))
        ab = ab_ref[rows, :]
        beta = per_chunk(pick(_sigmoid(ab), head0 + GDN_HEADS))
        g_all = neg_a * _softplus(ab + dt_bias)
        cum_all = sum(jnp.dot(tril_b, part, preferred_element_type=F32) for part in _split3(g_all))
        cum = per_chunk(pick(cum_all, head0))
        cum_row = jnp.sum(jnp.where(eye, cum, 0.0), axis=1, keepdims=True)
        grow = jnp.exp(jnp.where(incl, cum - cum_row, 0.0))
        k_beta = k * beta
        lower = jnp.where(strict, _bdot_nt(k_beta, k) * grow, 0.0)
        base_bits = int(np.log2(GDN_INV_BASE))
        power = -jnp.where(jnp.right_shift(block_bits, base_bits) == 0, lower, 0.0)
        inv = eye_f + power
        for _ in range(base_bits - 1):
            power = _bdot(power, power)
            inv = inv + _bdot(inv, power)
        for bits in range(base_bits, int(np.log2(cs))):
            cross = jnp.where(jnp.right_shift(block_bits, bits) == 1, lower, 0.0)
            inv = inv - _bdot(_bdot(inv, cross), inv)
        e_cum = jnp.exp(cum)
        sol = _bdot(inv, jnp.concatenate([v * beta, k_beta * e_cum], axis=2))
        u, w_qd = sol[:, :, :hd], jnp.concatenate([sol[:, :, hd:], q * e_cum], axis=1)
        qk = jnp.where(incl, _bdot_nt(q, k) * grow, 0.0)
        last = cum[:, cs - 1:cs, :]
        k_end = k * jnp.exp(last - cum)
        carry_decay = jnp.exp(last)
        st = st_ref[...]
        outs = []
        for s in range(n_sub):
            t = _bdot(of_sub(w_qd, s), st)
            v_new = of_sub(u, s) - t[:, :cs, :]
            outs.append(t[:, cs:, :] + _bdot(of_sub(qk, s), v_new))
            ke = of_sub(k_end, s)
            st = st * of_sub(carry_decay, s) + jnp.stack([_dot_tn(ke[j], v_new[j]) for j in range(gh)])
        st_ref[...] = st
        o = jnp.concatenate(outs, axis=1)
        o = o * lax.rsqrt(jnp.mean(o * o, axis=-1, keepdims=True) + 1e-6) * norm_g
        o_ref[rows, :] = (_merge_heads(o) * z_ref[rows, :].astype(F32)).astype(o_ref.dtype)
        return carry

    lax.fori_loop(0, n_chunks, body, 0)


def _gdn_mixer(proj, ab, a_log, dt_bias, norm_g):
    bsz, s, _ = proj.shape
    d, h = HEAD_DIM, GDN_HEADS
    assert GDN_CHUNK == HEAD_DIM and s % GDN_ROWS == 0
    n_chunks = s // GDN_ROWS
    gh = GDN_GROUP
    groups = h // gh
    seq = lambda off: pl.BlockSpec((None, s, gh * d), lambda b, i: (b, 0, off + i))
    vec = pl.BlockSpec((1, LANES), lambda b, i: (0, 0))
    pad = lambda a: jnp.zeros((1, LANES), F32).at[0, :h].set(a)
    return pl.pallas_call(
        functools.partial(_gdn_kernel, n_chunks=n_chunks),
        grid=(bsz, groups),
        in_specs=[seq(0), seq(groups), seq(2 * groups), seq(3 * groups),
                  pl.BlockSpec((None, s, LANES), lambda b, i: (b, 0, 0)), vec, vec, vec],
        out_specs=pl.BlockSpec((None, s, gh * d), lambda b, i: (b, 0, i)),
        out_shape=jax.ShapeDtypeStruct((bsz, s, h * d), BF16),
        scratch_shapes=[pltpu.VMEM((gh, d, d), F32)],
        compiler_params=_params("parallel", "parallel"),
        name="gdn",
    )(proj, proj, proj, proj, ab, pad(a_log), pad(dt_bias), norm_g.reshape(1, -1))


def _rope_table_kernel(pos_ref, invf_ref, cos_ref, sin_ref):
    half = ROPE_DIM // 2
    rest = LANES - ROPE_DIM
    for c0 in range(0, pos_ref.shape[1], LANES):
        ang = invf_ref[...] * pos_ref[:, c0:c0 + LANES].astype(F32)
        c, s = jnp.cos(ang), jnp.sin(ang)
        cos_t = jnp.concatenate([c, c, jnp.ones((rest, LANES), F32)], axis=0)
        sin_t = jnp.concatenate([-s, s, jnp.zeros((rest, LANES), F32)], axis=0)
        cos_ref[c0:c0 + LANES, :] = cos_t.T
        sin_ref[c0:c0 + LANES, :] = sin_t.T


def _rope_tables(positions):
    bsz, s = positions.shape
    half = ROPE_DIM // 2
    inv_freq = (np.float32(ROPE_THETA) ** (-np.arange(0, ROPE_DIM, 2, dtype=np.float32) / ROPE_DIM)).astype(np.float32)
    invf = np.ascontiguousarray(np.broadcast_to(inv_freq[:, None], (half, LANES)))
    out = jax.ShapeDtypeStruct((bsz, s, LANES), F32)
    return pl.pallas_call(
        _rope_table_kernel,
        grid=(bsz,),
        in_specs=[pl.BlockSpec((None, 1, s), lambda b: (b, 0, 0)),
                  pl.BlockSpec((half, LANES), lambda b: (0, 0))],
        out_specs=[pl.BlockSpec((None, s, LANES), lambda b: (b, 0, 0))] * 2,
        out_shape=[out, out],
        compiler_params=_params("parallel"),
        name="rope_table",
    )(positions.reshape(bsz, 1, s), jnp.asarray(invf))


def _moba_kernel(q_ref, k_ref, v_ref, o_ref, kr_s, vb_s, km_s, s_s, *, n_blocks):
    blk, hd, gh = MOBA_BLOCK, HEAD_DIM, MOBA_GROUP
    causal, _, _ = _tri_masks(blk)
    lane = lax.broadcasted_iota(jnp.int32, (1, hd), 1)
    blk_id = lax.broadcasted_iota(jnp.int32, (1, n_blocks, 1), 1)

    for j in range(n_blocks):
        rows = slice(j * blk, (j + 1) * blk)
        kr = _per_head(k_ref[rows, :], hd)
        kr_s[:, rows, :hd] = kr
        kr_s[:, rows, hd:] = jnp.broadcast_to(jnp.where(lane == j, 1.0, 0.0), (gh, blk, hd)).astype(BF16)
        for part, piece in enumerate(_split3(jnp.mean(kr.astype(F32), axis=1, keepdims=True))):
            km_s[:, part * n_blocks + j:part * n_blocks + j + 1, :] = piece
        vb_s[:, rows, :hd] = _per_head(v_ref[rows, :], hd).astype(BF16)
        vb_s[:, rows, hd:] = jnp.ones((gh, blk, hd), BF16)

    for qb in range(n_blocks):
        rows = slice(qb * blk, (qb + 1) * blk)
        qb16 = _per_head(q_ref[rows, :], hd)
        bias = jnp.zeros((gh, blk, hd), BF16)
        if qb > 0:
            parts = _bdot_nt(km_s[...], qb16)
            gate = parts[:, :n_blocks] + parts[:, n_blocks:2 * n_blocks] + parts[:, 2 * n_blocks:]
            gate = jnp.where(blk_id < qb, gate, NEG_INF)
            picked = []
            for j in range(qb):
                gj = gate[:, j:j + 1, :]
                beats = jnp.where(blk_id < j, jnp.where(gate >= gj, 1.0, 0.0), jnp.where(gate > gj, 1.0, 0.0))
                rank = jnp.sum(beats, axis=1, keepdims=True)
                picked.append(jnp.where(rank < MOBA_TOPK, 1.0, 0.0))
            picked.append(jnp.zeros((gh, hd - qb, blk), F32))
            picked = jnp.concatenate(picked, axis=1)
            keep = jnp.stack([picked[g].T for g in range(gh)])
            bias = jnp.where(lane < qb, (1.0 - keep) * NEG_INF, 0.0).astype(BF16)
        width = (qb + 1) * blk
        logits = _bdot_nt(jnp.concatenate([qb16, bias], axis=2), kr_s[:, :width, :])
        if qb > 0:
            s_s[:, :, :qb * blk] = logits[:, :, :qb * blk]
        s_s[:, :, rows] = jnp.where(causal, logits[:, :, qb * blk:], NEG_INF)
        top = s_s[:, :, :LANES]
        for c0 in range(LANES, width, LANES):
            top = jnp.maximum(top, s_s[:, :, c0:c0 + LANES])
        p = jnp.exp2(s_s[:, :, :width] - jnp.max(top, axis=2, keepdims=True))
        pv = _bdot(p, vb_s[:, :width, :])
        o_ref[rows, :] = _merge_heads(pv[:, :, :hd] / pv[:, :, hd:]).astype(o_ref.dtype)


def _moba_mixer(proj):
    bsz, s, _ = proj.shape
    d, h, gh = HEAD_DIM, MOBA_HEADS, MOBA_GROUP
    groups = h // gh
    seq = lambda off: pl.BlockSpec((None, s, gh * d), lambda b, i: (b, 0, off + i))
    return pl.pallas_call(
        functools.partial(_moba_kernel, n_blocks=s // MOBA_BLOCK),
        grid=(bsz, groups),
        in_specs=[seq(0), seq(groups), seq(2 * groups)],
        out_specs=pl.BlockSpec((None, s, gh * d), lambda b, i: (b, 0, i)),
        out_shape=jax.ShapeDtypeStruct((bsz, s, h * d), BF16),
        scratch_shapes=[pltpu.VMEM((gh, s, 2 * d), BF16),
                        pltpu.VMEM((gh, s, 2 * d), BF16),
                        pltpu.VMEM((gh, 3 * (s // MOBA_BLOCK), d), BF16),
                        pltpu.VMEM((gh, MOBA_BLOCK, s), F32)],
        compiler_params=_params("parallel", "parallel"),
        name="moba",
    )(proj, proj, proj)


def _pad_cols(w, multiple=LANES):
    n = w.shape[1]
    return jnp.pad(w, ((0, 0), (0, -n % multiple)))


def kernel(x, positions, gla_w_in, gla_w_gk, gla_b_gk, gla_norm_g, gla_w_out, moba_w_in, moba_w_out,
           gdn_w_in, gdn_conv_w, gdn_a_log, gdn_dt_bias, gdn_norm_g, gdn_w_out, hgrn_lower_bounds,
           hgrn_w_in, hgrn_norm_g, hgrn_w_out, ffn_w_gu, ffn_w_down, ln_g, ln_b):
    bsz, s, d = x.shape
    assert (s % MOBA_BLOCK, (bsz * s) % TOKEN_TILE, d) == (0, 0, D_MODEL)
    t = bsz * s
    xf = x.reshape(t, d)
    cos = sin = None
    for i in range(DEPTH):
        kind, j = i % N_MIXERS, i // N_MIXERS
        tokens = lambda a: a.reshape(bsz, s, -1)
        if kind == 0:
            n_main = sum(GLA_COLS[:4])
            low_w = _pad_cols(gla_w_in[j, :, n_main:])[None]
            proj, log_g = _gla_proj(xf, gla_w_in, j, low_w, gla_w_gk[j], gla_b_gk[j])
            o = _gla_mixer(tokens(proj), tokens(log_g), gla_norm_g[j])
            w_out = gla_w_out
        elif kind == 1:
            if cos is None:
                cos, sin = _rope_tables(positions)
            o = _moba_mixer(tokens(_moba_proj(xf, moba_w_in, j, cos, sin)))
            w_out = moba_w_out
        elif kind == 2:
            n_main = sum(GDN_COLS[:2])
            ab_w = _pad_cols(gdn_w_in[j, :, n_main:])[None]
            proj, ab = _gdn_proj(xf, gdn_w_in, j, ab_w, gdn_conv_w, s)
            o = _gdn_mixer(tokens(proj), tokens(ab), gdn_a_log[j], gdn_dt_bias[j], gdn_norm_g[j])
            w_out = gdn_w_out
        else:
            q, log_f, ig = _hgrn_proj(xf, hgrn_w_in, j, hgrn_lower_bounds, i)
            o = _hgrn_mixer(tokens(q), tokens(log_f), tokens(ig), hgrn_norm_g[j])
            w_out = hgrn_w_out
        xf = _layer_tail(o.reshape(t, d), w_out, j, xf, ffn_w_gu, ffn_w_down, i, ln_g, ln_b)
    return xf.reshape(bsz, s, d)
```

```python
import functools

import numpy as np
import jax
import jax.numpy as jnp
from jax import lax
from jax.experimental import pallas as pl
from jax.experimental.pallas import tpu as pltpu

F32 = jnp.float32
BF16 = jnp.bfloat16

D_MODEL = 1024
DEPTH = 4
N_MIXERS = 4
HEAD_DIM = 128
CHUNK = 64
SUPER = 256
MIX_GROUP = 4
HGRN_GROUP = 8
MOBA_GROUP = 4
GLA_HEADS = 4
GLA_DK = 128
GLA_DV = 256
GLA_GATE_RANK = 16
GLA_GATE_NORMALIZER = 16.0
MOBA_HEADS = 8
MOBA_BLOCK = 256
MOBA_TOPK = 3
ROPE_THETA = 500000.0
ROPE_DIM = 32
NEG_INF = -1e30
LOG2_E = 1.4426950408889634
GDN_HEADS = 8
GDN_CONV = 4
GDN_ROWS = 256
GDN_CHUNK = 128
GDN_INV_BASE = 16
GDN_GROUP = 8
GLA_COLS = (512, 512, 1024, 1024, 16)
GDN_COLS = (3072, 1024, 8, 8)
HGRN_COLS = (1024, 1024, 1024, 1024)
HGRN_HEADS = 8
DEEPNORM_ALPHA = (2.0 * DEPTH) ** 0.25

LANES = 128
BF16_ROWS = 16
VMEM_LIMIT = 56 * 1024 * 1024
TOKEN_TILE = 1024
COL_CHUNK = 512
GDN_PROJ_CHUNK = 256
TAIL_TILE = 512
LN_SUBTILE = 256


def _dot_tn(a, b):
    return lax.dot_general(a.astype(BF16), b.astype(BF16), (((0,), (0,)), ((), ())),
                           preferred_element_type=F32)


def _bdot(a, b):
    return lax.dot_general(a.astype(BF16), b.astype(BF16), (((2,), (1,)), ((0,), (0,))),
                           preferred_element_type=F32)


def _bdot_nt(a, b):
    return lax.dot_general(a.astype(BF16), b.astype(BF16), (((2,), (2,)), ((0,), (0,))),
                           preferred_element_type=F32)


def _sigmoid(x):
    return 1.0 / (1.0 + jnp.exp(-x))


def _silu(x):
    half = 0.5 * x
    return half + half * jnp.tanh(half)


def _softplus(x):
    return jnp.maximum(x, 0.0) + jnp.log1p(jnp.exp(-jnp.abs(x)))


def _log_sigmoid(x):
    return -_softplus(-x)


def _col_chunks(n, width=COL_CHUNK):
    return tuple((c0, min(width, n - c0)) for c0 in range(0, n, width))


def _params(*semantics):
    return pltpu.CompilerParams(dimension_semantics=semantics, vmem_limit_bytes=VMEM_LIMIT)


def _resident(shape):
    zeros = (0,) * len(shape)
    return pl.BlockSpec(shape, lambda *_: zeros, pipeline_mode=pl.Buffered(1))


def _layer_norm(z, g, b):
    mu = jnp.mean(z, axis=-1, keepdims=True)
    zc = z - mu
    var = jnp.mean(zc * zc, axis=-1, keepdims=True)
    return zc * lax.rsqrt(var + 1e-5) * g + b


def _tri_masks(n):
    row = lax.broadcasted_iota(jnp.int32, (n, n), 0)
    col = lax.broadcasted_iota(jnp.int32, (n, n), 1)
    return row >= col, row > col, row == col


def _layer_window(layer, k, n, col_block=0):
    return pl.BlockSpec((None, k, n), lambda *_: (layer, 0, col_block), pipeline_mode=pl.Buffered(1))


def _gdn_proj_kernel(x_ref, w_ref, wab_ref, cw_ref, o_ref, ab_ref, carry_ref, *, tiles_per_seq):
    i = pl.program_id(0)
    tile = x_ref.shape[0]
    hist = BF16_ROWS
    n_conv = cw_ref.shape[1]
    n_norm = 2 * n_conv // 3
    first = (i % tiles_per_seq) == 0

    @pl.when(i == 0)
    def _():
        carry_ref[...] = jnp.zeros_like(carry_ref)

    xb = x_ref[...].astype(BF16)
    for c0, cw in _col_chunks(w_ref.shape[1], GDN_PROJ_CHUNK):
        cols = slice(c0, c0 + cw)
        y = jnp.dot(xb, w_ref[:, cols].astype(BF16), preferred_element_type=F32)
        if c0 < n_conv:
            prev = jnp.where(first, 0.0, carry_ref[:, cols])
            carry_ref[:, cols] = y[tile - hist:, :]
            ext = jnp.concatenate([prev, y], axis=0)
            w = cw_ref[:, cols]
            y = y * w[GDN_CONV - 1:GDN_CONV, :]
            for j in range(GDN_CONV - 1):
                lo = hist - (GDN_CONV - 1) + j
                y = y + ext[lo:lo + tile, :] * w[j:j + 1, :]
        y = _silu(y)
        if c0 < n_norm:
            heads = _per_head(y, HEAD_DIM)
            heads = heads * lax.rsqrt(jnp.sum(heads * heads, axis=-1, keepdims=True) + 1e-6)
            y = _merge_heads(heads * (HEAD_DIM ** -0.5) if c0 < n_norm // 2 else heads)
        o_ref[:, cols] = y.astype(o_ref.dtype)
    ab_ref[...] = jnp.dot(xb, wab_ref[...].astype(BF16), preferred_element_type=F32)


def _gdn_proj(x, w_in, layer, ab_w, conv_w, seq_len):
    t, k = x.shape
    n_main = sum(GDN_COLS[:2])
    n_conv = GDN_COLS[0]
    assert n_conv % GDN_PROJ_CHUNK == 0 and (n_conv // 3) % GDN_PROJ_CHUNK == 0 and seq_len % TOKEN_TILE == 0
    row = lambda i: (i, 0)
    return pl.pallas_call(
        functools.partial(_gdn_proj_kernel, tiles_per_seq=seq_len // TOKEN_TILE),
        grid=(t // TOKEN_TILE,),
        in_specs=[pl.BlockSpec((TOKEN_TILE, k), row), _layer_window(layer, k, n_main),
                  _layer_window(0, k, ab_w.shape[2]), _layer_window(layer, GDN_CONV, n_conv)],
        out_specs=[pl.BlockSpec((TOKEN_TILE, n_main), row), pl.BlockSpec((TOKEN_TILE, ab_w.shape[2]), row)],
        out_shape=[jax.ShapeDtypeStruct((t, n_main), BF16), jax.ShapeDtypeStruct((t, ab_w.shape[2]), F32)],
        scratch_shapes=[pltpu.VMEM((BF16_ROWS, n_conv), F32)],
        compiler_params=_params("arbitrary"),
        name="gdn_proj",
    )(x, w_in, ab_w, conv_w)


def _moba_proj_kernel(x_ref, w_ref, cos_ref, sin_ref, o_ref):
    xb = x_ref[...].astype(BF16)
    half = ROPE_DIM // 2
    n_qk = 2 * (w_ref.shape[1] // 3)
    for c0, cw in _col_chunks(w_ref.shape[1]):
        cols = slice(c0, c0 + cw)
        y = jnp.dot(xb, w_ref[:, cols].astype(BF16), preferred_element_type=F32)
        if c0 < n_qk:
            head_lane = jnp.bitwise_and(lax.broadcasted_iota(jnp.int32, (1, cw), 1), HEAD_DIM - 1)
            partner = jnp.where(head_lane < half, pltpu.roll(y, cw - half, 1), pltpu.roll(y, half, 1))
            reps = cw // HEAD_DIM
            y = (y * jnp.concatenate([cos_ref[...]] * reps, axis=1)
                 + partner * jnp.concatenate([sin_ref[...]] * reps, axis=1))
            if c0 < n_qk // 2:
                y = y * (HEAD_DIM ** -0.5 * LOG2_E)
        o_ref[:, cols] = y.astype(o_ref.dtype)


def _moba_proj(x, w_in, layer, cos, sin):
    t, k = x.shape
    n = w_in.shape[2]
    assert (n // 3) % COL_CHUNK == 0
    row = lambda i: (i, 0)
    tab = pl.BlockSpec((TOKEN_TILE, LANES), row)
    return pl.pallas_call(
        _moba_proj_kernel,
        grid=(t // TOKEN_TILE,),
        in_specs=[pl.BlockSpec((TOKEN_TILE, k), row), _layer_window(layer, k, n), tab, tab],
        out_specs=pl.BlockSpec((TOKEN_TILE, n), row),
        out_shape=jax.ShapeDtypeStruct((t, n), BF16),
        compiler_params=_params("parallel"),
        name="moba_proj",
    )(x, w_in, cos.reshape(t, LANES), sin.reshape(t, LANES))


def _hgrn_proj_kernel(x_ref, wq_ref, wf_ref, wig_ref, lb_ref, q_ref, logf_ref, ig_ref, *, layer):
    xb = x_ref[...].astype(BF16)
    raw = lb_ref[...]
    e = jnp.exp(raw - jnp.max(raw, axis=0, keepdims=True))
    soft = e / jnp.sum(e, axis=0, keepdims=True)
    lb = jnp.sum(soft[:layer + 1, :], axis=0, keepdims=True) - soft[0:1, :]
    for c0, cw in _col_chunks(wq_ref.shape[1]):
        cols = slice(c0, c0 + cw)
        q_ref[:, cols] = jnp.dot(xb, wq_ref[:, cols].astype(BF16), preferred_element_type=F32).astype(q_ref.dtype)
        f = jnp.dot(xb, wf_ref[:, cols].astype(BF16), preferred_element_type=F32)
        lbc = lb[:, cols]
        logf_ref[:, cols] = jnp.log(lbc + (1.0 - lbc) * _sigmoid(f))
    n_i = wig_ref.shape[1] // 2
    for c0, cw in _col_chunks(wig_ref.shape[1]):
        cols = slice(c0, c0 + cw)
        y = jnp.dot(xb, wig_ref[:, cols].astype(BF16), preferred_element_type=F32)
        ig_ref[:, cols] = (_silu(y) if c0 >= n_i else y).astype(ig_ref.dtype)


def _hgrn_proj(x, w_in, layer, lower_bounds, depth_index):
    t, k = x.shape
    nq, nf, n_ig = HGRN_COLS[0], HGRN_COLS[1], sum(HGRN_COLS[2:])
    assert nq == nf and n_ig == 2 * nq and nq % COL_CHUNK == 0
    row = lambda i: (i, 0)
    out = lambda n: pl.BlockSpec((TOKEN_TILE, n), row)
    return pl.pallas_call(
        functools.partial(_hgrn_proj_kernel, layer=depth_index),
        grid=(t // TOKEN_TILE,),
        in_specs=[pl.BlockSpec((TOKEN_TILE, k), row), _layer_window(layer, k, nq, 0), _layer_window(layer, k, nf, 1),
                  _layer_window(layer, k, n_ig, 1), _resident(lower_bounds.shape)],
        out_specs=[out(nq), out(nf), out(n_ig)],
        out_shape=[jax.ShapeDtypeStruct((t, nq), BF16), jax.ShapeDtypeStruct((t, nf), F32),
                   jax.ShapeDtypeStruct((t, n_ig), BF16)],
        compiler_params=_params("parallel"),
        name="hgrn_proj",
    )(x, w_in, w_in, w_in, lower_bounds)


def _tail_kernel(o_ref, wo_ref, x_ref, g1_ref, b1_ref, wgu_ref, wd_ref, g2_ref, b2_ref, y_ref, h_ref, *, hidden):
    tile = y_ref.shape[0]
    wo = wo_ref[...].astype(BF16)
    for r0 in range(0, tile, LN_SUBTILE):
        rows = slice(r0, r0 + LN_SUBTILE)
        y = jnp.dot(o_ref[rows, :], wo, preferred_element_type=F32)
        y_ref[rows, :] = _layer_norm(DEEPNORM_ALPHA * x_ref[rows, :] + y, g1_ref[...], b1_ref[...])
    xb = y_ref[...].astype(BF16)
    for c0, cw in _col_chunks(hidden, 256):
        gate = jnp.dot(xb, wgu_ref[:, c0:c0 + cw].astype(BF16), preferred_element_type=F32)
        up = jnp.dot(xb, wgu_ref[:, hidden + c0:hidden + c0 + cw].astype(BF16), preferred_element_type=F32)
        h_ref[:, c0:c0 + cw] = (_silu(gate) * up).astype(BF16)
    for r0 in range(0, tile, LN_SUBTILE):
        rows = slice(r0, r0 + LN_SUBTILE)
        down = jnp.concatenate([jnp.dot(h_ref[rows, :], wd_ref[:, c0:c0 + cw].astype(BF16),
                                        preferred_element_type=F32) for c0, cw in _col_chunks(y_ref.shape[1])], axis=1)
        y_ref[rows, :] = _layer_norm(DEEPNORM_ALPHA * y_ref[rows, :] + down, g2_ref[...], b2_ref[...])


def _layer_tail(o, w_out, out_layer, x, w_gu, w_down, layer, ln_g, ln_b):
    t, k = o.shape
    d = x.shape[1]
    hidden = w_down.shape[1]
    row = lambda i: (i, 0)
    vec = lambda a: a.reshape(1, d)
    return pl.pallas_call(
        functools.partial(_tail_kernel, hidden=hidden),
        grid=(t // TAIL_TILE,),
        in_specs=[pl.BlockSpec((TAIL_TILE, k), row), _layer_window(out_layer, k, d),
                  pl.BlockSpec((TAIL_TILE, d), row), _resident((1, d)), _resident((1, d)),
                  _layer_window(layer, d, 2 * hidden), _layer_window(layer, hidden, d),
                  _resident((1, d)), _resident((1, d))],
        out_specs=pl.BlockSpec((TAIL_TILE, d), row),
        out_shape=jax.ShapeDtypeStruct((t, d), F32),
        scratch_shapes=[pltpu.VMEM((TAIL_TILE, hidden), BF16)],
        compiler_params=_params("parallel"),
        name="layer_tail",
    )(o, w_out, x, vec(ln_g[layer, 0]), vec(ln_b[layer, 0]), w_gu, w_down, vec(ln_g[layer, 1]), vec(ln_b[layer, 1]))


def _split3(x):
    hi = x.astype(BF16)
    rest = x - hi.astype(F32)
    mid = rest.astype(BF16)
    return hi, mid, (rest - mid.astype(F32)).astype(BF16)


def _per_head(x, width):
    return jnp.stack([x[:, j * width:(j + 1) * width] for j in range(x.shape[1] // width)])


def _merge_heads(x):
    return jnp.concatenate([x[j] for j in range(x.shape[0])], axis=1)


def _gla_block(q, k, v, log_g, st):
    g, rows, dk = q.shape
    n = rows // CHUNK
    row = lax.broadcasted_iota(jnp.int32, (rows, rows), 0)
    col = lax.broadcasted_iota(jnp.int32, (rows, rows), 1)
    chunk_start = jnp.bitwise_and(row, -CHUNK)
    causal = jnp.logical_and(col <= row, col >= chunk_start)
    parts = [p for j in range(g) for p in _split3(log_g[j])]
    acc = jnp.dot(jnp.where(causal, 1.0, 0.0).astype(BF16), jnp.concatenate(parts, axis=1),
                  preferred_element_type=F32)
    cum = jnp.stack([acc[:, 3 * j * dk:(3 * j + 1) * dk] + acc[:, (3 * j + 1) * dk:(3 * j + 2) * dk]
                     + acc[:, (3 * j + 2) * dk:(3 * j + 3) * dk] for j in range(g)])
    lasts = [cum[:, (i + 1) * CHUNK - 1:(i + 1) * CHUNK, :] for i in range(n)]
    total = jnp.concatenate([jnp.broadcast_to(last, (g, CHUNK, dk)) for last in lasts], axis=1)
    q_dec = (q * jnp.exp(cum)).astype(BF16)
    k_inv = k * jnp.exp(-cum)
    k_end = (k * jnp.exp(total - cum)).astype(BF16)
    vb = v.astype(BF16)
    scores = jnp.where(causal, _bdot_nt(q_dec, k_inv), 0.0)
    o_intra = _bdot(scores, vb)
    o_inter = []
    for i in range(n):
        rs = slice(i * CHUNK, (i + 1) * CHUNK)
        o_inter.append(_bdot_nt(q_dec[:, rs, :], st))
        update = jnp.stack([_dot_tn(vb[j, rs, :], k_end[j, rs, :]) for j in range(g)])
        st = st * jnp.exp(lasts[i]) + update
    return o_intra + jnp.concatenate(o_inter, axis=1), st


def _gla_finish(o, norm_g, gate, o_ref, rows):
    o = o * lax.rsqrt(jnp.mean(o * o, axis=-1, keepdims=True) + 1e-6) * norm_g
    o_ref[rows, :] = (_merge_heads(o) * gate.astype(F32)).astype(o_ref.dtype)


def _gla_proj_kernel(x_ref, w_ref, wlow_ref, wgk_ref, bgk_ref, o_ref, logg_ref):
    xb = x_ref[...].astype(BF16)
    n_q = GLA_COLS[0]
    n_qkv = sum(GLA_COLS[:3])
    for c0, cw in _col_chunks(w_ref.shape[1]):
        cols = slice(c0, c0 + cw)
        y = jnp.dot(xb, w_ref[:, cols].astype(BF16), preferred_element_type=F32)
        if c0 < n_q:
            y = y * (GLA_DK ** -0.5)
        elif c0 >= n_qkv:
            y = _silu(y)
        o_ref[:, cols] = y.astype(o_ref.dtype)
    low = jnp.dot(xb, wlow_ref[...].astype(BF16), preferred_element_type=F32)
    logit = jnp.dot(low.astype(BF16), wgk_ref[...].astype(BF16), preferred_element_type=F32) + bgk_ref[...]
    logg_ref[...] = _log_sigmoid(logit) * (1.0 / GLA_GATE_NORMALIZER)


def _gla_proj(x, w_in, layer, low_w, w_gk, b_gk):
    t, k = x.shape
    n_main = sum(GLA_COLS[:4])
    n_gate = w_gk.shape[1]
    assert GLA_COLS[0] % COL_CHUNK == 0 and sum(GLA_COLS[:3]) % COL_CHUNK == 0
    wgk_pad = jnp.zeros((low_w.shape[2], n_gate), F32).at[:GLA_GATE_RANK].set(w_gk)
    row = lambda i: (i, 0)
    return pl.pallas_call(
        _gla_proj_kernel,
        grid=(t // TOKEN_TILE,),
        in_specs=[pl.BlockSpec((TOKEN_TILE, k), row), _layer_window(layer, k, n_main),
                  _layer_window(0, k, low_w.shape[2]), _resident(wgk_pad.shape), _resident((1, n_gate))],
        out_specs=[pl.BlockSpec((TOKEN_TILE, n_main), row), pl.BlockSpec((TOKEN_TILE, n_gate), row)],
        out_shape=[jax.ShapeDtypeStruct((t, n_main), BF16), jax.ShapeDtypeStruct((t, n_gate), F32)],
        compiler_params=_params("parallel"),
        name="gla_proj",
    )(x, w_in, low_w, wgk_pad, b_gk.reshape(1, n_gate))


def _gla_kernel(q_ref, k_ref, v_ref, g_ref, logg_ref, ng_ref, o_ref, st_ref, *, n_blocks):
    st_ref[...] = jnp.zeros_like(st_ref)
    norm_g = ng_ref[...]

    def body(c, carry):
        rows = pl.ds(pl.multiple_of(c * SUPER, SUPER), SUPER)
        o, st = _gla_block(_per_head(q_ref[rows, :].astype(F32), GLA_DK), _per_head(k_ref[rows, :].astype(F32), GLA_DK),
                           _per_head(v_ref[rows, :], GLA_DV), _per_head(logg_ref[rows, :], GLA_DK), st_ref[...])
        st_ref[...] = st
        _gla_finish(o, norm_g, g_ref[rows, :], o_ref, rows)
        return carry

    lax.fori_loop(0, n_blocks, body, 0)


def _gla_mixer(proj, log_g, norm_g):
    bsz, s, _ = proj.shape
    dk, dv, h, gh = GLA_DK, GLA_DV, GLA_HEADS, MIX_GROUP
    groups = h // gh
    seq = lambda width, off: pl.BlockSpec((None, s, gh * width), lambda b, i: (b, 0, off + i))
    return pl.pallas_call(
        functools.partial(_gla_kernel, n_blocks=s // SUPER),
        grid=(bsz, groups),
        in_specs=[seq(dk, 0), seq(dk, groups), seq(dv, (2 * h * dk) // (gh * dv)),
                  seq(dv, (2 * h * dk) // (gh * dv) + groups), seq(dk, 0),
                  pl.BlockSpec((1, dv), lambda b, i: (0, 0))],
        out_specs=pl.BlockSpec((None, s, gh * dv), lambda b, i: (b, 0, i)),
        out_shape=jax.ShapeDtypeStruct((bsz, s, h * dv), BF16),
        scratch_shapes=[pltpu.VMEM((gh, dv, dk), F32)],
        compiler_params=_params("parallel", "parallel"),
        name="gla",
    )(proj, proj, proj, proj, log_g, norm_g.reshape(1, -1))


def _hgrn_kernel(q_ref, logf_ref, i_ref, g_ref, ng_ref, o_ref, st_ref, *, n_blocks):
    st_ref[...] = jnp.zeros_like(st_ref)
    norm_g = ng_ref[...]

    def body(c, carry):
        rows = pl.ds(pl.multiple_of(c * SUPER, SUPER), SUPER)
        log_f = logf_ref[rows, :]
        o, st = _gla_block(_per_head(q_ref[rows, :].astype(F32), HEAD_DIM), _per_head(1.0 - jnp.exp(log_f), HEAD_DIM),
                           _per_head(i_ref[rows, :], HEAD_DIM), _per_head(log_f, HEAD_DIM), st_ref[...])
        st_ref[...] = st
        _gla_finish(o, norm_g, g_ref[rows, :], o_ref, rows)
        return carry

    lax.fori_loop(0, n_blocks, body, 0)


def _hgrn_mixer(q, log_f, ig, norm_g):
    bsz, s, _ = q.shape
    d, h, gh = HEAD_DIM, HGRN_HEADS, HGRN_GROUP
    groups = h // gh
    seq = lambda off: pl.BlockSpec((None, s, gh * d), lambda b, i: (b, 0, off + i))
    return pl.pallas_call(
        functools.partial(_hgrn_kernel, n_blocks=s // SUPER),
        grid=(bsz, groups),
        in_specs=[seq(0), seq(0), seq(0), seq(groups), pl.BlockSpec((1, d), lambda b, i: (0, 0))],
        out_specs=pl.BlockSpec((None, s, gh * d), lambda b, i: (b, 0, i)),
        out_shape=jax.ShapeDtypeStruct((bsz, s, h * d), BF16),
        scratch_shapes=[pltpu.VMEM((gh, d, d), F32)],
        compiler_params=_params("parallel", "parallel"),
        name="hgrn2",
    )(q, log_f, ig, ig, norm_g.reshape(1, -1))


def _gdn_kernel(q_ref, k_ref, v_ref, z_ref, ab_ref, alog_ref, dtb_ref, ng_ref, o_ref, st_ref, *, n_chunks):
    rn, cs, gh, hd = GDN_ROWS, GDN_CHUNK, GDN_GROUP, HEAD_DIM
    n_sub = rn // cs
    head0 = pl.program_id(1) * gh
    row = lax.broadcasted_iota(jnp.int32, (cs, cs), 0)
    col = lax.broadcasted_iota(jnp.int32, (cs, cs), 1)
    incl, strict, eye = row >= col, row > col, row == col
    eye_f = jnp.where(eye, 1.0, 0.0)
    block_bits = jnp.bitwise_xor(row, col)
    big_row = lax.broadcasted_iota(jnp.int32, (rn, rn), 0)
    big_col = lax.broadcasted_iota(jnp.int32, (rn, rn), 1)
    tril_b = jnp.where(jnp.logical_and(big_col <= big_row, big_col >= jnp.bitwise_and(big_row, -cs)),
                       1.0, 0.0).astype(BF16)
    lane = lax.broadcasted_iota(jnp.int32, (1, LANES), 1)
    norm_g = ng_ref[...]
    neg_a = -jnp.exp(alog_ref[...])
    dt_bias = dtb_ref[...]
    st_ref[...] = jnp.zeros_like(st_ref)

    def per_chunk(x):
        width = x.shape[1] // gh
        return jnp.stack([x[s * cs:(s + 1) * cs, j * width:(j + 1) * width]
                          for j in range(gh) for s in range(n_sub)])

    def pick(x, first):
        return jnp.concatenate([jnp.broadcast_to(jnp.sum(jnp.where(lane == first + j, x, 0.0), axis=1,
                                                         keepdims=True), (rn, hd)) for j in range(gh)], axis=1)

    def of_sub(x, s):
        return jnp.stack([x[j * n_sub + s] for j in range(gh)])

    def body(c, carry):
        rows = pl.ds(pl.multiple_of(c * rn, rn), rn)
        q = per_chunk(q_ref[rows, :].astype(F32))
        k = per_chunk(k_ref[rows, :].astype(F32))
        v = per_chunk(v_ref[rows, :].astype(F32))
        ab = ab_ref[rows, :]
        beta = per_chunk(pick(_sigmoid(ab), head0 + GDN_HEADS))
        g_all = neg_a * _softplus(ab + dt_bias)
        cum_all = sum(jnp.dot(tril_b, part, preferred_element_type=F32) for part in _split3(g_all))
        cum = per_chunk(pick(cum_all, head0))
        cum_row = jnp.sum(jnp.where(eye, cum, 0.0), axis=1, keepdims=True)
        grow = jnp.exp(jnp.where(incl, cum - cum_row, 0.0))
        k_beta = k * beta
        lower = jnp.where(strict, _bdot_nt(k_beta, k) * grow, 0.0)
        base_bits = int(np.log2(GDN_INV_BASE))
        power = -jnp.where(jnp.right_shift(block_bits, base_bits) == 0, lower, 0.0)
        inv = eye_f + power
        for _ in range(base_bits - 1):
            power = _bdot(power, power)
            inv = inv + _bdot(inv, power)
        for bits in range(base_bits, int(np.log2(cs))):
            cross = jnp.where(jnp.right_shift(block_bits, bits) == 1, lower, 0.0)
            inv = inv - _bdot(_bdot(inv, cross), inv)
        e_cum = jnp.exp(cum)
        sol = _bdot(inv, jnp.concatenate([v * beta, k_beta * e_cum], axis=2))
        u, w_qd = sol[:, :, :hd], jnp.concatenate([sol[:, :, hd:], q * e_cum], axis=1)
        qk = jnp.where(incl, _bdot_nt(q, k) * grow, 0.0)
        last = cum[:, cs - 1:cs, :]
        k_end = k * jnp.exp(last - cum)
        carry_decay = jnp.exp(last)
        st = st_ref[...]
        outs = []
        for s in range(n_sub):
            t = _bdot(of_sub(w_qd, s), st)
            v_new = of_sub(u, s) - t[:, :cs, :]
            outs.append(t[:, cs:, :] + _bdot(of_sub(qk, s), v_new))
            ke = of_sub(k_end, s)
            st = st * of_sub(carry_decay, s) + jnp.stack([_dot_tn(ke[j], v_new[j]) for j in range(gh)])
        st_ref[...] = st
        o = jnp.concatenate(outs, axis=1)
        o = o * lax.rsqrt(jnp.mean(o * o, axis=-1, keepdims=True) + 1e-6) * norm_g
        o_ref[rows, :] = (_merge_heads(o) * z_ref[rows, :].astype(F32)).astype(o_ref.dtype)
        return carry

    lax.fori_loop(0, n_chunks, body, 0)


def _gdn_mixer(proj, ab, a_log, dt_bias, norm_g):
    bsz, s, _ = proj.shape
    d, h = HEAD_DIM, GDN_HEADS
    assert GDN_CHUNK == HEAD_DIM and s % GDN_ROWS == 0
    n_chunks = s // GDN_ROWS
    gh = GDN_GROUP
    groups = h // gh
    seq = lambda off: pl.BlockSpec((None, s, gh * d), lambda b, i: (b, 0, off + i))
    vec = pl.BlockSpec((1, LANES), lambda b, i: (0, 0))
    pad = lambda a: jnp.zeros((1, LANES), F32).at[0, :h].set(a)
    return pl.pallas_call(
        functools.partial(_gdn_kernel, n_chunks=n_chunks),
        grid=(bsz, groups),
        in_specs=[seq(0), seq(groups), seq(2 * groups), seq(3 * groups),
                  pl.BlockSpec((None, s, LANES), lambda b, i: (b, 0, 0)), vec, vec, vec],
        out_specs=pl.BlockSpec((None, s, gh * d), lambda b, i: (b, 0, i)),
        out_shape=jax.ShapeDtypeStruct((bsz, s, h * d), BF16),
        scratch_shapes=[pltpu.VMEM((gh, d, d), F32)],
        compiler_params=_params("parallel", "parallel"),
        name="gdn",
    )(proj, proj, proj, proj, ab, pad(a_log), pad(dt_bias), norm_g.reshape(1, -1))


def _rope_table_kernel(pos_ref, invf_ref, cos_ref, sin_ref):
    half = ROPE_DIM // 2
    rest = LANES - ROPE_DIM
    for c0 in range(0, pos_ref.shape[1], LANES):
        ang = invf_ref[...] * pos_ref[:, c0:c0 + LANES].astype(F32)
        c, s = jnp.cos(ang), jnp.sin(ang)
        cos_t = jnp.concatenate([c, c, jnp.ones((rest, LANES), F32)], axis=0)
        sin_t = jnp.concatenate([-s, s, jnp.zeros((rest, LANES), F32)], axis=0)
        cos_ref[c0:c0 + LANES, :] = cos_t.T
        sin_ref[c0:c0 + LANES, :] = sin_t.T


def _rope_tables(positions):
    bsz, s = positions.shape
    half = ROPE_DIM // 2
    inv_freq = (np.float32(ROPE_THETA) ** (-np.arange(0, ROPE_DIM, 2, dtype=np.float32) / ROPE_DIM)).astype(np.float32)
    invf = np.ascontiguousarray(np.broadcast_to(inv_freq[:, None], (half, LANES)))
    out = jax.ShapeDtypeStruct((bsz, s, LANES), F32)
    return pl.pallas_call(
        _rope_table_kernel,
        grid=(bsz,),
        in_specs=[pl.BlockSpec((None, 1, s), lambda b: (b, 0, 0)),
                  pl.BlockSpec((half, LANES), lambda b: (0, 0))],
        out_specs=[pl.BlockSpec((None, s, LANES), lambda b: (b, 0, 0))] * 2,
        out_shape=[out, out],
        compiler_params=_params("parallel"),
        name="rope_table",
    )(positions.reshape(bsz, 1, s), jnp.asarray(invf))


def _moba_kernel(q_ref, k_ref, v_ref, o_ref, kr_s, vb_s, km_s, s_s, *, n_blocks):
    blk, hd, gh = MOBA_BLOCK, HEAD_DIM, MOBA_GROUP
    causal, _, _ = _tri_masks(blk)
    lane = lax.broadcasted_iota(jnp.int32, (1, hd), 1)
    blk_id = lax.broadcasted_iota(jnp.int32, (1, n_blocks, 1), 1)

    for j in range(n_blocks):
        rows = slice(j * blk, (j + 1) * blk)
        kr = _per_head(k_ref[rows, :], hd)
        kr_s[:, rows, :hd] = kr
        kr_s[:, rows, hd:] = jnp.broadcast_to(jnp.where(lane == j, 1.0, 0.0), (gh, blk, hd)).astype(BF16)
        for part, piece in enumerate(_split3(jnp.mean(kr.astype(F32), axis=1, keepdims=True))):
            km_s[:, part * n_blocks + j:part * n_blocks + j + 1, :] = piece
        vb_s[:, rows, :hd] = _per_head(v_ref[rows, :], hd).astype(BF16)
        vb_s[:, rows, hd:] = jnp.ones((gh, blk, hd), BF16)

    for qb in range(n_blocks):
        rows = slice(qb * blk, (qb + 1) * blk)
        qb16 = _per_head(q_ref[rows, :], hd)
        bias = jnp.zeros((gh, blk, hd), BF16)
        if qb > 0:
            parts = _bdot_nt(km_s[...], qb16)
            gate = parts[:, :n_blocks] + parts[:, n_blocks:2 * n_blocks] + parts[:, 2 * n_blocks:]
            gate = jnp.where(blk_id < qb, gate, NEG_INF)
            picked = []
            for j in range(qb):
                gj = gate[:, j:j + 1, :]
                beats = jnp.where(blk_id < j, jnp.where(gate >= gj, 1.0, 0.0), jnp.where(gate > gj, 1.0, 0.0))
                rank = jnp.sum(beats, axis=1, keepdims=True)
                picked.append(jnp.where(rank < MOBA_TOPK, 1.0, 0.0))
            picked.append(jnp.zeros((gh, hd - qb, blk), F32))
            picked = jnp.concatenate(picked, axis=1)
            keep = jnp.stack([picked[g].T for g in range(gh)])
            bias = jnp.where(lane < qb, (1.0 - keep) * NEG_INF, 0.0).astype(BF16)
        width = (qb + 1) * blk
        logits = _bdot_nt(jnp.concatenate([qb16, bias], axis=2), kr_s[:, :width, :])
        if qb > 0:
            s_s[:, :, :qb * blk] = logits[:, :, :qb * blk]
        s_s[:, :, rows] = jnp.where(causal, logits[:, :, qb * blk:], NEG_INF)
        top = s_s[:, :, :LANES]
        for c0 in range(LANES, width, LANES):
            top = jnp.maximum(top, s_s[:, :, c0:c0 + LANES])
        p = jnp.exp2(s_s[:, :, :width] - jnp.max(top, axis=2, keepdims=True))
        pv = _bdot(p, vb_s[:, :width, :])
        o_ref[rows, :] = _merge_heads(pv[:, :, :hd] / pv[:, :, hd:]).astype(o_ref.dtype)


def _moba_mixer(proj):
    bsz, s, _ = proj.shape
    d, h, gh = HEAD_DIM, MOBA_HEADS, MOBA_GROUP
    groups = h // gh
    seq = lambda off: pl.BlockSpec((None, s, gh * d), lambda b, i: (b, 0, off + i))
    return pl.pallas_call(
        functools.partial(_moba_kernel, n_blocks=s // MOBA_BLOCK),
        grid=(bsz, groups),
        in_specs=[seq(0), seq(groups), seq(2 * groups)],
        out_specs=pl.BlockSpec((None, s, gh * d), lambda b, i: (b, 0, i)),
        out_shape=jax.ShapeDtypeStruct((bsz, s, h * d), BF16),
        scratch_shapes=[pltpu.VMEM((gh, s, 2 * d), BF16),
                        pltpu.VMEM((gh, s, 2 * d), BF16),
                        pltpu.VMEM((gh, 3 * (s // MOBA_BLOCK), d), BF16),
                        pltpu.VMEM((gh, MOBA_BLOCK, s), F32)],
        compiler_params=_params("parallel", "parallel"),
        name="moba",
    )(proj, proj, proj)


def _pad_cols(w, multiple=LANES):
    n = w.shape[1]
    return jnp.pad(w, ((0, 0), (0, -n % multiple)))


def kernel(x, positions, gla_w_in, gla_w_gk, gla_b_gk, gla_norm_g, gla_w_out, moba_w_in, moba_w_out,
           gdn_w_in, gdn_conv_w, gdn_a_log, gdn_dt_bias, gdn_norm_g, gdn_w_out, hgrn_lower_bounds,
           hgrn_w_in, hgrn_norm_g, hgrn_w_out, ffn_w_gu, ffn_w_down, ln_g, ln_b):
    bsz, s, d = x.shape
    assert (s % MOBA_BLOCK, (bsz * s) % TOKEN_TILE, d) == (0, 0, D_MODEL)
    t = bsz * s
    xf = x.reshape(t, d)
    cos = sin = None
    for i in range(DEPTH):
        kind, j = i % N_MIXERS, i // N_MIXERS
        tokens = lambda a: a.reshape(bsz, s, -1)
        if kind == 0:
            n_main = sum(GLA_COLS[:4])
            low_w = _pad_cols(gla_w_in[j, :, n_main:])[None]
            proj, log_g = _gla_proj(xf, gla_w_in, j, low_w, gla_w_gk[j], gla_b_gk[j])
            o = _gla_mixer(tokens(proj), tokens(log_g), gla_norm_g[j])
            w_out = gla_w_out
        elif kind == 1:
            if cos is None:
                cos, sin = _rope_tables(positions)
            o = _moba_mixer(tokens(_moba_proj(xf, moba_w_in, j, cos, sin)))
            w_out = moba_w_out
        elif kind == 2:
            n_main = sum(GDN_COLS[:2])
            ab_w = _pad_cols(gdn_w_in[j, :, n_main:])[None]
            proj, ab = _gdn_proj(xf, gdn_w_in, j, ab_w, gdn_conv_w, s)
            o = _gdn_mixer(tokens(proj), tokens(ab), gdn_a_log[j], gdn_dt_bias[j], gdn_norm_g[j])
            w_out = gdn_w_out
        else:
            q, log_f, ig = _hgrn_proj(xf, hgrn_w_in, j, hgrn_lower_bounds, i)
            o = _hgrn_mixer(tokens(q), tokens(log_f), tokens(ig), hgrn_norm_g[j])
            w_out = hgrn_w_out
        xf = _layer_tail(o.reshape(t, d), w_out, j, xf, ffn_w_gu, ffn_w_down, i, ln_g, ln_b)
    return xf.reshape(bsz, s, d)
```
